```python
import math
import jax, jax.numpy as jnp
from jax import lax
import numpy as np

D_MODEL = 2048
BATCH = 4
SEQ = 2048
DEPTH = 2

N_MIXERS = 2
N_A = (DEPTH + 1) // 2
N_B = DEPTH // 2
S5_WIDTH = D_MODEL
S5_GROUP = 16
S5_GROUPS = S5_WIDTH // S5_GROUP
S5_STATE = 64
S5_DT_MIN = 1e-3
S5_DT_MAX = 1e-1
LRU_WIDTH = ((4 * D_MODEL // 3 + 255) // 256) * 256
LRU_BLOCKS = 16
LRU_BLOCK = LRU_WIDTH // LRU_BLOCKS
LRU_C = 8.0
CONV_WIDTH = 4
FFN_HIDDEN = ((8 * D_MODEL // 3 + 255) // 256) * 256
N_MOD = 6
EPS = 1e-6

kernel_name = "adaln_hybrid_s5_rglru_swiglu"


def rmsnorm(x, gain):
    xf = x.astype(jnp.float32)
    y = xf * lax.rsqrt(jnp.mean(xf * xf, axis=-1, keepdims=True) + EPS) * gain.astype(jnp.float32)
    return y.astype(x.dtype)


def _complex_affine_combine(left, right):
    a1r, a1i, b1r, b1i = left
    a2r, a2i, b2r, b2i = right
    return (a2r * a1r - a2i * a1i,
            a2r * a1i + a2i * a1r,
            a2r * b1r - a2i * b1i + b2r,
            a2r * b1i + a2i * b1r + b2i)


def _real_affine_combine(left, right):
    a1, b1 = left
    a2, b2 = right
    return (a2 * a1, a2 * b1 + b2)


def s5_mixer(h, w_in, lam_re, lam_im, log_dt, b_re, b_im, c_re, c_im, d_skip, w_glu):
    f32 = jnp.float32
    bsz, seq, _ = h.shape
    u = h @ w_in
    uf = u.astype(f32)
    ug = uf.reshape(bsz, seq, S5_GROUPS, S5_GROUP)
    dt = jnp.exp(log_dt.astype(f32))[:, None]
    lr = lam_re.astype(f32)
    li = lam_im.astype(f32)
    mag = jnp.exp(lr * dt)
    ab_re = mag * jnp.cos(li * dt)
    ab_im = mag * jnp.sin(li * dt)
    nr, ni = ab_re - 1.0, ab_im
    den = lr * lr + li * li
    f_re = (nr * lr + ni * li) / den
    f_im = (ni * lr - nr * li) / den
    br, bi = b_re.astype(f32), b_im.astype(f32)
    bb_re = f_re[..., None] * br - f_im[..., None] * bi
    bb_im = f_re[..., None] * bi + f_im[..., None] * br
    bu_re = jnp.einsum('blgc,gpc->blgp', ug, bb_re)
    bu_im = jnp.einsum('blgc,gpc->blgp', ug, bb_im)
    a_re = jnp.broadcast_to(ab_re[None, None], (1, seq, S5_GROUPS, S5_STATE))
    a_im = jnp.broadcast_to(ab_im[None, None], (1, seq, S5_GROUPS, S5_STATE))
    _, _, s_re, s_im = lax.associative_scan(
        _complex_affine_combine, (a_re, a_im, bu_re, bu_im), axis=1)
    y = (jnp.einsum('blgp,gcp->blgc', s_re, c_re.astype(f32))
         - jnp.einsum('blgp,gcp->blgc', s_im, c_im.astype(f32)))
    y = y.reshape(bsz, seq, S5_WIDTH) + d_skip.astype(f32) * uf
    y = jax.nn.gelu(y).astype(h.dtype)
    val, gate = jnp.split(y @ w_glu, 2, axis=-1)
    return val * jax.nn.sigmoid(gate)


def causal_depthwise_conv(x, w, b):
    y = lax.conv_general_dilated(
        x, w.astype(x.dtype), window_strides=(1,), padding=[(CONV_WIDTH - 1, 0)],
        dimension_numbers=('NWC', 'WIO', 'NWC'), feature_group_count=x.shape[-1])
    return y + b.astype(x.dtype)


def rglru_mixer(h, w_in, conv_w, conv_b, w_rg, b_rg, w_ig, b_ig, lam, w_out):
    f32 = jnp.float32
    bsz, seq, _ = h.shape
    gate_branch, xb = jnp.split(h @ w_in, 2, axis=-1)
    xb = causal_depthwise_conv(xb, conv_w, conv_b).astype(f32)
    xblk = xb.reshape(bsz, seq, LRU_BLOCKS, LRU_BLOCK)
    r = jax.nn.sigmoid(jnp.einsum('blhi,hij->blhj', xblk, w_rg.astype(f32)).reshape(bsz, seq, LRU_WIDTH)
                       + b_rg.astype(f32))
    ig = jax.nn.sigmoid(jnp.einsum('blhi,hij->blhj', xblk, w_ig.astype(f32)).reshape(bsz, seq, LRU_WIDTH)
                        + b_ig.astype(f32))
    log_a = -LRU_C * r * jax.nn.softplus(-lam.astype(f32))
    a = jnp.exp(log_a)
    mult = jnp.sqrt(-jnp.expm1(2.0 * log_a))
    _, hs = lax.associative_scan(_real_affine_combine, (a, mult * (ig * xb)), axis=1)
    y = hs * jax.nn.gelu(gate_branch.astype(f32))
    return y.astype(h.dtype) @ w_out


def swiglu(h, w_gu, w_down):
    g, u = jnp.split(h @ w_gu, 2, axis=-1)
    return (jax.nn.silu(g) * u) @ w_down


def setup_inputs(seed: int = 0) -> dict:
    key = jax.random.key(seed)
    ks = jax.random.split(key, 32)
    f32 = jnp.float32
    nrm = lambda k, shape, s: jax.random.normal(k, shape, f32) * s
    D = D_MODEL
    x = nrm(ks[0], (BATCH, SEQ, D), 1.0)
    c = nrm(ks[1], (BATCH, D), 1.0)
    norm_g = 1.0 + nrm(ks[2], (DEPTH, 2, D), 0.02)
    w_ada = nrm(ks[3], (DEPTH, D, N_MOD * D), 0.5 * D ** -0.5)
    b_ada = nrm(ks[4], (DEPTH, N_MOD * D), 0.02)
    s5_w_in = nrm(ks[5], (N_A, D, S5_WIDTH), D ** -0.5)
    n = jnp.arange(S5_STATE, dtype=f32)
    s5_lam_re = -0.5 + nrm(ks[6], (N_A, S5_GROUPS, S5_STATE), 0.01)
    s5_lam_im = math.pi * n + nrm(ks[7], (N_A, S5_GROUPS, S5_STATE), 0.01)
    s5_log_dt = jax.random.uniform(ks[8], (N_A, S5_GROUPS), f32,
                                   math.log(S5_DT_MIN), math.log(S5_DT_MAX))
    s5_b_re = nrm(ks[9], (N_A, S5_GROUPS, S5_STATE, S5_GROUP), (2 * S5_GROUP) ** -0.5)
    s5_b_im = nrm(ks[10], (N_A, S5_GROUPS, S5_STATE, S5_GROUP), (2 * S5_GROUP) ** -0.5)
    s5_c_re = nrm(ks[11], (N_A, S5_GROUPS, S5_GROUP, S5_STATE), (2 * S5_STATE) ** -0.5)
    s5_c_im = nrm(ks[12], (N_A, S5_GROUPS, S5_GROUP, S5_STATE), (2 * S5_STATE) ** -0.5)
    s5_d = nrm(ks[13], (N_A, S5_WIDTH), 1.0)
    s5_w_glu = nrm(ks[14], (N_A, S5_WIDTH, 2 * D), S5_WIDTH ** -0.5)
    lru_w_in = nrm(ks[15], (N_B, D, 2 * LRU_WIDTH), D ** -0.5)
    lru_conv_w = nrm(ks[16], (N_B, CONV_WIDTH, 1, LRU_WIDTH), CONV_WIDTH ** -0.5)
    lru_conv_b = nrm(ks[17], (N_B, LRU_WIDTH), 0.02)
    lru_w_rg = nrm(ks[18], (N_B, LRU_BLOCKS, LRU_BLOCK, LRU_BLOCK), LRU_BLOCK ** -0.5)
    lru_b_rg = nrm(ks[19], (N_B, LRU_WIDTH), 0.1)
    lru_w_ig = nrm(ks[20], (N_B, LRU_BLOCKS, LRU_BLOCK, LRU_BLOCK), LRU_BLOCK ** -0.5)
    lru_b_ig = nrm(ks[21], (N_B, LRU_WIDTH), 0.1)
    a_pow = jax.random.uniform(ks[22], (N_B, LRU_WIDTH), f32, 0.9, 0.999)
    a0 = a_pow ** (1.0 / LRU_C)
    lru_lam = jnp.log(a0) - jnp.log1p(-a0)
    lru_w_out = nrm(ks[23], (N_B, LRU_WIDTH, D), LRU_WIDTH ** -0.5)
    ffn_w_gu = nrm(ks[24], (DEPTH, D, 2 * FFN_HIDDEN), D ** -0.5)
    ffn_w_down = nrm(ks[25], (DEPTH, FFN_HIDDEN, D), FFN_HIDDEN ** -0.5)
    final_g = 1.0 + nrm(ks[26], (D,), 0.02)
    return {"x": x, "c": c, "norm_g": norm_g, "w_ada": w_ada, "b_ada": b_ada,
            "s5_w_in": s5_w_in, "s5_lam_re": s5_lam_re, "s5_lam_im": s5_lam_im,
            "s5_log_dt": s5_log_dt, "s5_b_re": s5_b_re, "s5_b_im": s5_b_im,
            "s5_c_re": s5_c_re, "s5_c_im": s5_c_im, "s5_d": s5_d, "s5_w_glu": s5_w_glu,
            "lru_w_in": lru_w_in, "lru_conv_w": lru_conv_w, "lru_conv_b": lru_conv_b,
            "lru_w_rg": lru_w_rg, "lru_b_rg": lru_b_rg, "lru_w_ig": lru_w_ig,
            "lru_b_ig": lru_b_ig, "lru_lam": lru_lam, "lru_w_out": lru_w_out,
            "ffn_w_gu": ffn_w_gu, "ffn_w_down": ffn_w_down, "final_g": final_g}


def reference(x, c, norm_g, w_ada, b_ada,
              s5_w_in, s5_lam_re, s5_lam_im, s5_log_dt, s5_b_re, s5_b_im,
              s5_c_re, s5_c_im, s5_d, s5_w_glu,
              lru_w_in, lru_conv_w, lru_conv_b, lru_w_rg, lru_b_rg, lru_w_ig,
              lru_b_ig, lru_lam, lru_w_out,
              ffn_w_gu, ffn_w_down, final_g):
    cond = jax.nn.silu(c)
    for i in range(DEPTH):
        mod = cond @ w_ada[i] + b_ada[i]
        sh1, sc1, g1, sh2, sc2, g2 = [m[:, None, :] for m in jnp.split(mod, N_MOD, axis=-1)]
        h = rmsnorm(x, norm_g[i, 0]) * (1.0 + sc1) + sh1
        j = i // N_MIXERS
        if i % N_MIXERS == 0:
            y = s5_mixer(h, s5_w_in[j], s5_lam_re[j], s5_lam_im[j], s5_log_dt[j],
                         s5_b_re[j], s5_b_im[j], s5_c_re[j], s5_c_im[j], s5_d[j], s5_w_glu[j])
        else:
            y = rglru_mixer(h, lru_w_in[j], lru_conv_w[j], lru_conv_b[j], lru_w_rg[j],
                            lru_b_rg[j], lru_w_ig[j], lru_b_ig[j], lru_lam[j], lru_w_out[j])
        x = x + g1 * y
        h = rmsnorm(x, norm_g[i, 1]) * (1.0 + sc2) + sh2
        x = x + g2 * swiglu(h, ffn_w_gu[i], ffn_w_down[i])
    return rmsnorm(x, final_g)
```

```python
import functools

import jax
import jax.numpy as jnp
from jax import lax
from jax.experimental import pallas as pl
from jax.experimental.pallas import tpu as pltpu

F32 = jnp.float32
BF16 = jnp.bfloat16

EPS = 1e-6
LRU_C = 8.0
LANES = 128
SUBLANES = 8
MXU_N = 256
S5_CHUNK = 8
S5_TILE_GROUPS = 8
VMEM_LIMIT = 56 * 1024 * 1024


def _params(semantics, vmem=VMEM_LIMIT):
    return pltpu.CompilerParams(dimension_semantics=semantics, vmem_limit_bytes=vmem)


def _dot(a, b):
    return jnp.dot(a, b, preferred_element_type=F32)


def _rms_mod(x, gain, scale, shift):
    ms = jnp.mean(x * x, axis=-1, keepdims=True)
    y = x * lax.rsqrt(ms + EPS) * gain
    return y * (1.0 + scale) + shift


def _load_tiles(scr, rows):
    return jnp.concatenate([scr[q, rows, :] for q in range(scr.shape[0])], axis=1)


def _store_tiles(scr, rows, val):
    for q in range(scr.shape[0]):
        scr[q, rows, :] = val[:, q * LANES:(q + 1) * LANES]


def _mod_block(tm, rows_per_batch):
    nsub = max(1, tm // rows_per_batch)
    assert tm % (rows_per_batch * nsub) == 0 or rows_per_batch % tm == 0
    return nsub, rows_per_batch * nsub


def _mod_kernel(c_ref, w_ref, b_ref, o_ref):
    c = c_ref[...]
    cond = c * jax.nn.sigmoid(c)
    o_ref[...] = _dot(cond.astype(BF16), w_ref[...].astype(BF16)) + b_ref[...]


def _modulation(c, w_ada, b_ada):
    depth, d, n = w_ada.shape
    b = c.shape[0]
    rows = -(-b // SUBLANES) * SUBLANES
    cp = jnp.zeros((rows, d), F32).at[:b].set(c)
    tn = 1024 if n % 1024 == 0 else n
    out = pl.pallas_call(
        _mod_kernel,
        grid=(depth, n // tn),
        in_specs=[
            pl.BlockSpec((rows, d), lambda i, j: (0, 0)),
            pl.BlockSpec((None, d, tn), lambda i, j: (i, 0, j)),
            pl.BlockSpec((None, 1, tn), lambda i, j: (i, 0, j)),
        ],
        out_specs=pl.BlockSpec((None, rows, tn), lambda i, j: (i, 0, j)),
        out_shape=jax.ShapeDtypeStruct((depth, rows, n), F32),
        compiler_params=_params(("parallel", "parallel")),
        name="adaln_mod",
    )(cp, w_ada, b_ada.reshape(depth, 1, n))
    return out[:, :b]


def _make_ln_mm_kernel(nsub, sub_rows, swiglu, n_axis):
    def kern(*refs):
        if swiglu:
            x_ref, g_ref, sc_ref, sh_ref, wa_ref, wb_ref, o_ref, h_ref = refs
        else:
            x_ref, g_ref, sc_ref, sh_ref, wa_ref, o_ref, h_ref = refs

        @pl.when(pl.program_id(n_axis) == 0)
        def _():
            for s in range(nsub):
                rows = slice(s * sub_rows, (s + 1) * sub_rows)
                h = _rms_mod(x_ref[rows, :], g_ref[...], sc_ref[s], sh_ref[s])
                h_ref[rows, :] = h.astype(BF16)

        h = h_ref[...]
        acc = _dot(h, wa_ref[...])
        if swiglu:
            acc = acc * jax.nn.sigmoid(acc) * _dot(h, wb_ref[...])
        o_ref[...] = acc.astype(o_ref.dtype)

    return kern


def _ln_matmul(x2d, gain, scale, shift, w, *, rows_per_batch, tm, tn, swiglu, out_dtype,
               s5_view=False, name):
    d = gain.shape[-1]
    n_out = w.shape[1] // 2 if swiglu else w.shape[1]
    assert n_out % tn == 0
    nt = n_out // tn
    rows = x2d.shape[0]
    assert rows % tm == 0
    nsub, mod_rows = _mod_block(tm, rows_per_batch)
    sub_rows = tm // nsub
    gain2 = gain.reshape(1, d)

    if s5_view:
        grid = (S5_CHUNK, rows // tm, nt)
        x_spec = pl.BlockSpec((tm, d), lambda t, m, n: (m, t))
        mod_spec = pl.BlockSpec((nsub, 1, d), lambda t, m, n: ((m * tm) // mod_rows, 0, 0))
        g_spec = pl.BlockSpec((1, d), lambda t, m, n: (0, 0))
        wa_spec = pl.BlockSpec((d, tn), lambda t, m, n: (0, n))
        wb_spec = pl.BlockSpec((d, tn), lambda t, m, n: (0, n + nt))
        o_spec = pl.BlockSpec((None, tm, tn), lambda t, m, n: (t, m, n))
        out_shape = jax.ShapeDtypeStruct((S5_CHUNK, rows, n_out), out_dtype)
        sem = ("parallel", "parallel", "arbitrary")
        n_axis = 2
    else:
        grid = (rows // tm, nt)
        x_spec = pl.BlockSpec((tm, d), lambda m, n: (m, 0))
        mod_spec = pl.BlockSpec((nsub, 1, d), lambda m, n: ((m * tm) // mod_rows, 0, 0))
        g_spec = pl.BlockSpec((1, d), lambda m, n: (0, 0))
        wa_spec = pl.BlockSpec((d, tn), lambda m, n: (0, n))
        wb_spec = pl.BlockSpec((d, tn), lambda m, n: (0, n + nt))
        o_spec = pl.BlockSpec((tm, tn), lambda m, n: (m, n))
        out_shape = jax.ShapeDtypeStruct((rows, n_out), out_dtype)
        sem = ("parallel", "arbitrary")
        n_axis = 1

    in_specs = [x_spec, g_spec, mod_spec, mod_spec, wa_spec]
    args = [x2d, gain2, scale, shift, w]
    if swiglu:
        in_specs.append(wb_spec)
        args.append(w)
    return pl.pallas_call(
        _make_ln_mm_kernel(nsub, sub_rows, swiglu, n_axis),
        grid=grid,
        in_specs=in_specs,
        out_specs=o_spec,
        out_shape=out_shape,
        scratch_shapes=[pltpu.VMEM((tm, d), BF16)],
        compiler_params=_params(sem),
        name=name,
    )(*args)


def _make_mm_res_kernel(nsub, sub_rows, glu):
    def kern(*refs):
        if glu:
            a_ref, wa_ref, wb_ref, res_ref, gate_ref, o_ref = refs
        else:
            a_ref, wa_ref, res_ref, gate_ref, o_ref = refs
        a = a_ref[...]
        acc = _dot(a, wa_ref[...])
        if glu:
            acc = acc * jax.nn.sigmoid(_dot(a, wb_ref[...]))
        for s in range(nsub):
            rows = slice(s * sub_rows, (s + 1) * sub_rows)
            o_ref[rows, :] = res_ref[rows, :] + gate_ref[s] * acc[rows, :]

    return kern


def _matmul_residual(a, w, res2d, gate, *, rows_per_batch, tm, tn, glu, s5_view=False, name):
    k = a.shape[-1]
    n_out = w.shape[1] // 2 if glu else w.shape[1]
    assert n_out % tn == 0
    nt = n_out // tn
    rows = a.shape[-2]
    assert rows % tm == 0
    nsub, mod_rows = _mod_block(tm, rows_per_batch)
    sub_rows = tm // nsub

    if s5_view:
        grid = (S5_CHUNK, rows // tm, nt)
        a_spec = pl.BlockSpec((None, tm, k), lambda t, m, n: (t, m, 0))
        wa_spec = pl.BlockSpec((k, tn), lambda t, m, n: (0, n))
        wb_spec = pl.BlockSpec((k, tn), lambda t, m, n: (0, n + nt))
        r_spec = pl.BlockSpec((tm, tn), lambda t, m, n: (m, t * nt + n))
        gate_spec = pl.BlockSpec((nsub, 1, tn), lambda t, m, n: ((m * tm) // mod_rows, 0, n))
        sem = ("parallel", "parallel", "parallel")
    else:
        grid = (rows // tm, nt)
        a_spec = pl.BlockSpec((tm, k), lambda m, n: (m, 0))
        wa_spec = pl.BlockSpec((k, tn), lambda m, n: (0, n))
        wb_spec = pl.BlockSpec((k, tn), lambda m, n: (0, n + nt))
        r_spec = pl.BlockSpec((tm, tn), lambda m, n: (m, n))
        gate_spec = pl.BlockSpec((nsub, 1, tn), lambda m, n: ((m * tm) // mod_rows, 0, n))
        sem = ("parallel", "parallel")

    in_specs = [a_spec, wa_spec]
    args = [a, w]
    if glu:
        in_specs.append(wb_spec)
        args.append(w)
    in_specs += [r_spec, gate_spec]
    args += [res2d, gate]
    return pl.pallas_call(
        _make_mm_res_kernel(nsub, sub_rows, glu),
        grid=grid,
        in_specs=in_specs,
        out_specs=r_spec,
        out_shape=jax.ShapeDtypeStruct(res2d.shape, F32),
        compiler_params=_params(sem),
        name=name,
    )(*args)


def _cmul(ar, ai, br, bi):
    return ar * br - ai * bi, ar * bi + ai * br


def _s5_prep_kernel(lr_ref, li_ref, ldt_ref, br_ref, bi_ref, cr_ref, ci_ref, seg_ref,
                    win_ref, wout_ref, k_ref, lam_ref):
    lr = lr_ref[...]
    li = li_ref[...]
    dt = jnp.exp(ldt_ref[...])
    mag = jnp.exp(lr * dt)
    ab_re = mag * jnp.cos(li * dt)
    ab_im = mag * jnp.sin(li * dt)
    nr, ni = ab_re - 1.0, ab_im
    den = lr * lr + li * li
    f_re = (nr * lr + ni * li) / den
    f_im = (ni * lr - nr * li) / den
    br, bi = br_ref[...], bi_ref[...]
    bb_re = f_re * br - f_im * bi
    bb_im = f_re * bi + f_im * br
    cr, ci = cr_ref[...], ci_ref[...]
    seg = seg_ref[...]
    gc = br.shape[0]

    pows = [(jnp.ones_like(ab_re), jnp.zeros_like(ab_im))]
    for _ in range(S5_CHUNK):
        pows.append(_cmul(pows[-1][0], pows[-1][1], ab_re, ab_im))

    for t in range(S5_CHUNK):
        pr, pi = pows[S5_CHUNK - 1 - t]
        wr, wi = _cmul(pr, pi, bb_re, bb_im)
        win_ref[t, 0] = wr
        win_ref[t, 1] = wi

    for tau in range(S5_CHUNK + 1):
        pr, pi = pows[tau]
        qr, qi = _cmul(cr, ci, pr, pi)
        if tau >= 1:
            wout_ref[tau - 1, 0] = qr
            wout_ref[tau - 1, 1] = -qi
        if tau < S5_CHUNK:
            prod = jnp.concatenate(
                [qr[c:c + 1] * bb_re - qi[c:c + 1] * bb_im for c in range(gc)], axis=0)
            hi = prod.astype(BF16)
            lo = (prod - hi.astype(F32)).astype(BF16)
            k_ref[tau] = _dot(hi, seg) + _dot(lo, seg)

    lam_ref[0:1, :] = pows[S5_CHUNK][0]
    lam_ref[1:2, :] = pows[S5_CHUNK][1]


def _s5_operators(lam_re, lam_im, log_dt, b_re, b_im, c_re, c_im):
    g, p = lam_re.shape
    gc = b_re.shape[-1]
    gp = g * p
    tg = S5_TILE_GROUPS
    assert gc * tg == LANES and g % tg == 0
    nj = g // tg
    lr = lam_re.reshape(1, gp)
    li = lam_im.reshape(1, gp)
    ldt = jnp.repeat(log_dt, p).reshape(1, gp)
    brt = b_re.transpose(2, 0, 1).reshape(gc, gp)
    bit = b_im.transpose(2, 0, 1).reshape(gc, gp)
    crt = c_re.transpose(1, 0, 2).reshape(gc, gp)
    cit = c_im.transpose(1, 0, 2).reshape(gc, gp)
    seg = (jnp.arange(gp)[:, None] // p == jnp.arange(g)[None, :]).astype(BF16)

    win, wout, kk, lam = pl.pallas_call(
        _s5_prep_kernel,
        out_shape=[
            jax.ShapeDtypeStruct((S5_CHUNK, 2, gc, gp), F32),
            jax.ShapeDtypeStruct((S5_CHUNK, 2, gc, gp), F32),
            jax.ShapeDtypeStruct((S5_CHUNK, gc * gc, g), F32),
            jax.ShapeDtypeStruct((2, gp), F32),
        ],
        compiler_params=pltpu.CompilerParams(vmem_limit_bytes=VMEM_LIMIT),
        name="s5_prep",
    )(lr, li, ldt, brt, bit, crt, cit, seg)

    eye = jnp.eye(tg, dtype=bool)
    tc = S5_CHUNK
    k5 = kk.reshape(tc, gc, gc, nj, tg)
    tau = jnp.arange(tc)[None, :] - jnp.arange(tc)[:, None]
    toe = jnp.where((tau >= 0)[:, :, None, None, None, None],
                    k5[jnp.clip(tau, 0, tc - 1)], 0.0)
    toe = toe.transpose(4, 0, 5, 3, 1, 2)
    w_io = jnp.where(eye[None, None, :, None, None, :, None],
                     toe[:, :, :, :, :, None, :], 0.0)
    w_io = w_io.reshape(nj, tc * LANES, tc * LANES).astype(BF16)
    a = win.reshape(tc, 2, gc, nj, tg, p).transpose(3, 0, 2, 1, 4, 5)
    w_is = jnp.where(eye[None, None, :, None, None, :, None],
                     a[:, :, None, :, :, :, :], 0.0)
    w_is = w_is.reshape(nj, tc * LANES, 2 * tg * p).astype(BF16)
    o = wout.reshape(tc, 2, gc, nj, tg, p).transpose(3, 1, 4, 5, 0, 2)
    w_so = jnp.where(eye[None, None, :, None, None, :, None],
                     o[:, :, :, :, :, None, :], 0.0)
    w_so = w_so.reshape(nj, 2 * tg * p, tc * LANES).astype(BF16)
    lam8 = lam.reshape(2, nj, tg * p).transpose(1, 0, 2).reshape(nj, 1, 2 * tg * p)
    return w_io, w_is, w_so, lam8


def _make_s5_kernel(nb, rows_per_batch, half):
    nseg = SUBLANES
    sl = rows_per_batch // nseg
    tc = S5_CHUNK

    def kern(u_ref, wio_ref, wis_ref, wso_ref, lam_ref, d_ref, y_ref, s_scr, h_scr):
        lhs = jnp.concatenate([u_ref[t].astype(BF16) for t in range(tc)], axis=1)
        _store_tiles(s_scr, slice(None), _dot(lhs, wis_ref[...]))
        lam = lam_ref[...]
        ar, ai = lam[:, :half], lam[:, half:]

        for b in range(nb):
            base = b * rows_per_batch

            def seg_scan(i, carry):
                er, ei, qr, qi = carry
                rows = pl.ds(base + i, nseg, stride=sl)
                _store_tiles(h_scr, rows, jnp.concatenate([er, ei], axis=1))
                s = _load_tiles(s_scr, rows)
                er2 = ar * er - ai * ei + s[:, :half]
                ei2 = ar * ei + ai * er + s[:, half:]
                qr2, qi2 = _cmul(qr, qi, ar, ai)
                return er2, ei2, qr2, qi2

            zero = jnp.zeros((nseg, half), F32)
            er, ei, pr, pi = lax.fori_loop(
                0, sl, seg_scan,
                (zero, zero, jnp.ones((1, half), F32), jnp.zeros((1, half), F32)))
            cr = jnp.zeros((1, half), F32)
            ci = jnp.zeros((1, half), F32)
            crs, cis = [], []
            for s in range(nseg):
                crs.append(cr)
                cis.append(ci)
                nr, ni = _cmul(pr, pi, cr, ci)
                cr, ci = nr + er[s:s + 1], ni + ei[s:s + 1]
            cr8 = jnp.concatenate(crs, axis=0)
            ci8 = jnp.concatenate(cis, axis=0)

            def fix(i, carry):
                qr, qi = carry
                rows = pl.ds(base + i, nseg, stride=sl)
                hl = _load_tiles(h_scr, rows)
                dr, di = _cmul(qr, qi, cr8, ci8)
                _store_tiles(h_scr, rows,
                             jnp.concatenate([hl[:, :half] + dr, hl[:, half:] + di], axis=1))
                return _cmul(qr, qi, ar, ai)

            lax.fori_loop(0, sl, fix, (jnp.ones((1, half), F32), jnp.zeros((1, half), F32)))

        hs = _load_tiles(h_scr, slice(None)).astype(BF16)
        y = _dot(lhs, wio_ref[...]) + _dot(hs, wso_ref[...])
        d = d_ref[...]
        for t in range(tc):
            yt = y[:, t * LANES:(t + 1) * LANES] + d * u_ref[t]
            y_ref[t] = jax.nn.gelu(yt).astype(BF16)

    return kern


def _s5_core(u3, ops, d_skip, *, nb, rows_per_batch):
    w_io, w_is, w_so, lam8 = ops
    tc, rows, e = u3.shape
    nj = e // LANES
    kdim = tc * LANES
    sdim = w_is.shape[-1]
    assert rows == nb * rows_per_batch and rows_per_batch % SUBLANES == 0
    return pl.pallas_call(
        _make_s5_kernel(nb, rows_per_batch, sdim // 2),
        grid=(nj,),
        in_specs=[
            pl.BlockSpec((tc, rows, LANES), lambda j: (0, 0, j)),
            pl.BlockSpec((None, kdim, kdim), lambda j: (j, 0, 0)),
            pl.BlockSpec((None, kdim, sdim), lambda j: (j, 0, 0)),
            pl.BlockSpec((None, sdim, kdim), lambda j: (j, 0, 0)),
            pl.BlockSpec((None, 1, sdim), lambda j: (j, 0, 0)),
            pl.BlockSpec((1, LANES), lambda j: (0, j)),
        ],
        out_specs=pl.BlockSpec((tc, rows, LANES), lambda j: (0, 0, j)),
        out_shape=jax.ShapeDtypeStruct((tc, rows, e), BF16),
        scratch_shapes=[pltpu.VMEM((sdim // LANES, rows, LANES), F32),
                        pltpu.VMEM((sdim // LANES, rows, LANES), F32)],
        compiler_params=_params(("parallel",)),
        name="s5_core",
    )(u3, w_io, w_is, w_so, lam8, d_skip.reshape(1, e))


def _band_windows(width, blk, tn):
    starts, ends = [], []
    for n in range(width // tn):
        h_lo = (n * tn) // blk
        h_hi = (n * tn + tn - 1) // blk
        starts.append((h_lo * blk) // LANES)
        ends.append(-(-((h_hi + 1) * blk) // LANES))
    kw = max(e - s for s, e in zip(starts, ends))
    total = width // LANES
    starts = [min(s, total - kw) for s in starts]
    return starts, kw * LANES


def _make_lru_kernel(e, tm, tn, starts, kw, tiles_per_seq):
    nseg = SUBLANES
    sl = tm // nseg
    halo = SUBLANES
    ntile = e // tn

    def kern(gx_ref, cw_ref, cb_ref, wrg_ref, wig_ref, brg_ref, big_ref, lam_ref, y_ref,
             xpad, xc_scr, a_scr, b_scr, e_scr, p_scr, hcar):
        @pl.when(pl.program_id(0) % tiles_per_seq == 0)
        def _():
            xpad[0:halo, :] = jnp.zeros((halo, e), F32)
            hcar[...] = jnp.zeros((1, e), F32)

        xpad[halo:halo + tm, :] = gx_ref[:, e:2 * e]
        xc = cb_ref[...] + cw_ref[3:4, :] * xpad[halo:halo + tm, :]
        for k in range(3):
            xc = xc + cw_ref[k:k + 1, :] * xpad[halo - 3 + k:halo - 3 + k + tm, :]
        xpad[0:halo, :] = xpad[tm:tm + halo, :]
        xc_scr[...] = xc

        lam = lam_ref[...]
        neg = -lam
        softplus = jnp.maximum(neg, 0.0) + jnp.log1p(jnp.exp(-jnp.abs(neg)))

        for n in range(ntile):
            cols = slice(n * tn, (n + 1) * tn)
            win = xc_scr[:, starts[n] * LANES:starts[n] * LANES + kw].astype(BF16)
            r = jax.nn.sigmoid(_dot(win, wrg_ref[n]) + brg_ref[:, cols])
            ig = jax.nn.sigmoid(_dot(win, wig_ref[n]) + big_ref[:, cols])
            log_a = (-LRU_C) * r * softplus[:, cols]
            a = jnp.exp(log_a)
            mult = jnp.sqrt(1.0 - a * a)
            _store_tiles(a_scr, slice(None), a)
            _store_tiles(b_scr, slice(None), mult * (ig * xc_scr[:, cols]))

            def seg_scan(i, carry):
                hh, pp = carry
                rows = pl.ds(i, nseg, stride=sl)
                av = _load_tiles(a_scr, rows)
                hh = av * hh + _load_tiles(b_scr, rows)
                pp = av * pp
                _store_tiles(e_scr, rows, hh)
                _store_tiles(p_scr, rows, pp)
                return hh, pp

            hh, pp = lax.fori_loop(0, sl, seg_scan,
                                   (jnp.zeros((nseg, tn), F32), jnp.ones((nseg, tn), F32)))
            c = hcar[:, cols]
            cs = []
            for s in range(nseg):
                cs.append(jnp.broadcast_to(c, (sl, tn)))
                c = hh[s:s + 1] + pp[s:s + 1] * c
            hcar[:, cols] = c
            h = (_load_tiles(e_scr, slice(None))
                 + _load_tiles(p_scr, slice(None)) * jnp.concatenate(cs, axis=0))
            y_ref[:, cols] = (h * jax.nn.gelu(gx_ref[:, cols])).astype(BF16)

    return kern


def _lru_core(gx, conv_w, conv_b, w_rg, b_rg, w_ig, b_ig, lam, *, seq, tm):
    rows, e2 = gx.shape
    e = e2 // 2
    nblk, blk, _ = w_rg.shape
    tn = MXU_N
    assert e % tn == 0 and seq % tm == 0 and tm % (SUBLANES * SUBLANES) == 0
    starts, kw = _band_windows(e, blk, tn)

    def banded(w):
        idx = jnp.arange(nblk)
        dense = jnp.zeros((nblk, blk, nblk, blk), F32).at[idx, :, idx, :].set(w).reshape(e, e)
        return jnp.stack([dense[s * LANES:s * LANES + kw, n * tn:(n + 1) * tn]
                          for n, s in enumerate(starts)]).astype(BF16)

    wrg, wig = banded(w_rg), banded(w_ig)
    ntile = e // tn
    full = lambda shape: pl.BlockSpec(shape, lambda m: (0,) * len(shape))
    return pl.pallas_call(
        _make_lru_kernel(e, tm, tn, starts, kw, seq // tm),
        grid=(rows // tm,),
        in_specs=[
            pl.BlockSpec((tm, e2), lambda m: (m, 0)),
            full((conv_w.shape[0], e)),
            full((1, e)),
            full((ntile, kw, tn)),
            full((ntile, kw, tn)),
            full((1, e)),
            full((1, e)),
            full((1, e)),
        ],
        out_specs=pl.BlockSpec((tm, e), lambda m: (m, 0)),
        out_shape=jax.ShapeDtypeStruct((rows, e), BF16),
        scratch_shapes=[
            pltpu.VMEM((tm + SUBLANES, e), F32),
            pltpu.VMEM((tm, e), F32),
            pltpu.VMEM((tn // LANES, tm, LANES), F32),
            pltpu.VMEM((tn // LANES, tm, LANES), F32),
            pltpu.VMEM((tn // LANES, tm, LANES), F32),
            pltpu.VMEM((tn // LANES, tm, LANES), F32),
            pltpu.VMEM((1, e), F32),
        ],
        compiler_params=_params(("arbitrary",)),
        name="lru_core",
    )(gx, conv_w.reshape(conv_w.shape[0], e), conv_b.reshape(1, e), wrg, wig,
      b_rg.reshape(1, e), b_ig.reshape(1, e), lam.reshape(1, e))


def _rms_kernel(x_ref, g_ref, o_ref):
    x = x_ref[...]
    ms = jnp.mean(x * x, axis=-1, keepdims=True)
    o_ref[...] = x * lax.rsqrt(ms + EPS) * g_ref[...]


def _final_norm(x2d, gain, tm):
    rows, d = x2d.shape
    return pl.pallas_call(
        _rms_kernel,
        grid=(rows // tm,),
        in_specs=[pl.BlockSpec((tm, d), lambda m: (m, 0)), pl.BlockSpec((1, d), lambda m: (0, 0))],
        out_specs=pl.BlockSpec((tm, d), lambda m: (m, 0)),
        out_shape=jax.ShapeDtypeStruct((rows, d), F32),
        compiler_params=_params(("parallel",)),
        name="final_norm",
    )(x2d, gain.reshape(1, d))


def _tile(n, pref):
    t = min(n, pref)
    while n % t:
        t //= 2
    return t


def kernel(x, c, norm_g, w_ada, b_ada, s5_w_in, s5_lam_re, s5_lam_im, s5_log_dt, s5_b_re, s5_b_im, s5_c_re, s5_c_im, s5_d, s5_w_glu, lru_w_in, lru_conv_w, lru_conv_b, lru_w_rg, lru_b_rg, lru_w_ig, lru_b_ig, lru_lam, lru_w_out, ffn_w_gu, ffn_w_down, final_g):
    bsz, seq, d = x.shape
    depth = w_ada.shape[0]
    rows = bsz * seq
    x2 = x.reshape(rows, d)
    mods = _modulation(c, w_ada, b_ada).reshape(depth, bsz, 6, 1, d)
    tm = _tile(seq, 512)

    for i in range(depth):
        sh1, sc1, g1, sh2, sc2, g2 = [mods[i, :, q] for q in range(6)]
        j = i // 2
        if i % 2 == 0:
            crow = rows // S5_CHUNK
            cseq = seq // S5_CHUNK
            tms = _tile(cseq, 512) if cseq >= 512 else _tile(crow, 512)
            u3 = _ln_matmul(x2.reshape(crow, S5_CHUNK * d), norm_g[i, 0], sc1, sh1,
                            s5_w_in[j].astype(BF16), rows_per_batch=cseq, tm=tms,
                            tn=s5_w_in.shape[-1], swiglu=False, out_dtype=F32, s5_view=True,
                            name="s5_in")
            ops = _s5_operators(s5_lam_re[j], s5_lam_im[j], s5_log_dt[j], s5_b_re[j], s5_b_im[j],
                                s5_c_re[j], s5_c_im[j])
            y3 = _s5_core(u3, ops, s5_d[j], nb=bsz, rows_per_batch=cseq)
            x2 = _matmul_residual(y3, s5_w_glu[j].astype(BF16), x2.reshape(crow, S5_CHUNK * d), g1,
                                  rows_per_batch=cseq, tm=tms, tn=_tile(d, 512), glu=True,
                                  s5_view=True, name="s5_out").reshape(rows, d)
        else:
            e = lru_w_in.shape[-1] // 2
            gx = _ln_matmul(x2, norm_g[i, 0], sc1, sh1, lru_w_in[j].astype(BF16),
                            rows_per_batch=seq, tm=tm, tn=_tile(2 * e, 512), swiglu=False,
                            out_dtype=F32, name="lru_in")
            y = _lru_core(gx, lru_conv_w[j], lru_conv_b[j], lru_w_rg[j], lru_b_rg[j],
                          lru_w_ig[j], lru_b_ig[j], lru_lam[j], seq=seq, tm=_tile(seq, 256))
            x2 = _matmul_residual(y, lru_w_out[j].astype(BF16), x2, g1, rows_per_batch=seq,
                                  tm=tm, tn=_tile(d, 512), glu=False, name="lru_out")
        hidden = ffn_w_down.shape[1]
        act = _ln_matmul(x2, norm_g[i, 1], sc2, sh2, ffn_w_gu[i].astype(BF16), rows_per_batch=seq,
                         tm=tm, tn=_tile(hidden, 512), swiglu=True, out_dtype=BF16, name="ffn_up")
        x2 = _matmul_residual(act, ffn_w_down[i].astype(BF16), x2, g2, rows_per_batch=seq, tm=tm,
                              tn=_tile(d, 512), glu=False, name="ffn_down")

    return _final_norm(x2, final_g, tm).reshape(bsz, seq, d)
```

```python
import jax
import jax.numpy as jnp
from jax import lax
from jax.experimental import pallas as pl
from jax.experimental.pallas import tpu as pltpu

F32 = jnp.float32
BF16 = jnp.bfloat16

EPS = 1e-6
LRU_C = 8.0
LANES = 128
SUBLANES = 8
MXU_N = 256
S5_CHUNK = 8
S5_TILE_GROUPS = 8
VMEM_LIMIT = 56 * 1024 * 1024


def _params(semantics, vmem=VMEM_LIMIT):
    return pltpu.CompilerParams(dimension_semantics=semantics, vmem_limit_bytes=vmem)


def _dot(a, b):
    return jnp.dot(a, b, preferred_element_type=F32)


def _rms_mod(x, gain, scale, shift):
    ms = jnp.mean(x * x, axis=-1, keepdims=True)
    y = x * lax.rsqrt(ms + EPS) * gain
    return y * (1.0 + scale) + shift


def _cmul(ar, ai, br, bi):
    return ar * br - ai * bi, ar * bi + ai * br


def _mod_block(tm, rows_per_batch):
    nsub = max(1, tm // rows_per_batch)
    assert tm % (rows_per_batch * nsub) == 0 or rows_per_batch % tm == 0
    return nsub, rows_per_batch * nsub


def _mod_kernel(c_ref, w_ref, b_ref, o_ref):
    c = c_ref[...]
    cond = c * jax.nn.sigmoid(c)
    o_ref[...] = _dot(cond.astype(BF16), w_ref[...].astype(BF16)) + b_ref[...]


def _modulation(c, w_ada, b_ada):
    depth, d, n = w_ada.shape
    b = c.shape[0]
    rows = -(-b // SUBLANES) * SUBLANES
    cp = jnp.zeros((rows, d), F32).at[:b].set(c)
    tn = 1024 if n % 1024 == 0 else n
    out = pl.pallas_call(
        _mod_kernel,
        grid=(depth, n // tn),
        in_specs=[
            pl.BlockSpec((rows, d), lambda i, j: (0, 0)),
            pl.BlockSpec((None, d, tn), lambda i, j: (i, 0, j)),
            pl.BlockSpec((None, 1, tn), lambda i, j: (i, 0, j)),
        ],
        out_specs=pl.BlockSpec((None, rows, tn), lambda i, j: (i, 0, j)),
        out_shape=jax.ShapeDtypeStruct((depth, rows, n), F32),
        compiler_params=_params(("parallel", "parallel")),
        name="adaln_mod",
    )(cp, w_ada, b_ada.reshape(depth, 1, n))
    return out[:, :b]


def _make_ln_mm_kernel(nsub, sub_rows, swiglu, tiled_out):
    def kern(*refs):
        if swiglu:
            x_ref, g_ref, sc_ref, sh_ref, wa_ref, wb_ref, o_ref, h_ref = refs
        else:
            x_ref, g_ref, sc_ref, sh_ref, wa_ref, o_ref, h_ref = refs

        @pl.when(pl.program_id(1) == 0)
        def _():
            for s in range(nsub):
                rows = slice(s * sub_rows, (s + 1) * sub_rows)
                h = _rms_mod(x_ref[rows, :], g_ref[...], sc_ref[s], sh_ref[s])
                h_ref[rows, :] = h.astype(BF16)

        h = h_ref[...]
        acc = _dot(h, wa_ref[...])
        if swiglu:
            acc = acc * jax.nn.sigmoid(acc) * _dot(h, wb_ref[...])
        if tiled_out:
            for q in range(o_ref.shape[0]):
                o_ref[q] = acc[:, q * LANES:(q + 1) * LANES].astype(o_ref.dtype)
        else:
            o_ref[...] = acc.astype(o_ref.dtype)

    return kern


def _ln_matmul(x2d, gain, scale, shift, w, *, rows_per_batch, tm, tn, swiglu, out_dtype,
               tiled_out=False, name):
    rows, d = x2d.shape
    n_out = w.shape[1] // 2 if swiglu else w.shape[1]
    assert n_out % tn == 0 and rows % tm == 0 and tn % LANES == 0
    nt = n_out // tn
    nsub, mod_rows = _mod_block(tm, rows_per_batch)
    mod_spec = pl.BlockSpec((nsub, 1, d), lambda m, n: ((m * tm) // mod_rows, 0, 0))
    in_specs = [
        pl.BlockSpec((tm, d), lambda m, n: (m, 0)),
        pl.BlockSpec((1, d), lambda m, n: (0, 0)),
        mod_spec,
        mod_spec,
        pl.BlockSpec((d, tn), lambda m, n: (0, n)),
    ]
    args = [x2d, gain.reshape(1, d), scale, shift, w]
    if swiglu:
        in_specs.append(pl.BlockSpec((d, tn), lambda m, n: (0, n + nt)))
        args.append(w)
    if tiled_out:
        o_spec = pl.BlockSpec((tn // LANES, tm, LANES), lambda m, n: (n, m, 0))
        out_shape = jax.ShapeDtypeStruct((n_out // LANES, rows, LANES), out_dtype)
    else:
        o_spec = pl.BlockSpec((tm, tn), lambda m, n: (m, n))
        out_shape = jax.ShapeDtypeStruct((rows, n_out), out_dtype)
    return pl.pallas_call(
        _make_ln_mm_kernel(nsub, tm // nsub, swiglu, tiled_out),
        grid=(rows // tm, nt),
        in_specs=in_specs,
        out_specs=o_spec,
        out_shape=out_shape,
        scratch_shapes=[pltpu.VMEM((tm, d), BF16)],
        compiler_params=_params(("parallel", "arbitrary")),
        name=name,
    )(*args)


def _make_mm_res_kernel(nsub, sub_rows, glu, tiled_a):
    def kern(*refs):
        if glu:
            a_ref, wa_ref, wb_ref, res_ref, gate_ref, o_ref = refs
        else:
            a_ref, wa_ref, res_ref, gate_ref, o_ref = refs
        if tiled_a:
            a = jnp.concatenate([a_ref[q] for q in range(a_ref.shape[0])], axis=1)
        else:
            a = a_ref[...]
        acc = _dot(a, wa_ref[...])
        if glu:
            acc = acc * jax.nn.sigmoid(_dot(a, wb_ref[...]))
        for s in range(nsub):
            rows = slice(s * sub_rows, (s + 1) * sub_rows)
            o_ref[rows, :] = res_ref[rows, :] + gate_ref[s] * acc[rows, :]

    return kern


def _matmul_residual(a, w, res2d, gate, *, rows_per_batch, tm, tn, glu, tiled_a=False, name):
    rows, n_out = res2d.shape
    k = w.shape[0]
    assert n_out % tn == 0 and rows % tm == 0
    nt = n_out // tn
    nsub, mod_rows = _mod_block(tm, rows_per_batch)
    if tiled_a:
        a_spec = pl.BlockSpec((k // LANES, tm, LANES), lambda m, n: (0, m, 0))
    else:
        a_spec = pl.BlockSpec((tm, k), lambda m, n: (m, 0))
    in_specs = [a_spec, pl.BlockSpec((k, tn), lambda m, n: (0, n))]
    args = [a, w]
    if glu:
        in_specs.append(pl.BlockSpec((k, tn), lambda m, n: (0, n + nt)))
        args.append(w)
    r_spec = pl.BlockSpec((tm, tn), lambda m, n: (m, n))
    in_specs += [r_spec, pl.BlockSpec((nsub, 1, tn), lambda m, n: ((m * tm) // mod_rows, 0, n))]
    args += [res2d, gate]
    return pl.pallas_call(
        _make_mm_res_kernel(nsub, tm // nsub, glu, tiled_a),
        grid=(rows // tm, nt),
        in_specs=in_specs,
        out_specs=r_spec,
        out_shape=jax.ShapeDtypeStruct(res2d.shape, F32),
        compiler_params=_params(("parallel", "parallel")),
        name=name,
    )(*args)


def _make_s5_prep_kernel(tg, p):
    tc = S5_CHUNK
    half = tg * p

    def kern(lr_ref, li_ref, ldt_ref, br_ref, bi_ref, cr_ref, ci_ref,
             wio_ref, wis_ref, wso_ref, tab_ref):
        lr = lr_ref[...]
        li = li_ref[...]
        dt = jnp.exp(ldt_ref[...])
        mag = jnp.exp(lr * dt)
        ab_re = mag * jnp.cos(li * dt)
        ab_im = mag * jnp.sin(li * dt)
        nr, ni = ab_re - 1.0, ab_im
        den = lr * lr + li * li
        f_re = (nr * lr + ni * li) / den
        f_im = (ni * lr - nr * li) / den
        br, bi = br_ref[...], bi_ref[...]
        bb_re = f_re * br - f_im * bi
        bb_im = f_re * bi + f_im * br
        cr, ci = cr_ref[...], ci_ref[...]
        gid = lax.broadcasted_iota(jnp.int32, (1, half), 1) // p

        def expand(v):
            return jnp.concatenate([jnp.where(gid == g, v, 0.0) for g in range(tg)], axis=0)

        pows = [(jnp.ones_like(ab_re), jnp.zeros_like(ab_im))]
        for _ in range(tc):
            pows.append(_cmul(pows[-1][0], pows[-1][1], ab_re, ab_im))

        for t in range(tc):
            pr, pi = pows[tc - 1 - t]
            wr, wi = _cmul(pr, pi, bb_re, bb_im)
            rows = slice(t * LANES, (t + 1) * LANES)
            wis_ref[rows, 0:half] = expand(wr).astype(BF16)
            wis_ref[rows, half:2 * half] = expand(wi).astype(BF16)

        bexp = jnp.concatenate([expand(bb_re), expand(bb_im)], axis=1)
        blocks = []
        for tau in range(tc + 1):
            pr, pi = pows[tau]
            qr, qi = _cmul(cr, ci, pr, pi)
            er, ei = expand(qr), expand(-qi)
            if tau >= 1:
                cols = slice((tau - 1) * LANES, tau * LANES)
                wso_ref[0:half, cols] = er.T.astype(BF16)
                wso_ref[half:2 * half, cols] = ei.T.astype(BF16)
            if tau < tc:
                qexp = jnp.concatenate([er, ei], axis=1)
                blk = lax.dot_general(bexp, qexp, (((1,), (1,)), ((), ())),
                                      precision=lax.Precision.HIGHEST,
                                      preferred_element_type=F32)
                blocks.append(blk.astype(BF16))
        zero = jnp.zeros((LANES, LANES), BF16)
        for t in range(tc):
            for t2 in range(tc):
                wio_ref[t * LANES:(t + 1) * LANES, t2 * LANES:(t2 + 1) * LANES] = (
                    blocks[t2 - t] if t2 >= t else zero)

        lam = [(jnp.ones_like(ab_re), jnp.zeros_like(ab_im))]
        for _ in range(SUBLANES):
            lam.append(_cmul(lam[-1][0], lam[-1][1], pows[tc][0], pows[tc][1]))
        zrow = jnp.zeros_like(ab_re)
        groups = [[lam[r] for r in range(SUBLANES)]]
        for d in (1, 2, 4):
            groups.append([lam[d] if r >= d else (zrow, zrow) for r in range(SUBLANES)])
        groups.append([lam[SUBLANES]] * SUBLANES)
        for gi, grp in enumerate(groups):
            rows = slice(gi * SUBLANES, (gi + 1) * SUBLANES)
            tab_ref[rows, 0:half] = jnp.concatenate([v[0] for v in grp], axis=0)
            tab_ref[rows, half:2 * half] = jnp.concatenate([v[1] for v in grp], axis=0)

    return kern


def _s5_operators(lam_re, lam_im, log_dt, b_re, b_im, c_re, c_im):
    g, p = lam_re.shape
    gc = b_re.shape[-1]
    gp = g * p
    tg = S5_TILE_GROUPS
    assert gc * tg == LANES and g % tg == 0
    nj = g // tg
    half = tg * p
    kdim = S5_CHUNK * LANES
    lr = lam_re.reshape(1, gp)
    li = lam_im.reshape(1, gp)
    ldt = jnp.repeat(log_dt, p).reshape(1, gp)
    brt = b_re.transpose(2, 0, 1).reshape(gc, gp)
    bit = b_im.transpose(2, 0, 1).reshape(gc, gp)
    crt = c_re.transpose(1, 0, 2).reshape(gc, gp)
    cit = c_im.transpose(1, 0, 2).reshape(gc, gp)
    row = pl.BlockSpec((1, half), lambda j: (0, j))
    mat = pl.BlockSpec((gc, half), lambda j: (0, j))
    return pl.pallas_call(
        _make_s5_prep_kernel(tg, p),
        grid=(nj,),
        in_specs=[row, row, row, mat, mat, mat, mat],
        out_specs=[
            pl.BlockSpec((None, kdim, kdim), lambda j: (j, 0, 0)),
            pl.BlockSpec((None, kdim, 2 * half), lambda j: (j, 0, 0)),
            pl.BlockSpec((None, 2 * half, kdim), lambda j: (j, 0, 0)),
            pl.BlockSpec((None, 5 * SUBLANES, 2 * half), lambda j: (j, 0, 0)),
        ],
        out_shape=[
            jax.ShapeDtypeStruct((nj, kdim, kdim), BF16),
            jax.ShapeDtypeStruct((nj, kdim, 2 * half), BF16),
            jax.ShapeDtypeStruct((nj, 2 * half, kdim), BF16),
            jax.ShapeDtypeStruct((nj, 5 * SUBLANES, 2 * half), F32),
        ],
        compiler_params=_params(("parallel",)),
        name="s5_prep",
    )(lr, li, ldt, brt, bit, crt, cit)


def _make_s5_kernel(nb, chunks_per_batch, half):
    tc = S5_CHUNK
    nchunk = nb * chunks_per_batch
    tiles_per_batch = chunks_per_batch // SUBLANES
    ntile = nchunk // SUBLANES

    def kern(u_ref, wio_ref, wis_ref, wso_ref, tab_ref, d_ref, y_ref, s_scr, y_scr):
        lhs = jnp.concatenate(
            [u_ref[pl.ds(t, nchunk, stride=tc), :].astype(BF16) for t in range(tc)], axis=1)
        s_scr[...] = _dot(lhs, wis_ref[...])

        tab = tab_ref[...]
        pw_r, pw_i = tab[0:8, :half], tab[0:8, half:]
        steps = [(d, tab[8 * i:8 * i + 8, :half], tab[8 * i:8 * i + 8, half:])
                 for i, d in ((1, 1), (2, 2), (3, 4))]
        l8_r, l8_i = tab[32:33, :half], tab[32:33, half:]
        first_row = lax.broadcasted_iota(jnp.int32, (SUBLANES, half), 0) == 0

        def tile_scan(m, carry):
            cr, ci = carry
            keep = jnp.where(m % tiles_per_batch == 0, 0.0, 1.0)
            cr, ci = cr * keep, ci * keep
            rows = pl.ds(pl.multiple_of(m * SUBLANES, SUBLANES), SUBLANES)
            s = s_scr[rows, :]
            er, ei = s[:, :half], s[:, half:]
            for d, mr, mi in steps:
                rr = pltpu.roll(er, d, axis=0)
                ri = pltpu.roll(ei, d, axis=0)
                er, ei = er + (mr * rr - mi * ri), ei + (mr * ri + mi * rr)
            xr = jnp.where(first_row, 0.0, pltpu.roll(er, 1, axis=0))
            xi = jnp.where(first_row, 0.0, pltpu.roll(ei, 1, axis=0))
            hr = xr + (pw_r * cr - pw_i * ci)
            hi = xi + (pw_r * ci + pw_i * cr)
            s_scr[rows, :] = jnp.concatenate([hr, hi], axis=1)
            ncr = er[7:8] + (l8_r * cr - l8_i * ci)
            nci = ei[7:8] + (l8_r * ci + l8_i * cr)
            return ncr, nci

        zero = jnp.zeros((1, half), F32)
        lax.fori_loop(0, ntile, tile_scan, (zero, zero), unroll=2)

        y = _dot(lhs, wio_ref[...]) + _dot(s_scr[...].astype(BF16), wso_ref[...])
        for t in range(tc):
            y_scr[pl.ds(t, nchunk, stride=tc), :] = y[:, t * LANES:(t + 1) * LANES]
        y_ref[...] = jax.nn.gelu(y_scr[...] + d_ref[...] * u_ref[...]).astype(BF16)

    return kern


def _s5_core(u3, ops, d_skip, *, nb, seq):
    w_io, w_is, w_so, tab = ops
    nj, rows, _ = u3.shape
    kdim = S5_CHUNK * LANES
    sdim = w_is.shape[-1]
    chunks_per_batch = seq // S5_CHUNK
    assert rows == nb * seq and chunks_per_batch % SUBLANES == 0
    tok = pl.BlockSpec((None, rows, LANES), lambda j: (j, 0, 0))
    return pl.pallas_call(
        _make_s5_kernel(nb, chunks_per_batch, sdim // 2),
        grid=(nj,),
        in_specs=[
            tok,
            pl.BlockSpec((None, kdim, kdim), lambda j: (j, 0, 0)),
            pl.BlockSpec((None, kdim, sdim), lambda j: (j, 0, 0)),
            pl.BlockSpec((None, sdim, kdim), lambda j: (j, 0, 0)),
            pl.BlockSpec((None, 5 * SUBLANES, sdim), lambda j: (j, 0, 0)),
            pl.BlockSpec((1, LANES), lambda j: (0, j)),
        ],
        out_specs=tok,
        out_shape=jax.ShapeDtypeStruct((nj, rows, LANES), BF16),
        scratch_shapes=[pltpu.VMEM((rows // S5_CHUNK, sdim), F32), pltpu.VMEM((rows, LANES), F32)],
        compiler_params=_params(("parallel",)),
        name="s5_core",
    )(u3, w_io, w_is, w_so, tab, d_skip.reshape(1, nj * LANES))


def _band_windows(width, blk, tn):
    starts, ends = [], []
    for n in range(width // tn):
        h_lo = (n * tn) // blk
        h_hi = (n * tn + tn - 1) // blk
        starts.append((h_lo * blk) // LANES)
        ends.append(-(-((h_hi + 1) * blk) // LANES))
    kw = max(e - s for s, e in zip(starts, ends))
    total = width // LANES
    starts = [min(s, total - kw) for s in starts]
    return starts, kw * LANES


def _banded(w, starts, kw, tn):
    nblk, blk, _ = w.shape
    tiles = []
    for n, s in enumerate(starts):
        pieces = []
        for h in range((n * tn) // blk, (n * tn + tn - 1) // blk + 1):
            j0 = max(0, n * tn - h * blk)
            j1 = min(blk, (n + 1) * tn - h * blk)
            r0 = h * blk - s * LANES
            pieces.append(jnp.pad(w[h][:, j0:j1], ((r0, kw - r0 - blk), (0, 0))))
        tiles.append(jnp.concatenate(pieces, axis=1))
    return jnp.stack(tiles).astype(BF16)


def _make_lru_kernel(e, tm, tn, starts, kw, tiles_per_seq, col_groups):
    halo = SUBLANES
    ntile = e // tn
    nrow = tm // SUBLANES

    def kern(gx_ref, cw_ref, cb_ref, wrg_ref, wig_ref, brg_ref, big_ref, lam_ref, y_ref,
             xpad, xc_scr, a_scr, b_scr, hcar):
        @pl.when(pl.program_id(0) % tiles_per_seq == 0)
        def _():
            xpad[0:halo, :] = jnp.zeros((halo, e), F32)
            hcar[...] = jnp.zeros((1, e), F32)

        xpad[halo:halo + tm, :] = gx_ref[:, e:2 * e]
        xc = cb_ref[...] + cw_ref[3:4, :] * xpad[halo:halo + tm, :]
        for k in range(3):
            xc = xc + cw_ref[k:k + 1, :] * xpad[halo - 3 + k:halo - 3 + k + tm, :]
        xpad[0:halo, :] = xpad[tm:tm + halo, :]
        xc_scr[...] = xc

        lam = lam_ref[...]
        neg = -lam
        softplus = jnp.maximum(neg, 0.0) + jnp.log1p(jnp.exp(-jnp.abs(neg)))

        for n in range(ntile):
            cols = slice(n * tn, (n + 1) * tn)
            win = xc_scr[:, starts[n] * LANES:starts[n] * LANES + kw].astype(BF16)
            r = jax.nn.sigmoid(_dot(win, wrg_ref[n]) + brg_ref[:, cols])
            ig = jax.nn.sigmoid(_dot(win, wig_ref[n]) + big_ref[:, cols])
            a = jnp.exp((-LRU_C) * r * softplus[:, cols])
            om = 1.0 - a * a
            mult = jnp.where(om > 0.0, om * lax.rsqrt(om), 0.0)
            a_scr[:, cols] = a
            b_scr[:, cols] = mult * (ig * xc_scr[:, cols])

        for c0, c1 in col_groups:
            row = lax.broadcasted_iota(jnp.int32, (SUBLANES, c1 - c0), 0)
            masks = [row >= d for d in (1, 2, 4)]

            def tile_scan(m, c, c0=c0, c1=c1, masks=masks):
                rows = pl.ds(pl.multiple_of(m * SUBLANES, SUBLANES), SUBLANES)
                av = a_scr[rows, c0:c1]
                bv = b_scr[rows, c0:c1]
                for d, keep in zip((1, 2, 4), masks):
                    bv = bv + jnp.where(keep, av * pltpu.roll(bv, d, axis=0), 0.0)
                    av = jnp.where(keep, av * pltpu.roll(av, d, axis=0), av)
                h = bv + av * c
                b_scr[rows, c0:c1] = h
                return h[7:8]

            hcar[:, c0:c1] = lax.fori_loop(0, nrow, tile_scan, hcar[:, c0:c1], unroll=2)

        y_ref[...] = (b_scr[...] * jax.nn.gelu(gx_ref[:, 0:e])).astype(BF16)

    return kern


def _lru_core(gx, conv_w, conv_b, w_rg, b_rg, w_ig, b_ig, lam, *, seq, tm):
    rows, e2 = gx.shape
    e = e2 // 2
    blk = w_rg.shape[1]
    tn = MXU_N
    assert e % tn == 0 and seq % tm == 0 and tm % SUBLANES == 0
    starts, kw = _band_windows(e, blk, tn)
    wrg, wig = _banded(w_rg, starts, kw, tn), _banded(w_ig, starts, kw, tn)
    ntile = e // tn
    group = 8 * LANES
    col_groups = [(c, min(c + group, e)) for c in range(0, e, group)]
    full = lambda shape: pl.BlockSpec(shape, lambda m: (0,) * len(shape))
    return pl.pallas_call(
        _make_lru_kernel(e, tm, tn, starts, kw, seq // tm, col_groups),
        grid=(rows // tm,),
        in_specs=[
            pl.BlockSpec((tm, e2), lambda m: (m, 0)),
            full((conv_w.shape[0], e)),
            full((1, e)),
            full((ntile, kw, tn)),
            full((ntile, kw, tn)),
            full((1, e)),
            full((1, e)),
            full((1, e)),
        ],
        out_specs=pl.BlockSpec((tm, e), lambda m: (m, 0)),
        out_shape=jax.ShapeDtypeStruct((rows, e), BF16),
        scratch_shapes=[
            pltpu.VMEM((tm + SUBLANES, e), F32),
            pltpu.VMEM((tm, e), F32),
            pltpu.VMEM((tm, e), F32),
            pltpu.VMEM((tm, e), F32),
            pltpu.VMEM((1, e), F32),
        ],
        compiler_params=_params(("arbitrary",)),
        name="lru_core",
    )(gx, conv_w.reshape(conv_w.shape[0], e), conv_b.reshape(1, e), wrg, wig,
      b_rg.reshape(1, e), b_ig.reshape(1, e), lam.reshape(1, e))


def _rms_kernel(x_ref, g_ref, o_ref):
    x = x_ref[...]
    ms = jnp.mean(x * x, axis=-1, keepdims=True)
    o_ref[...] = x * lax.rsqrt(ms + EPS) * g_ref[...]


def _final_norm(x2d, gain, tm):
    rows, d = x2d.shape
    return pl.pallas_call(
        _rms_kernel,
        grid=(rows // tm,),
        in_specs=[pl.BlockSpec((tm, d), lambda m: (m, 0)), pl.BlockSpec((1, d), lambda m: (0, 0))],
        out_specs=pl.BlockSpec((tm, d), lambda m: (m, 0)),
        out_shape=jax.ShapeDtypeStruct((rows, d), F32),
        compiler_params=_params(("parallel",)),
        name="final_norm",
    )(x2d, gain.reshape(1, d))


def _tile(n, pref):
    t = min(n, pref)
    while n % t:
        t //= 2
    return t


def kernel(x, c, norm_g, w_ada, b_ada, s5_w_in, s5_lam_re, s5_lam_im, s5_log_dt, s5_b_re, s5_b_im, s5_c_re, s5_c_im, s5_d, s5_w_glu, lru_w_in, lru_conv_w, lru_conv_b, lru_w_rg, lru_b_rg, lru_w_ig, lru_b_ig, lru_lam, lru_w_out, ffn_w_gu, ffn_w_down, final_g):
    bsz, seq, d = x.shape
    depth = w_ada.shape[0]
    rows = bsz * seq
    x2 = x.reshape(rows, d)
    mods = _modulation(c, w_ada, b_ada).reshape(depth, bsz, 6, 1, d)
    tm = _tile(seq, 512)
    tnd = _tile(d, 512)

    for i in range(depth):
        sh1, sc1, g1, sh2, sc2, g2 = [mods[i, :, q] for q in range(6)]
        j = i // 2
        if i % 2 == 0:
            u3 = _ln_matmul(x2, norm_g[i, 0], sc1, sh1, s5_w_in[j].astype(BF16),
                            rows_per_batch=seq, tm=tm, tn=s5_w_in.shape[-1], swiglu=False,
                            out_dtype=F32, tiled_out=True, name="s5_in")
            ops = _s5_operators(s5_lam_re[j], s5_lam_im[j], s5_log_dt[j], s5_b_re[j], s5_b_im[j],
                                s5_c_re[j], s5_c_im[j])
            y3 = _s5_core(u3, ops, s5_d[j], nb=bsz, seq=seq)
            x2 = _matmul_residual(y3, s5_w_glu[j].astype(BF16), x2, g1, rows_per_batch=seq,
                                  tm=tm, tn=tnd, glu=True, tiled_a=True, name="s5_out")
        else:
            e = lru_w_in.shape[-1] // 2
            gx = _ln_matmul(x2, norm_g[i, 0], sc1, sh1, lru_w_in[j].astype(BF16),
                            rows_per_batch=seq, tm=tm, tn=_tile(2 * e, 512), swiglu=False,
                            out_dtype=F32, name="lru_in")
            y = _lru_core(gx, lru_conv_w[j], lru_conv_b[j], lru_w_rg[j], lru_b_rg[j],
                          lru_w_ig[j], lru_b_ig[j], lru_lam[j], seq=seq, tm=_tile(seq, 256))
            x2 = _matmul_residual(y, lru_w_out[j].astype(BF16), x2, g1, rows_per_batch=seq,
                                  tm=tm, tn=tnd, glu=False, name="lru_out")
        hidden = ffn_w_down.shape[1]
        act = _ln_matmul(x2, norm_g[i, 1], sc2, sh2, ffn_w_gu[i].astype(BF16), rows_per_batch=seq,
                         tm=tm, tn=_tile(hidden, 512), swiglu=True, out_dtype=BF16, name="ffn_up")
        x2 = _matmul_residual(act, ffn_w_down[i].astype(BF16), x2, g2, rows_per_batch=seq, tm=tm,
                              tn=tnd, glu=False, name="ffn_down")

    return _final_norm(x2, final_g, tm).reshape(bsz, seq, d)
```

```python
import jax
import jax.numpy as jnp
from jax import lax
from jax.experimental import pallas as pl
from jax.experimental.pallas import tpu as pltpu

F32 = jnp.float32
BF16 = jnp.bfloat16

EPS = 1e-6
LRU_C = 8.0
LANES = 128
SUBLANES = 8
MXU_N = 256
S5_CHUNK = 8
S5_TILE_GROUPS = 8
VMEM_LIMIT = 56 * 1024 * 1024


def _params(semantics, vmem=VMEM_LIMIT):
    return pltpu.CompilerParams(dimension_semantics=semantics, vmem_limit_bytes=vmem)


def _dot(a, b):
    return jnp.dot(a, b, preferred_element_type=F32)


def _rms_mod(x, gain, scale, shift):
    ms = jnp.mean(x * x, axis=-1, keepdims=True)
    y = x * lax.rsqrt(ms + EPS) * gain
    return y * (1.0 + scale) + shift


def _cmul(ar, ai, br, bi):
    return ar * br - ai * bi, ar * bi + ai * br


def _mod_kernel(c_ref, w_ref, b_ref, o_ref):
    c = c_ref[...]
    cond = c * jax.nn.sigmoid(c)
    o_ref[...] = _dot(cond.astype(BF16), w_ref[...].astype(BF16)) + b_ref[...]


def _modulation(c, w_ada, b_ada):
    depth, d, n = w_ada.shape
    b = c.shape[0]
    rows = -(-b // SUBLANES) * SUBLANES
    cp = jnp.zeros((rows, d), F32).at[:b].set(c)
    tn = 1024 if n % 1024 == 0 else n
    out = pl.pallas_call(
        _mod_kernel,
        grid=(depth, n // tn),
        in_specs=[
            pl.BlockSpec((rows, d), lambda i, j: (0, 0)),
            pl.BlockSpec((None, d, tn), lambda i, j: (i, 0, j)),
            pl.BlockSpec((None, 1, tn), lambda i, j: (i, 0, j)),
        ],
        out_specs=pl.BlockSpec((None, rows, tn), lambda i, j: (i, 0, j)),
        out_shape=jax.ShapeDtypeStruct((depth, rows, n), F32),
        compiler_params=_params(("parallel", "parallel")),
        name="adaln_mod",
    )(cp, w_ada, b_ada.reshape(depth, 1, n))
    return out[:, :b]


def _resident(w_all, layer):
    _, k, n = w_all.shape
    return pl.BlockSpec((None, k, n), lambda m: (layer, 0, 0), pipeline_mode=pl.Buffered(1))


def _batch_vec(d, tm, seq):
    assert seq % tm == 0
    return pl.BlockSpec((1, 1, d), lambda m: ((m * tm) // seq, 0, 0))


def _ln_mm_kernel(x_ref, g_ref, sc_ref, sh_ref, w_ref, o_ref):
    h = _rms_mod(x_ref[...], g_ref[...], sc_ref[0], sh_ref[0]).astype(BF16)
    acc = _dot(h, w_ref[...])
    for q in range(o_ref.shape[0]):
        o_ref[q] = acc[:, q * LANES:(q + 1) * LANES]


def _ln_matmul_tiled(x2d, gain, scale, shift, w_all, layer, *, seq, tm, name):
    rows, d = x2d.shape
    n_out = w_all.shape[-1]
    assert rows % tm == 0 and n_out % LANES == 0
    vec = _batch_vec(d, tm, seq)
    return pl.pallas_call(
        _ln_mm_kernel,
        grid=(rows // tm,),
        in_specs=[
            pl.BlockSpec((tm, d), lambda m: (m, 0)),
            pl.BlockSpec((1, d), lambda m: (0, 0)),
            vec,
            vec,
            _resident(w_all, layer),
        ],
        out_specs=pl.BlockSpec((n_out // LANES, tm, LANES), lambda m: (0, m, 0)),
        out_shape=jax.ShapeDtypeStruct((n_out // LANES, rows, LANES), F32),
        compiler_params=_params(("parallel",)),
        name=name,
    )(x2d, gain.reshape(1, d), scale, shift, w_all)


def _mm_kernel(a_ref, w_ref, o_ref):
    o_ref[...] = _dot(a_ref[...], w_ref[...])


def _matmul_plain(a, w_all, layer, *, tm, name):
    rows, k = a.shape
    n_out = w_all.shape[-1]
    assert rows % tm == 0
    return pl.pallas_call(
        _mm_kernel,
        grid=(rows // tm,),
        in_specs=[pl.BlockSpec((tm, k), lambda m: (m, 0)), _resident(w_all, layer)],
        out_specs=pl.BlockSpec((tm, n_out), lambda m: (m, 0)),
        out_shape=jax.ShapeDtypeStruct((rows, n_out), F32),
        compiler_params=_params(("parallel",)),
        name=name,
    )(a, w_all)


def _swiglu_kernel(h_ref, wg_ref, wu_ref, o_ref):
    h = h_ref[...]
    g = _dot(h, wg_ref[...])
    o_ref[...] = (g * jax.nn.sigmoid(g) * _dot(h, wu_ref[...])).astype(BF16)


def _swiglu_up(h, w_all, layer, *, tm, tn, name):
    rows, d = h.shape
    n_out = w_all.shape[-1] // 2
    assert rows % tm == 0 and n_out % tn == 0
    nt = n_out // tn
    return pl.pallas_call(
        _swiglu_kernel,
        grid=(rows // tm, nt),
        in_specs=[
            pl.BlockSpec((tm, d), lambda m, n: (m, 0)),
            pl.BlockSpec((None, d, tn), lambda m, n: (layer, 0, n)),
            pl.BlockSpec((None, d, tn), lambda m, n: (layer, 0, n + nt)),
        ],
        out_specs=pl.BlockSpec((tm, tn), lambda m, n: (m, n)),
        out_shape=jax.ShapeDtypeStruct((rows, n_out), BF16),
        compiler_params=_params(("parallel", "parallel")),
        name=name,
    )(h, w_all, w_all)


def _make_mm_res_kernel(glu, tiled_a, n_out, mode):
    def kern(*refs):
        a_ref, w_ref, res_ref, gate_ref = refs[:4]
        if tiled_a:
            a = jnp.concatenate([a_ref[q] for q in range(a_ref.shape[0])], axis=1)
        else:
            a = a_ref[...]
        acc = _dot(a, w_ref[...])
        if glu:
            acc = acc[:, :n_out] * jax.nn.sigmoid(acc[:, n_out:])
        x = res_ref[...] + gate_ref[0] * acc
        if mode == "final":
            g_ref, o_ref = refs[4:]
            ms = jnp.mean(x * x, axis=-1, keepdims=True)
            o_ref[...] = x * lax.rsqrt(ms + EPS) * g_ref[...]
        elif mode == "x+h":
            g_ref, sc_ref, sh_ref, x_out, h_out = refs[4:]
            x_out[...] = x
            h_out[...] = _rms_mod(x, g_ref[...], sc_ref[0], sh_ref[0]).astype(BF16)
        else:
            (x_out,) = refs[4:]
            x_out[...] = x

    return kern


def _matmul_residual(a, w_all, layer, res2d, gate, *, seq, tm, glu=False, tiled_a=False,
                     next_norm=None, final_gain=None, name):
    rows, n_out = res2d.shape
    k = w_all.shape[1]
    assert rows % tm == 0
    if tiled_a:
        a_spec = pl.BlockSpec((k // LANES, tm, LANES), lambda m: (0, m, 0))
    else:
        a_spec = pl.BlockSpec((tm, k), lambda m: (m, 0))
    row_spec = pl.BlockSpec((tm, n_out), lambda m: (m, 0))
    gain_spec = pl.BlockSpec((1, n_out), lambda m: (0, 0))
    vec = _batch_vec(n_out, tm, seq)
    in_specs = [a_spec, _resident(w_all, layer), row_spec, vec]
    args = [a, w_all, res2d, gate]
    x_shape = jax.ShapeDtypeStruct((rows, n_out), F32)
    if final_gain is not None:
        mode = "final"
        in_specs.append(gain_spec)
        args.append(final_gain.reshape(1, n_out))
        out_specs, out_shape = row_spec, x_shape
    elif next_norm is not None:
        mode = "x+h"
        gain, scale, shift = next_norm
        in_specs += [gain_spec, vec, vec]
        args += [gain.reshape(1, n_out), scale, shift]
        out_specs = [row_spec, row_spec]
        out_shape = [x_shape, jax.ShapeDtypeStruct((rows, n_out), BF16)]
    else:
        mode = "x"
        out_specs, out_shape = [row_spec], [x_shape]
    out = pl.pallas_call(
        _make_mm_res_kernel(glu, tiled_a, n_out, mode),
        grid=(rows // tm,),
        in_specs=in_specs,
        out_specs=out_specs,
        out_shape=out_shape,
        compiler_params=_params(("parallel",)),
        name=name,
    )(*args)
    if mode == "final":
        return out
    return (out[0], out[1]) if mode == "x+h" else (out[0], None)


def _make_s5_prep_kernel(tg, p):
    tc = S5_CHUNK
    half = tg * p

    def kern(lr_ref, li_ref, ldt_ref, br_ref, bi_ref, cr_ref, ci_ref,
             wio_ref, wis_ref, wso_ref, tab_ref):
        lr = lr_ref[...]
        li = li_ref[...]
        dt = jnp.exp(ldt_ref[...])
        mag = jnp.exp(lr * dt)
        ab_re = mag * jnp.cos(li * dt)
        ab_im = mag * jnp.sin(li * dt)
        nr, ni = ab_re - 1.0, ab_im
        den = lr * lr + li * li
        f_re = (nr * lr + ni * li) / den
        f_im = (ni * lr - nr * li) / den
        br, bi = br_ref[...], bi_ref[...]
        bb_re = f_re * br - f_im * bi
        bb_im = f_re * bi + f_im * br
        cr, ci = cr_ref[...], ci_ref[...]
        gid = lax.broadcasted_iota(jnp.int32, (1, half), 1) // p

        def expand(v):
            return jnp.concatenate([jnp.where(gid == g, v, 0.0) for g in range(tg)], axis=0)

        pows = [(jnp.ones_like(ab_re), jnp.zeros_like(ab_im))]
        for _ in range(tc):
            pows.append(_cmul(pows[-1][0], pows[-1][1], ab_re, ab_im))

        for t in range(tc):
            pr, pi = pows[tc - 1 - t]
            wr, wi = _cmul(pr, pi, bb_re, bb_im)
            rows = slice(t * LANES, (t + 1) * LANES)
            wis_ref[rows, 0:half] = expand(wr).astype(BF16)
            wis_ref[rows, half:2 * half] = expand(wi).astype(BF16)

        bexp = jnp.concatenate([expand(bb_re), expand(bb_im)], axis=1)
        blocks = []
        for tau in range(tc + 1):
            pr, pi = pows[tau]
            qr, qi = _cmul(cr, ci, pr, pi)
            er, ei = expand(qr), expand(-qi)
            if tau >= 1:
                cols = slice((tau - 1) * LANES, tau * LANES)
                wso_ref[0:half, cols] = er.T.astype(BF16)
                wso_ref[half:2 * half, cols] = ei.T.astype(BF16)
            if tau < tc:
                qexp = jnp.concatenate([er, ei], axis=1)
                blk = lax.dot_general(bexp, qexp, (((1,), (1,)), ((), ())),
                                      precision=lax.Precision.HIGHEST,
                                      preferred_element_type=F32)
                blocks.append(blk.astype(BF16))
        zero = jnp.zeros((LANES, LANES), BF16)
        for t in range(tc):
            for t2 in range(tc):
                wio_ref[t * LANES:(t + 1) * LANES, t2 * LANES:(t2 + 1) * LANES] = (
                    blocks[t2 - t] if t2 >= t else zero)

        lam = [(jnp.ones_like(ab_re), jnp.zeros_like(ab_im))]
        for _ in range(SUBLANES):
            lam.append(_cmul(lam[-1][0], lam[-1][1], pows[tc][0], pows[tc][1]))
        zrow = jnp.zeros_like(ab_re)
        groups = [[lam[r] for r in range(SUBLANES)]]
        for d in (1, 2, 4):
            groups.append([lam[d] if r >= d else (zrow, zrow) for r in range(SUBLANES)])
        groups.append([lam[SUBLANES]] * SUBLANES)
        for gi, grp in enumerate(groups):
            rows = slice(gi * SUBLANES, (gi + 1) * SUBLANES)
            tab_ref[rows, 0:half] = jnp.concatenate([v[0] for v in grp], axis=0)
            tab_ref[rows, half:2 * half] = jnp.concatenate([v[1] for v in grp], axis=0)

    return kern


def _s5_operators(lam_re, lam_im, log_dt, b_re, b_im, c_re, c_im):
    g, p = lam_re.shape
    gc = b_re.shape[-1]
    gp = g * p
    tg = S5_TILE_GROUPS
    assert gc * tg == LANES and g % tg == 0
    nj = g // tg
    half = tg * p
    kdim = S5_CHUNK * LANES
    lr = lam_re.reshape(1, gp)
    li = lam_im.reshape(1, gp)
    ldt = jnp.repeat(log_dt, p).reshape(1, gp)
    brt = b_re.transpose(2, 0, 1).reshape(gc, gp)
    bit = b_im.transpose(2, 0, 1).reshape(gc, gp)
    crt = c_re.transpose(1, 0, 2).reshape(gc, gp)
    cit = c_im.transpose(1, 0, 2).reshape(gc, gp)
    row = pl.BlockSpec((1, half), lambda j: (0, j))
    mat = pl.BlockSpec((gc, half), lambda j: (0, j))
    return pl.pallas_call(
        _make_s5_prep_kernel(tg, p),
        grid=(nj,),
        in_specs=[row, row, row, mat, mat, mat, mat],
        out_specs=[
            pl.BlockSpec((None, kdim, kdim), lambda j: (j, 0, 0)),
            pl.BlockSpec((None, kdim, 2 * half), lambda j: (j, 0, 0)),
            pl.BlockSpec((None, 2 * half, kdim), lambda j: (j, 0, 0)),
            pl.BlockSpec((None, 5 * SUBLANES, 2 * half), lambda j: (j, 0, 0)),
        ],
        out_shape=[
            jax.ShapeDtypeStruct((nj, kdim, kdim), BF16),
            jax.ShapeDtypeStruct((nj, kdim, 2 * half), BF16),
            jax.ShapeDtypeStruct((nj, 2 * half, kdim), BF16),
            jax.ShapeDtypeStruct((nj, 5 * SUBLANES, 2 * half), F32),
        ],
        compiler_params=_params(("parallel",)),
        name="s5_prep",
    )(lr, li, ldt, brt, bit, crt, cit)


def _make_s5_kernel(nb, chunks_per_batch, half):
    tc = S5_CHUNK
    nchunk = nb * chunks_per_batch
    tiles_per_batch = chunks_per_batch // SUBLANES
    ntile = nchunk // SUBLANES

    def kern(u_ref, wio_ref, wis_ref, wso_ref, tab_ref, d_ref, y_ref, s_scr, y_scr):
        lhs = jnp.concatenate(
            [u_ref[pl.ds(t, nchunk, stride=tc), :].astype(BF16) for t in range(tc)], axis=1)
        s_scr[...] = _dot(lhs, wis_ref[...])

        tab = tab_ref[...]
        pw_r, pw_i = tab[0:8, :half], tab[0:8, half:]
        steps = [(d, tab[8 * i:8 * i + 8, :half], tab[8 * i:8 * i + 8, half:])
                 for i, d in ((1, 1), (2, 2), (3, 4))]
        l8_r, l8_i = tab[32:33, :half], tab[32:33, half:]
        first_row = lax.broadcasted_iota(jnp.int32, (SUBLANES, half), 0) == 0

        def tile_scan(m, carry):
            cr, ci = carry
            keep = jnp.where(m % tiles_per_batch == 0, 0.0, 1.0)
            cr, ci = cr * keep, ci * keep
            rows = pl.ds(pl.multiple_of(m * SUBLANES, SUBLANES), SUBLANES)
            s = s_scr[rows, :]
            er, ei = s[:, :half], s[:, half:]
            for d, mr, mi in steps:
                rr = pltpu.roll(er, d, axis=0)
                ri = pltpu.roll(ei, d, axis=0)
                er, ei = er + (mr * rr - mi * ri), ei + (mr * ri + mi * rr)
            xr = jnp.where(first_row, 0.0, pltpu.roll(er, 1, axis=0))
            xi = jnp.where(first_row, 0.0, pltpu.roll(ei, 1, axis=0))
            hr = xr + (pw_r * cr - pw_i * ci)
            hi = xi + (pw_r * ci + pw_i * cr)
            s_scr[rows, :] = jnp.concatenate([hr, hi], axis=1)
            ncr = er[7:8] + (l8_r * cr - l8_i * ci)
            nci = ei[7:8] + (l8_r * ci + l8_i * cr)
            return ncr, nci

        zero = jnp.zeros((1, half), F32)
        lax.fori_loop(0, ntile, tile_scan, (zero, zero), unroll=2)

        y = _dot(lhs, wio_ref[...]) + _dot(s_scr[...].astype(BF16), wso_ref[...])
        for t in range(tc):
            y_scr[pl.ds(t, nchunk, stride=tc), :] = y[:, t * LANES:(t + 1) * LANES]
        y_ref[...] = jax.nn.gelu(y_scr[...] + d_ref[...] * u_ref[...]).astype(BF16)

    return kern


def _s5_core(u3, ops, d_skip, *, nb, seq):
    w_io, w_is, w_so, tab = ops
    nj, rows, _ = u3.shape
    kdim = S5_CHUNK * LANES
    sdim = w_is.shape[-1]
    chunks_per_batch = seq // S5_CHUNK
    assert rows == nb * seq and chunks_per_batch % SUBLANES == 0
    tok = pl.BlockSpec((None, rows, LANES), lambda j: (j, 0, 0))
    return pl.pallas_call(
        _make_s5_kernel(nb, chunks_per_batch, sdim // 2),
        grid=(nj,),
        in_specs=[
            tok,
            pl.BlockSpec((None, kdim, kdim), lambda j: (j, 0, 0)),
            pl.BlockSpec((None, kdim, sdim), lambda j: (j, 0, 0)),
            pl.BlockSpec((None, sdim, kdim), lambda j: (j, 0, 0)),
            pl.BlockSpec((None, 5 * SUBLANES, sdim), lambda j: (j, 0, 0)),
            pl.BlockSpec((1, LANES), lambda j: (0, j)),
        ],
        out_specs=tok,
        out_shape=jax.ShapeDtypeStruct((nj, rows, LANES), BF16),
        scratch_shapes=[pltpu.VMEM((rows // S5_CHUNK, sdim), F32), pltpu.VMEM((rows, LANES), F32)],
        compiler_params=_params(("parallel",)),
        name="s5_core",
    )(u3, w_io, w_is, w_so, tab, d_skip.reshape(1, nj * LANES))


def _band_windows(width, blk, tn):
    starts, ends = [], []
    for n in range(width // tn):
        h_lo = (n * tn) // blk
        h_hi = (n * tn + tn - 1) // blk
        starts.append((h_lo * blk) // LANES)
        ends.append(-(-((h_hi + 1) * blk) // LANES))
    kw = max(e - s for s, e in zip(starts, ends))
    total = width // LANES
    starts = [min(s, total - kw) for s in starts]
    return starts, kw * LANES


def _banded(w, starts, kw, tn):
    nblk, blk, _ = w.shape
    tiles = []
    for n, s in enumerate(starts):
        pieces = []
        for h in range((n * tn) // blk, (n * tn + tn - 1) // blk + 1):
            j0 = max(0, n * tn - h * blk)
            j1 = min(blk, (n + 1) * tn - h * blk)
            r0 = h * blk - s * LANES
            pieces.append(jnp.pad(w[h][:, j0:j1], ((r0, kw - r0 - blk), (0, 0))))
        tiles.append(jnp.concatenate(pieces, axis=1))
    return jnp.stack(tiles).astype(BF16)


def _make_lru_kernel(e, tm, tn, starts, kw, tiles_per_seq, col_groups):
    halo = SUBLANES
    ntile = e // tn
    nrow = tm // SUBLANES

    def kern(gx_ref, cw_ref, cb_ref, wrg_ref, wig_ref, brg_ref, big_ref, lam_ref, y_ref,
             xpad, xc_scr, a_scr, b_scr, hcar):
        @pl.when(pl.program_id(0) % tiles_per_seq == 0)
        def _():
            xpad[0:halo, :] = jnp.zeros((halo, e), F32)
            hcar[...] = jnp.zeros((1, e), F32)

        xpad[halo:halo + tm, :] = gx_ref[:, e:2 * e]
        xc = cb_ref[...] + cw_ref[3:4, :] * xpad[halo:halo + tm, :]
        for k in range(3):
            xc = xc + cw_ref[k:k + 1, :] * xpad[halo - 3 + k:halo - 3 + k + tm, :]
        xpad[0:halo, :] = xpad[tm:tm + halo, :]
        xc_scr[...] = xc

        lam = lam_ref[...]
        neg = -lam
        softplus = jnp.maximum(neg, 0.0) + jnp.log1p(jnp.exp(-jnp.abs(neg)))

        for n in range(ntile):
            cols = slice(n * tn, (n + 1) * tn)
            win = xc_scr[:, starts[n] * LANES:starts[n] * LANES + kw].astype(BF16)
            r = jax.nn.sigmoid(_dot(win, wrg_ref[n]) + brg_ref[:, cols])
            ig = jax.nn.sigmoid(_dot(win, wig_ref[n]) + big_ref[:, cols])
            a = jnp.exp((-LRU_C) * r * softplus[:, cols])
            om = 1.0 - a * a
            mult = jnp.where(om > 0.0, om * lax.rsqrt(om), 0.0)
            a_scr[:, cols] = a
            b_scr[:, cols] = mult * (ig * xc_scr[:, cols])

        for c0, c1 in col_groups:
            row = lax.broadcasted_iota(jnp.int32, (SUBLANES, c1 - c0), 0)
            masks = [row >= d for d in (1, 2, 4)]

            def tile_scan(m, c, c0=c0, c1=c1, masks=masks):
                rows = pl.ds(pl.multiple_of(m * SUBLANES, SUBLANES), SUBLANES)
                av = a_scr[rows, c0:c1]
                bv = b_scr[rows, c0:c1]
                for d, keep in zip((1, 2, 4), masks):
                    bv = bv + jnp.where(keep, av * pltpu.roll(bv, d, axis=0), 0.0)
                    av = jnp.where(keep, av * pltpu.roll(av, d, axis=0), av)
                h = bv + av * c
                b_scr[rows, c0:c1] = h
                return h[7:8]

            hcar[:, c0:c1] = lax.fori_loop(0, nrow, tile_scan, hcar[:, c0:c1], unroll=2)

        y_ref[...] = (b_scr[...] * jax.nn.gelu(gx_ref[:, 0:e])).astype(BF16)

    return kern


def _lru_core(gx, conv_w, conv_b, w_rg, b_rg, w_ig, b_ig, lam, *, seq, tm):
    rows, e2 = gx.shape
    e = e2 // 2
    blk = w_rg.shape[1]
    tn = MXU_N
    assert e % tn == 0 and seq % tm == 0 and tm % SUBLANES == 0
    starts, kw = _band_windows(e, blk, tn)
    wrg, wig = _banded(w_rg, starts, kw, tn), _banded(w_ig, starts, kw, tn)
    ntile = e // tn
    group = 8 * LANES
    col_groups = [(c, min(c + group, e)) for c in range(0, e, group)]
    full = lambda shape: pl.BlockSpec(shape, lambda m: (0,) * len(shape))
    return pl.pallas_call(
        _make_lru_kernel(e, tm, tn, starts, kw, seq // tm, col_groups),
        grid=(rows // tm,),
        in_specs=[
            pl.BlockSpec((tm, e2), lambda m: (m, 0)),
            full((conv_w.shape[0], e)),
            full((1, e)),
            full((ntile, kw, tn)),
            full((ntile, kw, tn)),
            full((1, e)),
            full((1, e)),
            full((1, e)),
        ],
        out_specs=pl.BlockSpec((tm, e), lambda m: (m, 0)),
        out_shape=jax.ShapeDtypeStruct((rows, e), BF16),
        scratch_shapes=[
            pltpu.VMEM((tm + SUBLANES, e), F32),
            pltpu.VMEM((tm, e), F32),
            pltpu.VMEM((tm, e), F32),
            pltpu.VMEM((tm, e), F32),
            pltpu.VMEM((1, e), F32),
        ],
        compiler_params=_params(("arbitrary",)),
        name="lru_core",
    )(gx, conv_w.reshape(conv_w.shape[0], e), conv_b.reshape(1, e), wrg, wig,
      b_rg.reshape(1, e), b_ig.reshape(1, e), lam.reshape(1, e))


def _tile(n, pref):
    t = min(n, pref)
    while n % t:
        t //= 2
    return t


def kernel(x, c, norm_g, w_ada, b_ada, s5_w_in, s5_lam_re, s5_lam_im, s5_log_dt, s5_b_re, s5_b_im, s5_c_re, s5_c_im, s5_d, s5_w_glu, lru_w_in, lru_conv_w, lru_conv_b, lru_w_rg, lru_b_rg, lru_w_ig, lru_b_ig, lru_lam, lru_w_out, ffn_w_gu, ffn_w_down, final_g):
    bsz, seq, d = x.shape
    depth = w_ada.shape[0]
    rows = bsz * seq
    x2 = x.reshape(rows, d)
    mods = _modulation(c, w_ada, b_ada).reshape(depth, bsz, 6, 1, d)
    tm_in = _tile(seq, 512)
    tm_res = _tile(seq, 256)
    tm_up = _tile(rows, 2048)
    hidden = ffn_w_down.shape[1]
    w_s5_in, w_s5_glu = s5_w_in.astype(BF16), s5_w_glu.astype(BF16)
    w_lru_in, w_lru_out = lru_w_in.astype(BF16), lru_w_out.astype(BF16)
    w_gu, w_down = ffn_w_gu.astype(BF16), ffn_w_down.astype(BF16)

    h = None
    for i in range(depth):
        sh1, sc1, g1, sh2, sc2, g2 = [mods[i, :, q] for q in range(6)]
        ffn_norm = (norm_g[i, 1], sc2, sh2)
        j = i // 2
        if i % 2 == 0:
            u3 = _ln_matmul_tiled(x2, norm_g[i, 0], sc1, sh1, w_s5_in, j, seq=seq, tm=tm_in,
                                  name="s5_in")
            ops = _s5_operators(s5_lam_re[j], s5_lam_im[j], s5_log_dt[j], s5_b_re[j], s5_b_im[j],
                                s5_c_re[j], s5_c_im[j])
            y3 = _s5_core(u3, ops, s5_d[j], nb=bsz, seq=seq)
            x2, h = _matmul_residual(y3, w_s5_glu, j, x2, g1, seq=seq, tm=tm_res, glu=True,
                                     tiled_a=True, next_norm=ffn_norm, name="s5_out")
        else:
            gx = _matmul_plain(h, w_lru_in, j, tm=tm_res, name="lru_in")
            y = _lru_core(gx, lru_conv_w[j], lru_conv_b[j], lru_w_rg[j], lru_b_rg[j],
                          lru_w_ig[j], lru_b_ig[j], lru_lam[j], seq=seq, tm=tm_res)
            x2, h = _matmul_residual(y, w_lru_out, j, x2, g1, seq=seq, tm=tm_res,
                                     next_norm=ffn_norm, name="lru_out")
        act = _swiglu_up(h, w_gu, i, tm=tm_up, tn=_tile(hidden, 512), name="ffn_up")
        if i + 1 == depth:
            out = _matmul_residual(act, w_down, i, x2, g2, seq=seq, tm=tm_res, final_gain=final_g,
                                   name="ffn_down")
            return out.reshape(bsz, seq, d)
        nxt = None
        if (i + 1) % 2 == 1:
            nxt = (norm_g[i + 1, 0], mods[i + 1, :, 1], mods[i + 1, :, 0])
        x2, h = _matmul_residual(act, w_down, i, x2, g2, seq=seq, tm=tm_res, next_norm=nxt,
                                 name="ffn_down")
```

```python
import jax
import jax.numpy as jnp
from jax import lax
from jax.experimental import pallas as pl
from jax.experimental.pallas import tpu as pltpu

F32 = jnp.float32
BF16 = jnp.bfloat16

EPS = 1e-6
LRU_C = 8.0
LANES = 128
SUBLANES = 8
MXU_N = 256
S5_CHUNK = 8
S5_TILE_GROUPS = 8
VMEM_LIMIT = 56 * 1024 * 1024


def _params(semantics, vmem=VMEM_LIMIT):
    return pltpu.CompilerParams(dimension_semantics=semantics, vmem_limit_bytes=vmem)


def _dot(a, b):
    return jnp.dot(a, b, preferred_element_type=F32)


def _rms_mod(x, gain, scale, shift):
    ms = jnp.mean(x * x, axis=-1, keepdims=True)
    y = x * lax.rsqrt(ms + EPS) * gain
    return y * (1.0 + scale) + shift


def _cmul(ar, ai, br, bi):
    return ar * br - ai * bi, ar * bi + ai * br


def _mod_kernel(c_ref, w_ref, b_ref, o_ref):
    c = c_ref[...]
    cond = c * jax.nn.sigmoid(c)
    o_ref[...] = _dot(cond.astype(BF16), w_ref[...].astype(BF16)) + b_ref[...]


def _modulation(c, w_ada, b_ada):
    depth, d, n = w_ada.shape
    b = c.shape[0]
    rows = -(-b // SUBLANES) * SUBLANES
    cp = jnp.zeros((rows, d), F32).at[:b].set(c)
    tn = 1024 if n % 1024 == 0 else n
    out = pl.pallas_call(
        _mod_kernel,
        grid=(depth, n // tn),
        in_specs=[
            pl.BlockSpec((rows, d), lambda i, j: (0, 0)),
            pl.BlockSpec((None, d, tn), lambda i, j: (i, 0, j)),
            pl.BlockSpec((None, 1, tn), lambda i, j: (i, 0, j)),
        ],
        out_specs=pl.BlockSpec((None, rows, tn), lambda i, j: (i, 0, j)),
        out_shape=jax.ShapeDtypeStruct((depth, rows, n), F32),
        compiler_params=_params(("parallel", "parallel")),
        name="adaln_mod",
    )(cp, w_ada, b_ada.reshape(depth, 1, n))
    return out[:, :b]


def _resident(w_all, layer):
    _, k, n = w_all.shape
    return pl.BlockSpec((None, k, n), lambda m: (layer, 0, 0), pipeline_mode=pl.Buffered(1))


def _batch_vec(d, tm, seq):
    assert seq % tm == 0
    return pl.BlockSpec((1, 1, d), lambda m: ((m * tm) // seq, 0, 0))


def _ln_mm_kernel(x_ref, g_ref, sc_ref, sh_ref, w_ref, o_ref):
    h = _rms_mod(x_ref[...], g_ref[...], sc_ref[0], sh_ref[0]).astype(BF16)
    acc = _dot(h, w_ref[...])
    for q in range(o_ref.shape[0]):
        o_ref[q] = acc[:, q * LANES:(q + 1) * LANES]


def _ln_matmul_tiled(x2d, gain, scale, shift, w_all, layer, *, seq, tm, name):
    rows, d = x2d.shape
    n_out = w_all.shape[-1]
    assert rows % tm == 0 and n_out % LANES == 0
    vec = _batch_vec(d, tm, seq)
    return pl.pallas_call(
        _ln_mm_kernel,
        grid=(rows // tm,),
        in_specs=[
            pl.BlockSpec((tm, d), lambda m: (m, 0)),
            pl.BlockSpec((1, d), lambda m: (0, 0)),
            vec,
            vec,
            _resident(w_all, layer),
        ],
        out_specs=pl.BlockSpec((n_out // LANES, tm, LANES), lambda m: (0, m, 0)),
        out_shape=jax.ShapeDtypeStruct((n_out // LANES, rows, LANES), F32),
        compiler_params=_params(("parallel",)),
        name=name,
    )(x2d, gain.reshape(1, d), scale, shift, w_all)


def _make_lru_in_kernel(e, tm, tiles_per_seq):
    halo = SUBLANES

    def kern(a_ref, w_ref, cw_ref, cb_ref, o_ref, xpad):
        @pl.when(pl.program_id(0) % tiles_per_seq == 0)
        def _():
            xpad[0:halo, :] = jnp.zeros((halo, e), F32)

        acc = _dot(a_ref[...], w_ref[...])
        o_ref[:, 0:e] = jax.nn.gelu(acc[:, 0:e])
        xpad[halo:halo + tm, :] = acc[:, e:2 * e]
        nk = cw_ref.shape[0]
        xc = cb_ref[...] + cw_ref[nk - 1:nk, :] * xpad[halo:halo + tm, :]
        for k in range(nk - 1):
            off = halo - (nk - 1) + k
            xc = xc + cw_ref[k:k + 1, :] * xpad[off:off + tm, :]
        xpad[0:halo, :] = xpad[tm:tm + halo, :]
        o_ref[:, e:2 * e] = xc

    return kern


def _lru_in(a, w_all, layer, conv_w, conv_b, *, seq, tm, name):
    rows, k = a.shape
    e = w_all.shape[-1] // 2
    nk = conv_w.shape[0]
    assert rows % tm == 0 and seq % tm == 0 and nk - 1 <= SUBLANES
    return pl.pallas_call(
        _make_lru_in_kernel(e, tm, seq // tm),
        grid=(rows // tm,),
        in_specs=[
            pl.BlockSpec((tm, k), lambda m: (m, 0)),
            _resident(w_all, layer),
            pl.BlockSpec((nk, e), lambda m: (0, 0)),
            pl.BlockSpec((1, e), lambda m: (0, 0)),
        ],
        out_specs=pl.BlockSpec((tm, 2 * e), lambda m: (m, 0)),
        out_shape=jax.ShapeDtypeStruct((rows, 2 * e), F32),
        scratch_shapes=[pltpu.VMEM((tm + SUBLANES, e), F32)],
        compiler_params=_params(("arbitrary",)),
        name=name,
    )(a, w_all, conv_w.reshape(nk, e), conv_b.reshape(1, e))


def _swiglu_kernel(h_ref, wg_ref, wu_ref, o_ref, wg_s, wu_s):
    @pl.when(pl.program_id(1) == 0)
    def _():
        wg_s[...] = wg_ref[...].astype(BF16)
        wu_s[...] = wu_ref[...].astype(BF16)

    h = h_ref[...]
    g = _dot(h, wg_s[...])
    o_ref[...] = (g * jax.nn.sigmoid(g) * _dot(h, wu_s[...])).astype(BF16)


def _swiglu_up(h, w_all, layer, *, tm, tn, name):
    rows, d = h.shape
    n_out = w_all.shape[-1] // 2
    assert rows % tm == 0 and n_out % tn == 0
    nt = n_out // tn
    return pl.pallas_call(
        _swiglu_kernel,
        grid=(nt, rows // tm),
        in_specs=[
            pl.BlockSpec((tm, d), lambda n, m: (m, 0)),
            pl.BlockSpec((None, d, tn), lambda n, m: (layer, 0, n)),
            pl.BlockSpec((None, d, tn), lambda n, m: (layer, 0, n + nt)),
        ],
        out_specs=pl.BlockSpec((tm, tn), lambda n, m: (m, n)),
        out_shape=jax.ShapeDtypeStruct((rows, n_out), BF16),
        scratch_shapes=[pltpu.VMEM((d, tn), BF16), pltpu.VMEM((d, tn), BF16)],
        compiler_params=_params(("parallel", "arbitrary")),
        name=name,
    )(h, w_all, w_all)


def _make_mm_res_kernel(glu, tiled_a, n_out, mode):
    def kern(*refs):
        a_ref, w_ref, res_ref, gate_ref = refs[:4]
        if tiled_a:
            a = jnp.concatenate([a_ref[q] for q in range(a_ref.shape[0])], axis=1)
        else:
            a = a_ref[...]
        acc = _dot(a, w_ref[...])
        if glu:
            acc = acc[:, :n_out] * jax.nn.sigmoid(acc[:, n_out:])
        x = res_ref[...] + gate_ref[0] * acc
        if mode == "final":
            g_ref, o_ref = refs[4:]
            ms = jnp.mean(x * x, axis=-1, keepdims=True)
            o_ref[...] = x * lax.rsqrt(ms + EPS) * g_ref[...]
        elif mode == "x+h":
            g_ref, sc_ref, sh_ref, x_out, h_out = refs[4:]
            x_out[...] = x
            h_out[...] = _rms_mod(x, g_ref[...], sc_ref[0], sh_ref[0]).astype(BF16)
        else:
            (x_out,) = refs[4:]
            x_out[...] = x

    return kern


def _matmul_residual(a, w_all, layer, res2d, gate, *, seq, tm, glu=False, tiled_a=False,
                     next_norm=None, final_gain=None, name):
    rows, n_out = res2d.shape
    k = w_all.shape[1]
    assert rows % tm == 0
    if tiled_a:
        a_spec = pl.BlockSpec((k // LANES, tm, LANES), lambda m: (0, m, 0))
    else:
        a_spec = pl.BlockSpec((tm, k), lambda m: (m, 0))
    row_spec = pl.BlockSpec((tm, n_out), lambda m: (m, 0))
    gain_spec = pl.BlockSpec((1, n_out), lambda m: (0, 0))
    vec = _batch_vec(n_out, tm, seq)
    in_specs = [a_spec, _resident(w_all, layer), row_spec, vec]
    args = [a, w_all, res2d, gate]
    x_shape = jax.ShapeDtypeStruct((rows, n_out), F32)
    if final_gain is not None:
        mode = "final"
        in_specs.append(gain_spec)
        args.append(final_gain.reshape(1, n_out))
        out_specs, out_shape = row_spec, x_shape
    elif next_norm is not None:
        mode = "x+h"
        gain, scale, shift = next_norm
        in_specs += [gain_spec, vec, vec]
        args += [gain.reshape(1, n_out), scale, shift]
        out_specs = [row_spec, row_spec]
        out_shape = [x_shape, jax.ShapeDtypeStruct((rows, n_out), BF16)]
    else:
        mode = "x"
        out_specs, out_shape = [row_spec], [x_shape]
    out = pl.pallas_call(
        _make_mm_res_kernel(glu, tiled_a, n_out, mode),
        grid=(rows // tm,),
        in_specs=in_specs,
        out_specs=out_specs,
        out_shape=out_shape,
        compiler_params=_params(("parallel",)),
        name=name,
    )(*args)
    if mode == "final":
        return out
    return (out[0], out[1]) if mode == "x+h" else (out[0], None)


def _make_s5_prep_kernel(tg, p):
    tc = S5_CHUNK
    half = tg * p

    def kern(lr_ref, li_ref, ldt_ref, br_ref, bi_ref, cr_ref, ci_ref,
             wio_ref, wis_ref, wso_ref, tab_ref):
        lr = lr_ref[...]
        li = li_ref[...]
        dt = jnp.exp(ldt_ref[...])
        mag = jnp.exp(lr * dt)
        ab_re = mag * jnp.cos(li * dt)
        ab_im = mag * jnp.sin(li * dt)
        nr, ni = ab_re - 1.0, ab_im
        den = lr * lr + li * li
        f_re = (nr * lr + ni * li) / den
        f_im = (ni * lr - nr * li) / den
        br, bi = br_ref[...], bi_ref[...]
        bb_re = f_re * br - f_im * bi
        bb_im = f_re * bi + f_im * br
        cr, ci = cr_ref[...], ci_ref[...]
        gid = lax.broadcasted_iota(jnp.int32, (1, half), 1) // p

        def expand(v):
            return jnp.concatenate([jnp.where(gid == g, v, 0.0) for g in range(tg)], axis=0)

        pows = [(jnp.ones_like(ab_re), jnp.zeros_like(ab_im))]
        for _ in range(tc):
            pows.append(_cmul(pows[-1][0], pows[-1][1], ab_re, ab_im))

        for t in range(tc):
            pr, pi = pows[tc - 1 - t]
            wr, wi = _cmul(pr, pi, bb_re, bb_im)
            rows = slice(t * LANES, (t + 1) * LANES)
            wis_ref[rows, 0:half] = expand(wr).astype(BF16)
            wis_ref[rows, half:2 * half] = expand(wi).astype(BF16)

        def split(v):
            hi = v.astype(BF16)
            return hi, (v - hi.astype(F32)).astype(BF16)

        def nt_dot(a, b):
            return lax.dot_general(a, b, (((1,), (1,)), ((), ())), preferred_element_type=F32)

        b_hi, b_lo = split(jnp.concatenate([expand(bb_re), expand(bb_im)], axis=1))
        blocks = []
        for tau in range(tc + 1):
            pr, pi = pows[tau]
            qr, qi = _cmul(cr, ci, pr, pi)
            er, ei = expand(qr), expand(-qi)
            if tau >= 1:
                cols = slice((tau - 1) * LANES, tau * LANES)
                wso_ref[0:half, cols] = er.T.astype(BF16)
                wso_ref[half:2 * half, cols] = ei.T.astype(BF16)
            if tau < tc:
                q_hi, q_lo = split(jnp.concatenate([er, ei], axis=1))
                blk = nt_dot(b_hi, q_hi) + (nt_dot(b_hi, q_lo) + nt_dot(b_lo, q_hi))
                blocks.append(blk.astype(BF16))
        zero = jnp.zeros((LANES, LANES), BF16)
        for t in range(tc):
            for t2 in range(tc):
                wio_ref[t * LANES:(t + 1) * LANES, t2 * LANES:(t2 + 1) * LANES] = (
                    blocks[t2 - t] if t2 >= t else zero)

        lam = [(jnp.ones_like(ab_re), jnp.zeros_like(ab_im))]
        for _ in range(SUBLANES):
            lam.append(_cmul(lam[-1][0], lam[-1][1], pows[tc][0], pows[tc][1]))
        zrow = jnp.zeros_like(ab_re)
        groups = [[lam[r] for r in range(SUBLANES)]]
        for d in (1, 2, 4):
            groups.append([lam[d] if r >= d else (zrow, zrow) for r in range(SUBLANES)])
        groups.append([lam[SUBLANES]] * SUBLANES)
        for gi, grp in enumerate(groups):
            rows = slice(gi * SUBLANES, (gi + 1) * SUBLANES)
            tab_ref[rows, 0:half] = jnp.concatenate([v[0] for v in grp], axis=0)
            tab_ref[rows, half:2 * half] = jnp.concatenate([v[1] for v in grp], axis=0)

    return kern


def _s5_operators(lam_re, lam_im, log_dt, b_re, b_im, c_re, c_im):
    g, p = lam_re.shape
    gc = b_re.shape[-1]
    gp = g * p
    tg = S5_TILE_GROUPS
    assert gc * tg == LANES and g % tg == 0
    nj = g // tg
    half = tg * p
    kdim = S5_CHUNK * LANES
    lr = lam_re.reshape(1, gp)
    li = lam_im.reshape(1, gp)
    ldt = jnp.repeat(log_dt, p).reshape(1, gp)
    brt = b_re.transpose(2, 0, 1).reshape(gc, gp)
    bit = b_im.transpose(2, 0, 1).reshape(gc, gp)
    crt = c_re.transpose(1, 0, 2).reshape(gc, gp)
    cit = c_im.transpose(1, 0, 2).reshape(gc, gp)
    row = pl.BlockSpec((1, half), lambda j: (0, j))
    mat = pl.BlockSpec((gc, half), lambda j: (0, j))
    return pl.pallas_call(
        _make_s5_prep_kernel(tg, p),
        grid=(nj,),
        in_specs=[row, row, row, mat, mat, mat, mat],
        out_specs=[
            pl.BlockSpec((None, kdim, kdim), lambda j: (j, 0, 0)),
            pl.BlockSpec((None, kdim, 2 * half), lambda j: (j, 0, 0)),
            pl.BlockSpec((None, 2 * half, kdim), lambda j: (j, 0, 0)),
            pl.BlockSpec((None, 5 * SUBLANES, 2 * half), lambda j: (j, 0, 0)),
        ],
        out_shape=[
            jax.ShapeDtypeStruct((nj, kdim, kdim), BF16),
            jax.ShapeDtypeStruct((nj, kdim, 2 * half), BF16),
            jax.ShapeDtypeStruct((nj, 2 * half, kdim), BF16),
            jax.ShapeDtypeStruct((nj, 5 * SUBLANES, 2 * half), F32),
        ],
        compiler_params=_params(("parallel",)),
        name="s5_prep",
    )(lr, li, ldt, brt, bit, crt, cit)


def _make_s5_kernel(nb, chunks_per_batch, half):
    tc = S5_CHUNK
    nchunk = nb * chunks_per_batch
    tiles_per_batch = chunks_per_batch // SUBLANES
    ntile = nchunk // SUBLANES

    def kern(u_ref, wio_ref, wis_ref, wso_ref, tab_ref, d_ref, y_ref, s_scr, y_scr):
        lhs = jnp.concatenate(
            [u_ref[pl.ds(t, nchunk, stride=tc), :].astype(BF16) for t in range(tc)], axis=1)
        s_scr[...] = _dot(lhs, wis_ref[...])

        tab = tab_ref[...]
        pw_r, pw_i = tab[0:8, :half], tab[0:8, half:]
        steps = [(d, tab[8 * i:8 * i + 8, :half], tab[8 * i:8 * i + 8, half:])
                 for i, d in ((1, 1), (2, 2), (3, 4))]
        l8_r, l8_i = tab[32:33, :half], tab[32:33, half:]
        first_row = lax.broadcasted_iota(jnp.int32, (SUBLANES, half), 0) == 0

        for m in range(ntile):
            rows = slice(m * SUBLANES, (m + 1) * SUBLANES)
            s = s_scr[rows, :]
            er, ei = s[:, :half], s[:, half:]
            for d, mr, mi in steps:
                rr = pltpu.roll(er, d, axis=0)
                ri = pltpu.roll(ei, d, axis=0)
                er, ei = er + (mr * rr - mi * ri), ei + (mr * ri + mi * rr)
            hr = jnp.where(first_row, 0.0, pltpu.roll(er, 1, axis=0))
            hi = jnp.where(first_row, 0.0, pltpu.roll(ei, 1, axis=0))
            if m % tiles_per_batch == 0:
                cr, ci = er[7:8], ei[7:8]
            else:
                hr = hr + (pw_r * cr - pw_i * ci)
                hi = hi + (pw_r * ci + pw_i * cr)
                cr, ci = er[7:8] + (l8_r * cr - l8_i * ci), ei[7:8] + (l8_r * ci + l8_i * cr)
            s_scr[rows, :] = jnp.concatenate([hr, hi], axis=1)

        y_io = jnp.concatenate(
            [_dot(lhs[:, :(p + 1) * MXU_N], wio_ref[0:(p + 1) * MXU_N, p * MXU_N:(p + 1) * MXU_N])
             for p in range(tc * LANES // MXU_N)], axis=1)
        y = y_io + _dot(s_scr[...].astype(BF16), wso_ref[...])
        for t in range(tc):
            y_scr[pl.ds(t, nchunk, stride=tc), :] = y[:, t * LANES:(t + 1) * LANES]
        y_ref[...] = jax.nn.gelu(y_scr[...] + d_ref[...] * u_ref[...]).astype(BF16)

    return kern


def _s5_core(u3, ops, d_skip, *, nb, seq):
    w_io, w_is, w_so, tab = ops
    nj, rows, _ = u3.shape
    kdim = S5_CHUNK * LANES
    sdim = w_is.shape[-1]
    chunks_per_batch = seq // S5_CHUNK
    assert rows == nb * seq and chunks_per_batch % SUBLANES == 0
    tok = pl.BlockSpec((None, rows, LANES), lambda j: (j, 0, 0))
    return pl.pallas_call(
        _make_s5_kernel(nb, chunks_per_batch, sdim // 2),
        grid=(nj,),
        in_specs=[
            tok,
            pl.BlockSpec((None, kdim, kdim), lambda j: (j, 0, 0)),
            pl.BlockSpec((None, kdim, sdim), lambda j: (j, 0, 0)),
            pl.BlockSpec((None, sdim, kdim), lambda j: (j, 0, 0)),
            pl.BlockSpec((None, 5 * SUBLANES, sdim), lambda j: (j, 0, 0)),
            pl.BlockSpec((1, LANES), lambda j: (0, j)),
        ],
        out_specs=tok,
        out_shape=jax.ShapeDtypeStruct((nj, rows, LANES), BF16),
        scratch_shapes=[pltpu.VMEM((rows // S5_CHUNK, sdim), F32), pltpu.VMEM((rows, LANES), F32)],
        compiler_params=_params(("parallel",)),
        name="s5_core",
    )(u3, w_io, w_is, w_so, tab, d_skip.reshape(1, nj * LANES))


def _band_windows(width, blk, tn):
    starts, ends = [], []
    for n in range(width // tn):
        h_lo = (n * tn) // blk
        h_hi = (n * tn + tn - 1) // blk
        starts.append((h_lo * blk) // LANES)
        ends.append(-(-((h_hi + 1) * blk) // LANES))
    kw = max(e - s for s, e in zip(starts, ends))
    total = width // LANES
    starts = [min(s, total - kw) for s in starts]
    return starts, kw * LANES


def _banded(w, starts, kw, tn):
    nblk, blk, _ = w.shape
    tiles = []
    for n, s in enumerate(starts):
        pieces = []
        for h in range((n * tn) // blk, (n * tn + tn - 1) // blk + 1):
            j0 = max(0, n * tn - h * blk)
            j1 = min(blk, (n + 1) * tn - h * blk)
            r0 = h * blk - s * LANES
            pieces.append(jnp.pad(w[h][:, j0:j1], ((r0, kw - r0 - blk), (0, 0))))
        tiles.append(jnp.concatenate(pieces, axis=1))
    return jnp.stack(tiles).astype(BF16)


def _make_lru_kernel(e, tm, tn, starts, kw, tiles_per_seq, col_groups):
    ntile = e // tn
    nrow = tm // SUBLANES

    def kern(gx_ref, wrg_ref, wig_ref, brg_ref, big_ref, lam_ref, y_ref, a_scr, b_scr, hcar):
        @pl.when(pl.program_id(0) % tiles_per_seq == 0)
        def _():
            hcar[...] = jnp.zeros((1, e), F32)

        lam = lam_ref[...]
        neg = -lam
        softplus = jnp.maximum(neg, 0.0) + jnp.log1p(jnp.exp(-jnp.abs(neg)))

        for n in range(ntile):
            cols = slice(n * tn, (n + 1) * tn)
            k0 = e + starts[n] * LANES
            win = gx_ref[:, k0:k0 + kw].astype(BF16)
            r = jax.nn.sigmoid(_dot(win, wrg_ref[n]) + brg_ref[:, cols])
            ig = jax.nn.sigmoid(_dot(win, wig_ref[n]) + big_ref[:, cols])
            a = jnp.exp((-LRU_C) * r * softplus[:, cols])
            om = 1.0 - a * a
            mult = jnp.where(om > 0.0, om * lax.rsqrt(om), 0.0)
            a_scr[:, cols] = a
            b_scr[:, cols] = mult * (ig * gx_ref[:, e + n * tn:e + (n + 1) * tn])

        for c0, c1 in col_groups:
            row = lax.broadcasted_iota(jnp.int32, (SUBLANES, c1 - c0), 0)
            masks = [row >= d for d in (1, 2, 4)]

            def tile_scan(m, c, c0=c0, c1=c1, masks=masks):
                rows = pl.ds(pl.multiple_of(m * SUBLANES, SUBLANES), SUBLANES)
                av = a_scr[rows, c0:c1]
                bv = b_scr[rows, c0:c1]
                for d, keep in zip((1, 2, 4), masks):
                    bv = bv + jnp.where(keep, av * pltpu.roll(bv, d, axis=0), 0.0)
                    av = jnp.where(keep, av * pltpu.roll(av, d, axis=0), av)
                h = bv + av * c
                b_scr[rows, c0:c1] = h
                return h[7:8]

            hcar[:, c0:c1] = lax.fori_loop(0, nrow, tile_scan, hcar[:, c0:c1], unroll=2)

        y_ref[...] = (b_scr[...] * gx_ref[:, 0:e]).astype(BF16)

    return kern


def _lru_core(gx, w_rg, b_rg, w_ig, b_ig, lam, *, seq, tm):
    rows, e2 = gx.shape
    e = e2 // 2
    blk = w_rg.shape[1]
    tn = MXU_N
    assert e % tn == 0 and seq % tm == 0 and tm % SUBLANES == 0
    starts, kw = _band_windows(e, blk, tn)
    wrg, wig = _banded(w_rg, starts, kw, tn), _banded(w_ig, starts, kw, tn)
    ntile = e // tn
    group = 8 * LANES
    col_groups = [(c, min(c + group, e)) for c in range(0, e, group)]
    full = lambda shape: pl.BlockSpec(shape, lambda m: (0,) * len(shape))
    return pl.pallas_call(
        _make_lru_kernel(e, tm, tn, starts, kw, seq // tm, col_groups),
        grid=(rows // tm,),
        in_specs=[
            pl.BlockSpec((tm, e2), lambda m: (m, 0)),
            full((ntile, kw, tn)),
            full((ntile, kw, tn)),
            full((1, e)),
            full((1, e)),
            full((1, e)),
        ],
        out_specs=pl.BlockSpec((tm, e), lambda m: (m, 0)),
        out_shape=jax.ShapeDtypeStruct((rows, e), BF16),
        scratch_shapes=[
            pltpu.VMEM((tm, e), F32),
            pltpu.VMEM((tm, e), F32),
            pltpu.VMEM((1, e), F32),
        ],
        compiler_params=_params(("arbitrary",)),
        name="lru_core",
    )(gx, wrg, wig, b_rg.reshape(1, e), b_ig.reshape(1, e), lam.reshape(1, e))


def _tile(n, pref):
    t = min(n, pref)
    while n % t:
        t //= 2
    return t


def kernel(x, c, norm_g, w_ada, b_ada, s5_w_in, s5_lam_re, s5_lam_im, s5_log_dt, s5_b_re, s5_b_im, s5_c_re, s5_c_im, s5_d, s5_w_glu, lru_w_in, lru_conv_w, lru_conv_b, lru_w_rg, lru_b_rg, lru_w_ig, lru_b_ig, lru_lam, lru_w_out, ffn_w_gu, ffn_w_down, final_g):
    bsz, seq, d = x.shape
    depth = w_ada.shape[0]
    rows = bsz * seq
    x2 = x.reshape(rows, d)
    mods = _modulation(c, w_ada, b_ada).reshape(depth, bsz, 6, 1, d)
    tm_in = _tile(seq, 512)
    tm_res = _tile(seq, 256)
    tm_up = _tile(rows, 1024)
    hidden = ffn_w_down.shape[1]
    w_s5_in, w_s5_glu = s5_w_in.astype(BF16), s5_w_glu.astype(BF16)
    w_lru_in, w_lru_out = lru_w_in.astype(BF16), lru_w_out.astype(BF16)
    w_down = ffn_w_down.astype(BF16)

    h = None
    for i in range(depth):
        sh1, sc1, g1, sh2, sc2, g2 = [mods[i, :, q] for q in range(6)]
        ffn_norm = (norm_g[i, 1], sc2, sh2)
        j = i // 2
        if i % 2 == 0:
            u3 = _ln_matmul_tiled(x2, norm_g[i, 0], sc1, sh1, w_s5_in, j, seq=seq, tm=tm_in,
                                  name="s5_in")
            ops = _s5_operators(s5_lam_re[j], s5_lam_im[j], s5_log_dt[j], s5_b_re[j], s5_b_im[j],
                                s5_c_re[j], s5_c_im[j])
            y3 = _s5_core(u3, ops, s5_d[j], nb=bsz, seq=seq)
            x2, h = _matmul_residual(y3, w_s5_glu, j, x2, g1, seq=seq, tm=tm_res, glu=True,
                                     tiled_a=True, next_norm=ffn_norm, name="s5_out")
        else:
            gx = _lru_in(h, w_lru_in, j, lru_conv_w[j], lru_conv_b[j], seq=seq, tm=tm_res,
                         name="lru_in")
            y = _lru_core(gx, lru_w_rg[j], lru_b_rg[j], lru_w_ig[j], lru_b_ig[j], lru_lam[j],
                          seq=seq, tm=tm_res)
            x2, h = _matmul_residual(y, w_lru_out, j, x2, g1, seq=seq, tm=tm_res,
                                     next_norm=ffn_norm, name="lru_out")
        act = _swiglu_up(h, ffn_w_gu, i, tm=tm_up, tn=_tile(hidden, 512), name="ffn_up")
        if i + 1 == depth:
            out = _matmul_residual(act, w_down, i, x2, g2, seq=seq, tm=tm_res, final_gain=final_g,
                                   name="ffn_down")
            return out.reshape(bsz, seq, d)
        nxt = None
        if (i + 1) % 2 == 1:
            nxt = (norm_g[i + 1, 0], mods[i + 1, :, 1], mods[i + 1, :, 0])
        x2, h = _matmul_residual(act, w_down, i, x2, g2, seq=seq, tm=tm_res, next_norm=nxt,
                                 name="ffn_down")
```

```python
import jax
import jax.numpy as jnp
from jax import lax
from jax.experimental import pallas as pl
from jax.experimental.pallas import tpu as pltpu

F32 = jnp.float32
BF16 = jnp.bfloat16

EPS = 1e-6
LRU_C = 8.0
LANES = 128
SUBLANES = 8
MXU_N = 256
S5_CHUNK = 8
S5_TILE_GROUPS = 8
VMEM_LIMIT = 56 * 1024 * 1024
W_CHUNK_BYTES = 4 * 1024 * 1024


def _params(semantics, vmem=VMEM_LIMIT):
    return pltpu.CompilerParams(dimension_semantics=semantics, vmem_limit_bytes=vmem)


def _dot(a, b):
    return jnp.dot(a, b, preferred_element_type=F32)


def _rms_mod(x, gain, scale, shift):
    ms = jnp.mean(x * x, axis=-1, keepdims=True)
    y = x * lax.rsqrt(ms + EPS) * gain
    return y * (1.0 + scale) + shift


def _cmul(ar, ai, br, bi):
    return ar * br - ai * bi, ar * bi + ai * br


def _mod_kernel(c_ref, w_ref, b_ref, o_ref):
    c = c_ref[...]
    cond = c * jax.nn.sigmoid(c)
    o_ref[...] = _dot(cond.astype(BF16), w_ref[...].astype(BF16)) + b_ref[...]


def _modulation(c, w_ada, b_ada):
    depth, d, n = w_ada.shape
    b = c.shape[0]
    rows = -(-b // SUBLANES) * SUBLANES
    cp = jnp.zeros((rows, d), F32).at[:b].set(c)
    tn = 1024 if n % 1024 == 0 else n
    out = pl.pallas_call(
        _mod_kernel,
        grid=(depth, n // tn),
        in_specs=[
            pl.BlockSpec((rows, d), lambda i, j: (0, 0)),
            pl.BlockSpec((None, d, tn), lambda i, j: (i, 0, j)),
            pl.BlockSpec((None, 1, tn), lambda i, j: (i, 0, j)),
        ],
        out_specs=pl.BlockSpec((None, rows, tn), lambda i, j: (i, 0, j)),
        out_shape=jax.ShapeDtypeStruct((depth, rows, n), F32),
        compiler_params=_params(("parallel", "parallel")),
        name="adaln_mod",
    )(cp, w_ada, b_ada.reshape(depth, 1, n))
    return out[:, :b]


class _Resident:
    def __init__(self, w_all, layer, rows, tm, seq):
        _, self.k, self.n = w_all.shape
        assert rows % tm == 0 and seq % tm == 0
        ck = self.k
        while ck * self.n * 4 > W_CHUNK_BYTES and ck % 32 == 0:
            ck //= 2
        self.ck, self.nc = ck, self.k // ck
        self.layer, self.tm, self.seq = layer, tm, seq
        self.grid = (self.nc + rows // tm,)
        self.scratch = pltpu.VMEM((self.k, self.n), BF16)

    def tile(self, m):
        return jnp.maximum(m - self.nc, 0)

    def weight(self):
        return pl.BlockSpec((None, self.ck, self.n),
                            lambda m: (self.layer, jnp.minimum(m, self.nc - 1), 0))

    def rows(self, width):
        return pl.BlockSpec((self.tm, width), lambda m: (self.tile(m), 0))

    def whole(self, shape):
        return pl.BlockSpec(shape, lambda m: (0,) * len(shape))

    def batch_vec(self, d):
        return pl.BlockSpec((1, 1, d), lambda m: ((self.tile(m) * self.tm) // self.seq, 0, 0))

    def kernel(self, body, w_pos):
        nc, ck = self.nc, self.ck

        def kern(*refs):
            w_s = refs[-1]
            refs = list(refs[:-1])
            m = pl.program_id(0)

            @pl.when(m < nc)
            def _():
                w_s[pl.ds(pl.multiple_of(m * ck, ck), ck), :] = refs[w_pos][...].astype(BF16)

            @pl.when(m >= nc)
            def _():
                body(*refs[:w_pos], w_s, *refs[w_pos + 1:])

        return kern


def _ln_mm_kernel(x_ref, g_ref, sc_ref, sh_ref, w_ref, o_ref):
    h = _rms_mod(x_ref[...], g_ref[...], sc_ref[0], sh_ref[0]).astype(BF16)
    acc = _dot(h, w_ref[...])
    for q in range(o_ref.shape[0]):
        o_ref[q] = acc[:, q * LANES:(q + 1) * LANES]


def _ln_matmul_tiled(x2d, gain, scale, shift, w_all, layer, *, seq, tm, name):
    rows, d = x2d.shape
    n_out = w_all.shape[-1]
    assert n_out % LANES == 0
    plan = _Resident(w_all, layer, rows, tm, seq)
    vec = plan.batch_vec(d)
    return pl.pallas_call(
        plan.kernel(_ln_mm_kernel, 4),
        grid=plan.grid,
        in_specs=[plan.rows(d), plan.whole((1, d)), vec, vec, plan.weight()],
        out_specs=pl.BlockSpec((n_out // LANES, tm, LANES), lambda m: (0, plan.tile(m), 0)),
        out_shape=jax.ShapeDtypeStruct((n_out // LANES, rows, LANES), F32),
        scratch_shapes=[plan.scratch],
        compiler_params=_params(("arbitrary",)),
        name=name,
    )(x2d, gain.reshape(1, d), scale, shift, w_all)


def _make_lru_in_kernel(e, tm, tiles_per_seq, first_step):
    halo = SUBLANES

    def kern(a_ref, w_ref, cw_ref, cb_ref, o_ref, xpad):
        @pl.when((pl.program_id(0) - first_step) % tiles_per_seq == 0)
        def _():
            xpad[0:halo, :] = jnp.zeros((halo, e), F32)

        acc = _dot(a_ref[...], w_ref[...])
        o_ref[:, 0:e] = jax.nn.gelu(acc[:, 0:e])
        xpad[halo:halo + tm, :] = acc[:, e:2 * e]
        nk = cw_ref.shape[0]
        xc = cb_ref[...] + cw_ref[nk - 1:nk, :] * xpad[halo:halo + tm, :]
        for k in range(nk - 1):
            off = halo - (nk - 1) + k
            xc = xc + cw_ref[k:k + 1, :] * xpad[off:off + tm, :]
        xpad[0:halo, :] = xpad[tm:tm + halo, :]
        o_ref[:, e:2 * e] = xc

    return kern


def _lru_in(a, w_all, layer, conv_w, conv_b, *, seq, tm, name):
    rows, k = a.shape
    e = w_all.shape[-1] // 2
    nk = conv_w.shape[0]
    assert nk - 1 <= SUBLANES
    plan = _Resident(w_all, layer, rows, tm, seq)
    return pl.pallas_call(
        plan.kernel(_make_lru_in_kernel(e, tm, seq // tm, plan.nc), 1),
        grid=plan.grid,
        in_specs=[plan.rows(k), plan.weight(), plan.whole((nk, e)), plan.whole((1, e))],
        out_specs=plan.rows(2 * e),
        out_shape=jax.ShapeDtypeStruct((rows, 2 * e), F32),
        scratch_shapes=[pltpu.VMEM((tm + SUBLANES, e), F32), plan.scratch],
        compiler_params=_params(("arbitrary",)),
        name=name,
    )(a, w_all, conv_w.reshape(nk, e), conv_b.reshape(1, e))


def _swiglu_kernel(h_ref, wg_ref, wu_ref, o_ref, wg_s, wu_s):
    @pl.when(pl.program_id(1) == 0)
    def _():
        wg_s[...] = wg_ref[...].astype(BF16)
        wu_s[...] = wu_ref[...].astype(BF16)

    h = h_ref[...]
    g = _dot(h, wg_s[...])
    o_ref[...] = (g * jax.nn.sigmoid(g) * _dot(h, wu_s[...])).astype(BF16)


def _swiglu_up(h, w_all, layer, *, tm, tn, name):
    rows, d = h.shape
    n_out = w_all.shape[-1] // 2
    assert rows % tm == 0 and n_out % tn == 0
    nt = n_out // tn
    return pl.pallas_call(
        _swiglu_kernel,
        grid=(nt, rows // tm),
        in_specs=[
            pl.BlockSpec((tm, d), lambda n, m: (m, 0)),
            pl.BlockSpec((None, d, tn), lambda n, m: (layer, 0, n)),
            pl.BlockSpec((None, d, tn), lambda n, m: (layer, 0, n + nt)),
        ],
        out_specs=pl.BlockSpec((tm, tn), lambda n, m: (m, n)),
        out_shape=jax.ShapeDtypeStruct((rows, n_out), BF16),
        scratch_shapes=[pltpu.VMEM((d, tn), BF16), pltpu.VMEM((d, tn), BF16)],
        compiler_params=_params(("parallel", "arbitrary")),
        name=name,
    )(h, w_all, w_all)


def _make_mm_res_kernel(glu, tiled_a, n_out, mode):
    def kern(*refs):
        a_ref, w_ref, res_ref, gate_ref = refs[:4]
        if tiled_a:
            a = jnp.concatenate([a_ref[q] for q in range(a_ref.shape[0])], axis=1)
        else:
            a = a_ref[...]
        acc = _dot(a, w_ref[...])
        if glu:
            acc = acc[:, :n_out] * jax.nn.sigmoid(acc[:, n_out:])
        x = res_ref[...] + gate_ref[0] * acc
        if mode == "final":
            g_ref, o_ref = refs[4:]
            ms = jnp.mean(x * x, axis=-1, keepdims=True)
            o_ref[...] = x * lax.rsqrt(ms + EPS) * g_ref[...]
        elif mode == "x+h":
            g_ref, sc_ref, sh_ref, x_out, h_out = refs[4:]
            x_out[...] = x
            h_out[...] = _rms_mod(x, g_ref[...], sc_ref[0], sh_ref[0]).astype(BF16)
        else:
            (x_out,) = refs[4:]
            x_out[...] = x

    return kern


def _matmul_residual(a, w_all, layer, res2d, gate, *, seq, tm, glu=False, tiled_a=False,
                     next_norm=None, final_gain=None, name):
    rows, n_out = res2d.shape
    k = w_all.shape[1]
    plan = _Resident(w_all, layer, rows, tm, seq)
    if tiled_a:
        a_spec = pl.BlockSpec((k // LANES, tm, LANES), lambda m: (0, plan.tile(m), 0))
    else:
        a_spec = plan.rows(k)
    row_spec = plan.rows(n_out)
    gain_spec = plan.whole((1, n_out))
    vec = plan.batch_vec(n_out)
    in_specs = [a_spec, plan.weight(), row_spec, vec]
    args = [a, w_all, res2d, gate]
    x_shape = jax.ShapeDtypeStruct((rows, n_out), F32)
    if final_gain is not None:
        mode = "final"
        in_specs.append(gain_spec)
        args.append(final_gain.reshape(1, n_out))
        out_specs, out_shape = row_spec, x_shape
    elif next_norm is not None:
        mode = "x+h"
        gain, scale, shift = next_norm
        in_specs += [gain_spec, vec, vec]
        args += [gain.reshape(1, n_out), scale, shift]
        out_specs = [row_spec, row_spec]
        out_shape = [x_shape, jax.ShapeDtypeStruct((rows, n_out), BF16)]
    else:
        mode = "x"
        out_specs, out_shape = [row_spec], [x_shape]
    out = pl.pallas_call(
        plan.kernel(_make_mm_res_kernel(glu, tiled_a, n_out, mode), 1),
        grid=plan.grid,
        in_specs=in_specs,
        out_specs=out_specs,
        out_shape=out_shape,
        scratch_shapes=[plan.scratch],
        compiler_params=_params(("arbitrary",)),
        name=name,
    )(*args)
    if mode == "final":
        return out
    return (out[0], out[1]) if mode == "x+h" else (out[0], None)


def _make_s5_prep_kernel(tg, p):
    tc = S5_CHUNK
    half = tg * p

    def kern(lr_ref, li_ref, ldt_ref, br_ref, bi_ref, cr_ref, ci_ref,
             wio_ref, wis_ref, wso_ref, tab_ref):
        lr = lr_ref[...]
        li = li_ref[...]
        dt = jnp.exp(ldt_ref[...])
        mag = jnp.exp(lr * dt)
        ab_re = mag * jnp.cos(li * dt)
        ab_im = mag * jnp.sin(li * dt)
        nr, ni = ab_re - 1.0, ab_im
        den = lr * lr + li * li
        f_re = (nr * lr + ni * li) / den
        f_im = (ni * lr - nr * li) / den
        br, bi = br_ref[...], bi_ref[...]
        bb_re = f_re * br - f_im * bi
        bb_im = f_re * bi + f_im * br
        cr, ci = cr_ref[...], ci_ref[...]
        gid = lax.broadcasted_iota(jnp.int32, (1, half), 1) // p

        def expand(v):
            return jnp.concatenate([jnp.where(gid == g, v, 0.0) for g in range(tg)], axis=0)

        pows = [(jnp.ones_like(ab_re), jnp.zeros_like(ab_im))]
        for _ in range(tc):
            pows.append(_cmul(pows[-1][0], pows[-1][1], ab_re, ab_im))

        for t in range(tc):
            pr, pi = pows[tc - 1 - t]
            wr, wi = _cmul(pr, pi, bb_re, bb_im)
            rows = slice(t * LANES, (t + 1) * LANES)
            wis_ref[rows, 0:half] = expand(wr).astype(BF16)
            wis_ref[rows, half:2 * half] = expand(wi).astype(BF16)

        def split(v):
            hi = v.astype(BF16)
            return hi, (v - hi.astype(F32)).astype(BF16)

        def nt_dot(a, b):
            return lax.dot_general(a, b, (((1,), (1,)), ((), ())), preferred_element_type=F32)

        b_hi, b_lo = split(jnp.concatenate([expand(bb_re), expand(bb_im)], axis=1))
        blocks = []
        for tau in range(tc + 1):
            pr, pi = pows[tau]
            qr, qi = _cmul(cr, ci, pr, pi)
            er, ei = expand(qr), expand(-qi)
            if tau >= 1:
                cols = slice((tau - 1) * LANES, tau * LANES)
                wso_ref[0:half, cols] = er.T.astype(BF16)
                wso_ref[half:2 * half, cols] = ei.T.astype(BF16)
            if tau < tc:
                q_hi, q_lo = split(jnp.concatenate([er, ei], axis=1))
                blk = nt_dot(b_hi, q_hi) + (nt_dot(b_hi, q_lo) + nt_dot(b_lo, q_hi))
                blocks.append(blk.astype(BF16))
        zero = jnp.zeros((LANES, LANES), BF16)
        for t in range(tc):
            for t2 in range(tc):
                wio_ref[t * LANES:(t + 1) * LANES, t2 * LANES:(t2 + 1) * LANES] = (
                    blocks[t2 - t] if t2 >= t else zero)

        lam = [(jnp.ones_like(ab_re), jnp.zeros_like(ab_im))]
        for _ in range(SUBLANES):
            lam.append(_cmul(lam[-1][0], lam[-1][1], pows[tc][0], pows[tc][1]))
        zrow = jnp.zeros_like(ab_re)
        groups = [[lam[r] for r in range(SUBLANES)]]
        for d in (1, 2, 4):
            groups.append([lam[d] if r >= d else (zrow, zrow) for r in range(SUBLANES)])
        groups.append([lam[SUBLANES]] * SUBLANES)
        for gi, grp in enumerate(groups):
            rows = slice(gi * SUBLANES, (gi + 1) * SUBLANES)
            tab_ref[rows, 0:half] = jnp.concatenate([v[0] for v in grp], axis=0)
            tab_ref[rows, half:2 * half] = jnp.concatenate([v[1] for v in grp], axis=0)

    return kern


def _s5_operators(lam_re, lam_im, log_dt, b_re, b_im, c_re, c_im):
    g, p = lam_re.shape
    gc = b_re.shape[-1]
    gp = g * p
    tg = S5_TILE_GROUPS
    assert gc * tg == LANES and g % tg == 0
    nj = g // tg
    half = tg * p
    kdim = S5_CHUNK * LANES
    lr = lam_re.reshape(1, gp)
    li = lam_im.reshape(1, gp)
    ldt = jnp.repeat(log_dt, p).reshape(1, gp)
    brt = b_re.transpose(2, 0, 1).reshape(gc, gp)
    bit = b_im.transpose(2, 0, 1).reshape(gc, gp)
    crt = c_re.transpose(1, 0, 2).reshape(gc, gp)
    cit = c_im.transpose(1, 0, 2).reshape(gc, gp)
    row = pl.BlockSpec((1, half), lambda j: (0, j))
    mat = pl.BlockSpec((gc, half), lambda j: (0, j))
    return pl.pallas_call(
        _make_s5_prep_kernel(tg, p),
        grid=(nj,),
        in_specs=[row, row, row, mat, mat, mat, mat],
        out_specs=[
            pl.BlockSpec((None, kdim, kdim), lambda j: (j, 0, 0)),
            pl.BlockSpec((None, kdim, 2 * half), lambda j: (j, 0, 0)),
            pl.BlockSpec((None, 2 * half, kdim), lambda j: (j, 0, 0)),
            pl.BlockSpec((None, 5 * SUBLANES, 2 * half), lambda j: (j, 0, 0)),
        ],
        out_shape=[
            jax.ShapeDtypeStruct((nj, kdim, kdim), BF16),
            jax.ShapeDtypeStruct((nj, kdim, 2 * half), BF16),
            jax.ShapeDtypeStruct((nj, 2 * half, kdim), BF16),
            jax.ShapeDtypeStruct((nj, 5 * SUBLANES, 2 * half), F32),
        ],
        compiler_params=_params(("parallel",)),
        name="s5_prep",
    )(lr, li, ldt, brt, bit, crt, cit)


def _make_s5_kernel(nb, chunks_per_batch, half):
    tc = S5_CHUNK
    nchunk = nb * chunks_per_batch
    tiles_per_batch = chunks_per_batch // SUBLANES
    ntile = nchunk // SUBLANES

    def kern(u_ref, wio_ref, wis_ref, wso_ref, tab_ref, d_ref, y_ref, s_scr, y_scr):
        lhs = jnp.concatenate(
            [u_ref[pl.ds(t, nchunk, stride=tc), :].astype(BF16) for t in range(tc)], axis=1)
        s_scr[...] = _dot(lhs, wis_ref[...])

        tab = tab_ref[...]
        pw_r, pw_i = tab[0:8, :half], tab[0:8, half:]
        steps = [(d, tab[8 * i:8 * i + 8, :half], tab[8 * i:8 * i + 8, half:])
                 for i, d in ((1, 1), (2, 2), (3, 4))]
        l8_r, l8_i = tab[32:33, :half], tab[32:33, half:]
        first_row = lax.broadcasted_iota(jnp.int32, (SUBLANES, half), 0) == 0

        for m in range(ntile):
            rows = slice(m * SUBLANES, (m + 1) * SUBLANES)
            s = s_scr[rows, :]
            er, ei = s[:, :half], s[:, half:]
            for d, mr, mi in steps:
                rr = pltpu.roll(er, d, axis=0)
                ri = pltpu.roll(ei, d, axis=0)
                er, ei = er + (mr * rr - mi * ri), ei + (mr * ri + mi * rr)
            hr = jnp.where(first_row, 0.0, pltpu.roll(er, 1, axis=0))
            hi = jnp.where(first_row, 0.0, pltpu.roll(ei, 1, axis=0))
            if m % tiles_per_batch == 0:
                cr, ci = er[7:8], ei[7:8]
            else:
                hr = hr + (pw_r * cr - pw_i * ci)
                hi = hi + (pw_r * ci + pw_i * cr)
                cr, ci = er[7:8] + (l8_r * cr - l8_i * ci), ei[7:8] + (l8_r * ci + l8_i * cr)
            s_scr[rows, :] = jnp.concatenate([hr, hi], axis=1)

        y_io = jnp.concatenate(
            [_dot(lhs[:, :(p + 1) * MXU_N], wio_ref[0:(p + 1) * MXU_N, p * MXU_N:(p + 1) * MXU_N])
             for p in range(tc * LANES // MXU_N)], axis=1)
        y = y_io + _dot(s_scr[...].astype(BF16), wso_ref[...])
        for t in range(tc):
            y_scr[pl.ds(t, nchunk, stride=tc), :] = y[:, t * LANES:(t + 1) * LANES]
        y_ref[...] = jax.nn.gelu(y_scr[...] + d_ref[...] * u_ref[...]).astype(BF16)

    return kern


def _s5_core(u3, ops, d_skip, *, nb, seq):
    w_io, w_is, w_so, tab = ops
    nj, rows, _ = u3.shape
    kdim = S5_CHUNK * LANES
    sdim = w_is.shape[-1]
    chunks_per_batch = seq // S5_CHUNK
    assert rows == nb * seq and chunks_per_batch % SUBLANES == 0
    tok = pl.BlockSpec((None, rows, LANES), lambda j: (j, 0, 0))
    return pl.pallas_call(
        _make_s5_kernel(nb, chunks_per_batch, sdim // 2),
        grid=(nj,),
        in_specs=[
            tok,
            pl.BlockSpec((None, kdim, kdim), lambda j: (j, 0, 0)),
            pl.BlockSpec((None, kdim, sdim), lambda j: (j, 0, 0)),
            pl.BlockSpec((None, sdim, kdim), lambda j: (j, 0, 0)),
            pl.BlockSpec((None, 5 * SUBLANES, sdim), lambda j: (j, 0, 0)),
            pl.BlockSpec((1, LANES), lambda j: (0, j)),
        ],
        out_specs=tok,
        out_shape=jax.ShapeDtypeStruct((nj, rows, LANES), BF16),
        scratch_shapes=[pltpu.VMEM((rows // S5_CHUNK, sdim), F32), pltpu.VMEM((rows, LANES), F32)],
        compiler_params=_params(("parallel",)),
        name="s5_core",
    )(u3, w_io, w_is, w_so, tab, d_skip.reshape(1, nj * LANES))


def _band_windows(width, blk, tn):
    starts, ends = [], []
    for n in range(width // tn):
        h_lo = (n * tn) // blk
        h_hi = (n * tn + tn - 1) // blk
        starts.append((h_lo * blk) // LANES)
        ends.append(-(-((h_hi + 1) * blk) // LANES))
    kw = max(e - s for s, e in zip(starts, ends))
    total = width // LANES
    starts = [min(s, total - kw) for s in starts]
    return starts, kw * LANES


def _banded(w, starts, kw, tn):
    nblk, blk, _ = w.shape
    tiles = []
    for n, s in enumerate(starts):
        pieces = []
        for h in range((n * tn) // blk, (n * tn + tn - 1) // blk + 1):
            j0 = max(0, n * tn - h * blk)
            j1 = min(blk, (n + 1) * tn - h * blk)
            r0 = h * blk - s * LANES
            pieces.append(jnp.pad(w[h][:, j0:j1], ((r0, kw - r0 - blk), (0, 0))))
        tiles.append(jnp.concatenate(pieces, axis=1))
    return jnp.stack(tiles).astype(BF16)


def _make_lru_kernel(e, tm, tn, starts, kw, tiles_per_seq):
    ntile = e // tn
    nrow = tm // SUBLANES

    def kern(gx_ref, wrg_ref, wig_ref, brg_ref, big_ref, lam_ref, y_ref, hcar):
        @pl.when(pl.program_id(0) % tiles_per_seq == 0)
        def _():
            hcar[...] = jnp.zeros((1, e), F32)

        row = lax.broadcasted_iota(jnp.int32, (SUBLANES, tn), 0)
        masks = [row >= d for d in (1, 2, 4)]
        lam = lam_ref[...]
        neg = -lam
        softplus = jnp.maximum(neg, 0.0) + jnp.log1p(jnp.exp(-jnp.abs(neg)))
        rate = (-0.5 * LRU_C) * softplus
        hb_rg = 0.5 * brg_ref[...]
        hb_ig = 0.5 * big_ref[...]

        for n in range(ntile):
            cols = slice(n * tn, (n + 1) * tn)
            k0 = e + starts[n] * LANES
            win = gx_ref[:, k0:k0 + kw].astype(BF16)
            tr = jnp.tanh(0.5 * _dot(win, wrg_ref[n]) + hb_rg[:, cols])
            ig = 0.5 * jnp.tanh(0.5 * _dot(win, wig_ref[n]) + hb_ig[:, cols]) + 0.5
            a = jnp.exp(rate[:, cols] * tr + rate[:, cols])
            om = 1.0 - a * a
            mult = jnp.where(om > 0.0, om * lax.rsqrt(om), 0.0)
            b = mult * (ig * gx_ref[:, e + n * tn:e + (n + 1) * tn])

            c = hcar[:, cols]
            hs = []
            for m in range(nrow):
                rows = slice(m * SUBLANES, (m + 1) * SUBLANES)
                av, bv = a[rows], b[rows]
                for d, keep in zip((1, 2, 4), masks):
                    bv = bv + jnp.where(keep, av * pltpu.roll(bv, d, axis=0), 0.0)
                    av = jnp.where(keep, av * pltpu.roll(av, d, axis=0), av)
                h = bv + av * c
                c = h[SUBLANES - 1:SUBLANES]
                hs.append(h)
            hcar[:, cols] = c
            y_ref[:, cols] = (jnp.concatenate(hs, axis=0) * gx_ref[:, cols]).astype(BF16)

    return kern


def _lru_core(gx, w_rg, b_rg, w_ig, b_ig, lam, *, seq, tm):
    rows, e2 = gx.shape
    e = e2 // 2
    blk = w_rg.shape[1]
    tn = MXU_N
    assert e % tn == 0 and seq % tm == 0 and tm % SUBLANES == 0
    starts, kw = _band_windows(e, blk, tn)
    wrg, wig = _banded(w_rg, starts, kw, tn), _banded(w_ig, starts, kw, tn)
    ntile = e // tn
    full = lambda shape: pl.BlockSpec(shape, lambda m: (0,) * len(shape))
    return pl.pallas_call(
        _make_lru_kernel(e, tm, tn, starts, kw, seq // tm),
        grid=(rows // tm,),
        in_specs=[
            pl.BlockSpec((tm, e2), lambda m: (m, 0)),
            full((ntile, kw, tn)),
            full((ntile, kw, tn)),
            full((1, e)),
            full((1, e)),
            full((1, e)),
        ],
        out_specs=pl.BlockSpec((tm, e), lambda m: (m, 0)),
        out_shape=jax.ShapeDtypeStruct((rows, e), BF16),
        scratch_shapes=[pltpu.VMEM((1, e), F32)],
        compiler_params=_params(("arbitrary",)),
        name="lru_core",
    )(gx, wrg, wig, b_rg.reshape(1, e), b_ig.reshape(1, e), lam.reshape(1, e))


def _tile(n, pref):
    t = min(n, pref)
    while n % t:
        t //= 2
    return t


def kernel(x, c, norm_g, w_ada, b_ada, s5_w_in, s5_lam_re, s5_lam_im, s5_log_dt, s5_b_re, s5_b_im, s5_c_re, s5_c_im, s5_d, s5_w_glu, lru_w_in, lru_conv_w, lru_conv_b, lru_w_rg, lru_b_rg, lru_w_ig, lru_b_ig, lru_lam, lru_w_out, ffn_w_gu, ffn_w_down, final_g):
    bsz, seq, d = x.shape
    depth = w_ada.shape[0]
    rows = bsz * seq
    x2 = x.reshape(rows, d)
    mods = _modulation(c, w_ada, b_ada).reshape(depth, bsz, 6, 1, d)
    tm_in = _tile(seq, 512)
    tm_res = _tile(seq, 256)
    tm_up = _tile(rows, 1024)
    hidden = ffn_w_down.shape[1]

    h = None
    for i in range(depth):
        sh1, sc1, g1, sh2, sc2, g2 = [mods[i, :, q] for q in range(6)]
        ffn_norm = (norm_g[i, 1], sc2, sh2)
        j = i // 2
        if i % 2 == 0:
            u3 = _ln_matmul_tiled(x2, norm_g[i, 0], sc1, sh1, s5_w_in, j, seq=seq, tm=tm_in,
                                  name="s5_in")
            ops = _s5_operators(s5_lam_re[j], s5_lam_im[j], s5_log_dt[j], s5_b_re[j], s5_b_im[j],
                                s5_c_re[j], s5_c_im[j])
            y3 = _s5_core(u3, ops, s5_d[j], nb=bsz, seq=seq)
            x2, h = _matmul_residual(y3, s5_w_glu, j, x2, g1, seq=seq, tm=tm_res, glu=True,
                                     tiled_a=True, next_norm=ffn_norm, name="s5_out")
        else:
            gx = _lru_in(h, lru_w_in, j, lru_conv_w[j], lru_conv_b[j], seq=seq, tm=tm_res,
                         name="lru_in")
            y = _lru_core(gx, lru_w_rg[j], lru_b_rg[j], lru_w_ig[j], lru_b_ig[j], lru_lam[j],
                          seq=seq, tm=tm_res)
            x2, h = _matmul_residual(y, lru_w_out, j, x2, g1, seq=seq, tm=tm_res,
                                     next_norm=ffn_norm, name="lru_out")
        act = _swiglu_up(h, ffn_w_gu, i, tm=tm_up, tn=_tile(hidden, 512), name="ffn_up")
        if i + 1 == depth:
            out = _matmul_residual(act, ffn_w_down, i, x2, g2, seq=seq, tm=tm_res, final_gain=final_g,
                                   name="ffn_down")
            return out.reshape(bsz, seq, d)
        nxt = None
        if (i + 1) % 2 == 1:
            nxt = (norm_g[i + 1, 0], mods[i + 1, :, 1], mods[i + 1, :, 0])
        x2, h = _matmul_residual(act, ffn_w_down, i, x2, g2, seq=seq, tm=tm_res, next_norm=nxt,
                                 name="ffn_down")
```

```python
import jax
import jax.numpy as jnp
from jax import lax
from jax.experimental import pallas as pl
from jax.experimental.pallas import tpu as pltpu

F32 = jnp.float32
BF16 = jnp.bfloat16

EPS = 1e-6
LRU_C = 8.0
LANES = 128
SUBLANES = 8
MXU_N = 256
S5_CHUNK = 8
S5_TILE_GROUPS = 8
VMEM_LIMIT = 56 * 1024 * 1024
W_CHUNK_BYTES = 4 * 1024 * 1024


def _params(semantics, vmem=VMEM_LIMIT):
    return pltpu.CompilerParams(dimension_semantics=semantics, vmem_limit_bytes=vmem)


def _dot(a, b):
    return jnp.dot(a, b, preferred_element_type=F32)


def _rms_mod(x, gain, scale, shift):
    ms = jnp.mean(x * x, axis=-1, keepdims=True)
    y = x * lax.rsqrt(ms + EPS) * gain
    return y * (1.0 + scale) + shift


def _cmul(ar, ai, br, bi):
    return ar * br - ai * bi, ar * bi + ai * br


def _mod_kernel(c_ref, w_ref, b_ref, o_ref):
    c = c_ref[...]
    cond = c * jax.nn.sigmoid(c)
    o_ref[...] = _dot(cond.astype(BF16), w_ref[...].astype(BF16)) + b_ref[...]


def _modulation(c, w_ada, b_ada):
    depth, d, n = w_ada.shape
    b = c.shape[0]
    rows = -(-b // SUBLANES) * SUBLANES
    cp = jnp.zeros((rows, d), F32).at[:b].set(c)
    tn = 1024 if n % 1024 == 0 else n
    out = pl.pallas_call(
        _mod_kernel,
        grid=(depth, n // tn),
        in_specs=[
            pl.BlockSpec((rows, d), lambda i, j: (0, 0)),
            pl.BlockSpec((None, d, tn), lambda i, j: (i, 0, j)),
            pl.BlockSpec((None, 1, tn), lambda i, j: (i, 0, j)),
        ],
        out_specs=pl.BlockSpec((None, rows, tn), lambda i, j: (i, 0, j)),
        out_shape=jax.ShapeDtypeStruct((depth, rows, n), F32),
        compiler_params=_params(("parallel", "parallel")),
        name="adaln_mod",
    )(cp, w_ada, b_ada.reshape(depth, 1, n))
    return out


class _Resident:
    def __init__(self, w_all, layer, rows, tm, seq):
        _, self.k, self.n = w_all.shape
        assert rows % tm == 0 and seq % tm == 0
        ck = self.k
        while ck * self.n * 4 > W_CHUNK_BYTES and ck % 32 == 0:
            ck //= 2
        self.ck, self.nc = ck, self.k // ck
        self.layer, self.tm, self.seq = layer, tm, seq
        self.grid = (self.nc + rows // tm,)
        self.scratch = pltpu.VMEM((self.k, self.n), BF16)

    def tile(self, m):
        return jnp.maximum(m - self.nc, 0)

    def weight(self):
        return pl.BlockSpec((None, self.ck, self.n),
                            lambda m: (self.layer, jnp.minimum(m, self.nc - 1), 0))

    def rows(self, width):
        return pl.BlockSpec((self.tm, width), lambda m: (self.tile(m), 0))

    def whole(self, shape):
        return pl.BlockSpec(shape, lambda m: (0,) * len(shape))

    def mod_vec(self, sel):
        table, layer, which = sel
        return pl.BlockSpec(
            (None, 1, None, 1, table.shape[-1]),
            lambda m: (layer, (self.tile(m) * self.tm) // self.seq, which, 0, 0))

    def gain_vec(self, sel):
        table, i, j = sel
        return pl.BlockSpec((None, None, 1, table.shape[-1]), lambda m: (i, j, 0, 0))

    def kernel(self, body, w_pos):
        nc, ck = self.nc, self.ck

        def kern(*refs):
            w_s = refs[-1]
            refs = list(refs[:-1])
            m = pl.program_id(0)

            @pl.when(m < nc)
            def _():
                w_s[pl.ds(pl.multiple_of(m * ck, ck), ck), :] = refs[w_pos][...].astype(BF16)

            @pl.when(m >= nc)
            def _():
                body(*refs[:w_pos], w_s, *refs[w_pos + 1:])

        return kern


def _ln_mm_kernel(x_ref, g_ref, sc_ref, sh_ref, w_ref, o_ref):
    h = _rms_mod(x_ref[...], g_ref[...], sc_ref[0], sh_ref[0]).astype(BF16)
    acc = _dot(h, w_ref[...])
    for q in range(o_ref.shape[0]):
        o_ref[q] = acc[:, q * LANES:(q + 1) * LANES]


def _ln_matmul_tiled(x2d, gain, scale, shift, w_all, layer, *, seq, tm, name):
    rows, d = x2d.shape
    n_out = w_all.shape[-1]
    assert n_out % LANES == 0
    plan = _Resident(w_all, layer, rows, tm, seq)
    return pl.pallas_call(
        plan.kernel(_ln_mm_kernel, 4),
        grid=plan.grid,
        in_specs=[plan.rows(d), plan.gain_vec(gain), plan.mod_vec(scale), plan.mod_vec(shift),
                  plan.weight()],
        out_specs=pl.BlockSpec((n_out // LANES, tm, LANES), lambda m: (0, plan.tile(m), 0)),
        out_shape=jax.ShapeDtypeStruct((n_out // LANES, rows, LANES), F32),
        scratch_shapes=[plan.scratch],
        compiler_params=_params(("arbitrary",)),
        name=name,
    )(x2d, gain[0], scale[0], shift[0], w_all)


def _make_lru_in_kernel(e, tm, tiles_per_seq, first_step):
    halo = SUBLANES

    def kern(a_ref, w_ref, cw_ref, cb_ref, o_ref, xpad):
        @pl.when((pl.program_id(0) - first_step) % tiles_per_seq == 0)
        def _():
            xpad[0:halo, :] = jnp.zeros((halo, e), F32)

        acc = _dot(a_ref[...], w_ref[...])
        o_ref[:, 0:e] = jax.nn.gelu(acc[:, 0:e])
        xpad[halo:halo + tm, :] = acc[:, e:2 * e]
        nk = cw_ref.shape[0]
        xc = cb_ref[...] + cw_ref[nk - 1:nk, :] * xpad[halo:halo + tm, :]
        for k in range(nk - 1):
            off = halo - (nk - 1) + k
            xc = xc + cw_ref[k:k + 1, :] * xpad[off:off + tm, :]
        xpad[0:halo, :] = xpad[tm:tm + halo, :]
        o_ref[:, e:2 * e] = xc

    return kern


def _lru_in(a, w_all, layer, conv_w, conv_b, *, seq, tm, name):
    rows, k = a.shape
    e = w_all.shape[-1] // 2
    nk = conv_w.shape[0]
    assert nk - 1 <= SUBLANES
    plan = _Resident(w_all, layer, rows, tm, seq)
    return pl.pallas_call(
        plan.kernel(_make_lru_in_kernel(e, tm, seq // tm, plan.nc), 1),
        grid=plan.grid,
        in_specs=[plan.rows(k), plan.weight(), plan.whole((nk, e)), plan.whole((1, e))],
        out_specs=plan.rows(2 * e),
        out_shape=jax.ShapeDtypeStruct((rows, 2 * e), F32),
        scratch_shapes=[pltpu.VMEM((tm + SUBLANES, e), F32), plan.scratch],
        compiler_params=_params(("arbitrary",)),
        name=name,
    )(a, w_all, conv_w.reshape(nk, e), conv_b.reshape(1, e))


def _swiglu_kernel(h_ref, wg_ref, wu_ref, o_ref, wg_s, wu_s):
    @pl.when(pl.program_id(1) == 0)
    def _():
        wg_s[...] = wg_ref[...].astype(BF16)
        wu_s[...] = wu_ref[...].astype(BF16)

    h = h_ref[...]
    g = _dot(h, wg_s[...])
    o_ref[...] = (g * jax.nn.sigmoid(g) * _dot(h, wu_s[...])).astype(BF16)


def _swiglu_up(h, w_all, layer, *, tm, tn, name):
    rows, d = h.shape
    n_out = w_all.shape[-1] // 2
    assert rows % tm == 0 and n_out % tn == 0
    nt = n_out // tn
    return pl.pallas_call(
        _swiglu_kernel,
        grid=(nt, rows // tm),
        in_specs=[
            pl.BlockSpec((tm, d), lambda n, m: (m, 0)),
            pl.BlockSpec((None, d, tn), lambda n, m: (layer, 0, n)),
            pl.BlockSpec((None, d, tn), lambda n, m: (layer, 0, n + nt)),
        ],
        out_specs=pl.BlockSpec((tm, tn), lambda n, m: (m, n)),
        out_shape=jax.ShapeDtypeStruct((rows, n_out), BF16),
        scratch_shapes=[pltpu.VMEM((d, tn), BF16), pltpu.VMEM((d, tn), BF16)],
        compiler_params=_params(("parallel", "arbitrary")),
        name=name,
    )(h, w_all, w_all)


def _make_mm_res_kernel(glu, tiled_a, n_out, mode):
    def kern(*refs):
        a_ref, w_ref, res_ref, gate_ref = refs[:4]
        if tiled_a:
            a = jnp.concatenate([a_ref[q] for q in range(a_ref.shape[0])], axis=1)
        else:
            a = a_ref[...]
        acc = _dot(a, w_ref[...])
        if glu:
            acc = acc[:, :n_out] * jax.nn.sigmoid(acc[:, n_out:])
        x = res_ref[...] + gate_ref[0] * acc
        if mode == "final":
            g_ref, o_ref = refs[4:]
            ms = jnp.mean(x * x, axis=-1, keepdims=True)
            o_ref[...] = x * lax.rsqrt(ms + EPS) * g_ref[...]
        elif mode == "x+h":
            g_ref, sc_ref, sh_ref, x_out, h_out = refs[4:]
            x_out[...] = x
            h_out[...] = _rms_mod(x, g_ref[...], sc_ref[0], sh_ref[0]).astype(BF16)
        else:
            (x_out,) = refs[4:]
            x_out[...] = x

    return kern


def _matmul_residual(a, w_all, layer, res2d, gate, *, seq, tm, glu=False, tiled_a=False,
                     next_norm=None, final_gain=None, name):
    rows, n_out = res2d.shape
    k = w_all.shape[1]
    plan = _Resident(w_all, layer, rows, tm, seq)
    if tiled_a:
        a_spec = pl.BlockSpec((k // LANES, tm, LANES), lambda m: (0, plan.tile(m), 0))
    else:
        a_spec = plan.rows(k)
    row_spec = plan.rows(n_out)
    in_specs = [a_spec, plan.weight(), row_spec, plan.mod_vec(gate)]
    args = [a, w_all, res2d, gate[0]]
    x_shape = jax.ShapeDtypeStruct((rows, n_out), F32)
    if final_gain is not None:
        mode = "final"
        in_specs.append(plan.whole((1, n_out)))
        args.append(final_gain.reshape(1, n_out))
        out_specs, out_shape = row_spec, x_shape
    elif next_norm is not None:
        mode = "x+h"
        gain, scale, shift = next_norm
        in_specs += [plan.gain_vec(gain), plan.mod_vec(scale), plan.mod_vec(shift)]
        args += [gain[0], scale[0], shift[0]]
        out_specs = [row_spec, row_spec]
        out_shape = [x_shape, jax.ShapeDtypeStruct((rows, n_out), BF16)]
    else:
        mode = "x"
        out_specs, out_shape = [row_spec], [x_shape]
    out = pl.pallas_call(
        plan.kernel(_make_mm_res_kernel(glu, tiled_a, n_out, mode), 1),
        grid=plan.grid,
        in_specs=in_specs,
        out_specs=out_specs,
        out_shape=out_shape,
        scratch_shapes=[plan.scratch],
        compiler_params=_params(("arbitrary",)),
        name=name,
    )(*args)
    if mode == "final":
        return out
    return (out[0], out[1]) if mode == "x+h" else (out[0], None)


def _make_s5_prep_kernel(tg, p):
    tc = S5_CHUNK
    half = tg * p

    def kern(lr_ref, li_ref, ldt_ref, br_ref, bi_ref, cr_ref, ci_ref,
             wio_ref, wis_ref, wso_ref, tab_ref):
        lr = lr_ref[...]
        li = li_ref[...]
        dt = jnp.exp(ldt_ref[...])
        mag = jnp.exp(lr * dt)
        ab_re = mag * jnp.cos(li * dt)
        ab_im = mag * jnp.sin(li * dt)
        nr, ni = ab_re - 1.0, ab_im
        den = lr * lr + li * li
        f_re = (nr * lr + ni * li) / den
        f_im = (ni * lr - nr * li) / den
        br, bi = br_ref[...], bi_ref[...]
        bb_re = f_re * br - f_im * bi
        bb_im = f_re * bi + f_im * br
        cr, ci = cr_ref[...], ci_ref[...]
        gid = lax.broadcasted_iota(jnp.int32, (1, half), 1) // p

        def expand(v):
            return jnp.concatenate([jnp.where(gid == g, v, 0.0) for g in range(tg)], axis=0)

        pows = [(jnp.ones_like(ab_re), jnp.zeros_like(ab_im))]
        for _ in range(tc):
            pows.append(_cmul(pows[-1][0], pows[-1][1], ab_re, ab_im))

        for t in range(tc):
            pr, pi = pows[tc - 1 - t]
            wr, wi = _cmul(pr, pi, bb_re, bb_im)
            rows = slice(t * LANES, (t + 1) * LANES)
            wis_ref[rows, 0:half] = expand(wr).astype(BF16)
            wis_ref[rows, half:2 * half] = expand(wi).astype(BF16)

        def split(v):
            hi = v.astype(BF16)
            return hi, (v - hi.astype(F32)).astype(BF16)

        def nt_dot(a, b):
            return lax.dot_general(a, b, (((1,), (1,)), ((), ())), preferred_element_type=F32)

        b_hi, b_lo = split(jnp.concatenate([expand(bb_re), expand(bb_im)], axis=1))
        blocks = []
        for tau in range(tc + 1):
            pr, pi = pows[tau]
            qr, qi = _cmul(cr, ci, pr, pi)
            er, ei = expand(qr), expand(-qi)
            if tau >= 1:
                cols = slice((tau - 1) * LANES, tau * LANES)
                wso_ref[0:half, cols] = er.T.astype(BF16)
                wso_ref[half:2 * half, cols] = ei.T.astype(BF16)
            if tau < tc:
                q_hi, q_lo = split(jnp.concatenate([er, ei], axis=1))
                blk = nt_dot(b_hi, q_hi) + (nt_dot(b_hi, q_lo) + nt_dot(b_lo, q_hi))
                blocks.append(blk.astype(BF16))
        zero = jnp.zeros((LANES, LANES), BF16)
        for t in range(tc):
            for t2 in range(tc):
                wio_ref[t * LANES:(t + 1) * LANES, t2 * LANES:(t2 + 1) * LANES] = (
                    blocks[t2 - t] if t2 >= t else zero)

        lam = [(jnp.ones_like(ab_re), jnp.zeros_like(ab_im))]
        for _ in range(SUBLANES):
            lam.append(_cmul(lam[-1][0], lam[-1][1], pows[tc][0], pows[tc][1]))
        zrow = jnp.zeros_like(ab_re)
        groups = [[lam[r] for r in range(SUBLANES)]]
        for d in (1, 2, 4):
            groups.append([lam[d] if r >= d else (zrow, zrow) for r in range(SUBLANES)])
        groups.append([lam[SUBLANES]] * SUBLANES)
        for gi, grp in enumerate(groups):
            rows = slice(gi * SUBLANES, (gi + 1) * SUBLANES)
            tab_ref[rows, 0:half] = jnp.concatenate([v[0] for v in grp], axis=0)
            tab_ref[rows, half:2 * half] = jnp.concatenate([v[1] for v in grp], axis=0)

    return kern


def _s5_operators(lam_re, lam_im, log_dt, b_re, b_im, c_re, c_im):
    g, p = lam_re.shape
    gc = b_re.shape[-1]
    gp = g * p
    tg = S5_TILE_GROUPS
    assert gc * tg == LANES and g % tg == 0
    nj = g // tg
    half = tg * p
    kdim = S5_CHUNK * LANES
    lr = lam_re.reshape(1, gp)
    li = lam_im.reshape(1, gp)
    ldt = jnp.repeat(log_dt, p).reshape(1, gp)
    brt = b_re.transpose(2, 0, 1).reshape(gc, gp)
    bit = b_im.transpose(2, 0, 1).reshape(gc, gp)
    crt = c_re.transpose(1, 0, 2).reshape(gc, gp)
    cit = c_im.transpose(1, 0, 2).reshape(gc, gp)
    row = pl.BlockSpec((1, half), lambda j: (0, j))
    mat = pl.BlockSpec((gc, half), lambda j: (0, j))
    return pl.pallas_call(
        _make_s5_prep_kernel(tg, p),
        grid=(nj,),
        in_specs=[row, row, row, mat, mat, mat, mat],
        out_specs=[
            pl.BlockSpec((None, kdim, kdim), lambda j: (j, 0, 0)),
            pl.BlockSpec((None, kdim, 2 * half), lambda j: (j, 0, 0)),
            pl.BlockSpec((None, 2 * half, kdim), lambda j: (j, 0, 0)),
            pl.BlockSpec((None, 5 * SUBLANES, 2 * half), lambda j: (j, 0, 0)),
        ],
        out_shape=[
            jax.ShapeDtypeStruct((nj, kdim, kdim), BF16),
            jax.ShapeDtypeStruct((nj, kdim, 2 * half), BF16),
            jax.ShapeDtypeStruct((nj, 2 * half, kdim), BF16),
            jax.ShapeDtypeStruct((nj, 5 * SUBLANES, 2 * half), F32),
        ],
        compiler_params=_params(("parallel",)),
        name="s5_prep",
    )(lr, li, ldt, brt, bit, crt, cit)


def _make_s5_kernel(nb, chunks_per_batch, half):
    tc = S5_CHUNK
    nchunk = nb * chunks_per_batch
    tiles_per_batch = chunks_per_batch // SUBLANES
    ntile = nchunk // SUBLANES

    def kern(u_ref, wio_ref, wis_ref, wso_ref, tab_ref, d_ref, y_ref, s_scr, y_scr):
        lhs = jnp.concatenate(
            [u_ref[pl.ds(t, nchunk, stride=tc), :].astype(BF16) for t in range(tc)], axis=1)
        s_scr[...] = _dot(lhs, wis_ref[...])

        tab = tab_ref[...]
        pw_r, pw_i = tab[0:8, :half], tab[0:8, half:]
        steps = [(d, tab[8 * i:8 * i + 8, :half], tab[8 * i:8 * i + 8, half:])
                 for i, d in ((1, 1), (2, 2), (3, 4))]
        l8_r, l8_i = tab[32:33, :half], tab[32:33, half:]
        first_row = lax.broadcasted_iota(jnp.int32, (SUBLANES, half), 0) == 0

        for m in range(ntile):
            rows = slice(m * SUBLANES, (m + 1) * SUBLANES)
            s = s_scr[rows, :]
            er, ei = s[:, :half], s[:, half:]
            for d, mr, mi in steps:
                rr = pltpu.roll(er, d, axis=0)
                ri = pltpu.roll(ei, d, axis=0)
                er, ei = er + (mr * rr - mi * ri), ei + (mr * ri + mi * rr)
            hr = jnp.where(first_row, 0.0, pltpu.roll(er, 1, axis=0))
            hi = jnp.where(first_row, 0.0, pltpu.roll(ei, 1, axis=0))
            if m % tiles_per_batch == 0:
                cr, ci = er[7:8], ei[7:8]
            else:
                hr = hr + (pw_r * cr - pw_i * ci)
                hi = hi + (pw_r * ci + pw_i * cr)
                cr, ci = er[7:8] + (l8_r * cr - l8_i * ci), ei[7:8] + (l8_r * ci + l8_i * cr)
            s_scr[rows, :] = jnp.concatenate([hr, hi], axis=1)

        y_io = jnp.concatenate(
            [_dot(lhs[:, :(p + 1) * MXU_N], wio_ref[0:(p + 1) * MXU_N, p * MXU_N:(p + 1) * MXU_N])
             for p in range(tc * LANES // MXU_N)], axis=1)
        y = y_io + _dot(s_scr[...].astype(BF16), wso_ref[...])
        for t in range(tc):
            y_scr[pl.ds(t, nchunk, stride=tc), :] = y[:, t * LANES:(t + 1) * LANES]
        y_ref[...] = jax.nn.gelu(y_scr[...] + d_ref[...] * u_ref[...]).astype(BF16)

    return kern


def _s5_core(u3, ops, d_skip, *, nb, seq):
    w_io, w_is, w_so, tab = ops
    nj, rows, _ = u3.shape
    kdim = S5_CHUNK * LANES
    sdim = w_is.shape[-1]
    chunks_per_batch = seq // S5_CHUNK
    assert rows == nb * seq and chunks_per_batch % SUBLANES == 0
    tok = pl.BlockSpec((None, rows, LANES), lambda j: (j, 0, 0))
    return pl.pallas_call(
        _make_s5_kernel(nb, chunks_per_batch, sdim // 2),
        grid=(nj,),
        in_specs=[
            tok,
            pl.BlockSpec((None, kdim, kdim), lambda j: (j, 0, 0)),
            pl.BlockSpec((None, kdim, sdim), lambda j: (j, 0, 0)),
            pl.BlockSpec((None, sdim, kdim), lambda j: (j, 0, 0)),
            pl.BlockSpec((None, 5 * SUBLANES, sdim), lambda j: (j, 0, 0)),
            pl.BlockSpec((1, LANES), lambda j: (0, j)),
        ],
        out_specs=tok,
        out_shape=jax.ShapeDtypeStruct((nj, rows, LANES), BF16),
        scratch_shapes=[pltpu.VMEM((rows // S5_CHUNK, sdim), F32), pltpu.VMEM((rows, LANES), F32)],
        compiler_params=_params(("parallel",)),
        name="s5_core",
    )(u3, w_io, w_is, w_so, tab, d_skip.reshape(1, nj * LANES))


def _band_windows(width, blk, tn):
    starts, ends = [], []
    for n in range(width // tn):
        h_lo = (n * tn) // blk
        h_hi = (n * tn + tn - 1) // blk
        starts.append((h_lo * blk) // LANES)
        ends.append(-(-((h_hi + 1) * blk) // LANES))
    kw = max(e - s for s, e in zip(starts, ends))
    total = width // LANES
    starts = [min(s, total - kw) for s in starts]
    return starts, kw * LANES


def _banded(w, starts, kw, tn):
    blk = w.shape[-1]
    tiles = []
    for n, s in enumerate(starts):
        pieces = []
        for h in range((n * tn) // blk, (n * tn + tn - 1) // blk + 1):
            j0 = max(0, n * tn - h * blk)
            j1 = min(blk, (n + 1) * tn - h * blk)
            r0 = h * blk - s * LANES
            pieces.append(jnp.pad(w[:, h, :, j0:j1], ((0, 0), (r0, kw - r0 - blk), (0, 0))))
        tiles.append(jnp.concatenate(pieces, axis=2))
    return jnp.stack(tiles, axis=1).astype(BF16)


def _make_lru_kernel(e, tm, tn, starts, kw, tiles_per_seq):
    ntile = e // tn
    nrow = tm // SUBLANES

    def kern(gx_ref, w_ref, vec_ref, y_ref, hcar):
        @pl.when(pl.program_id(0) % tiles_per_seq == 0)
        def _():
            hcar[...] = jnp.zeros((1, e), F32)

        row = lax.broadcasted_iota(jnp.int32, (SUBLANES, tn), 0)
        masks = [row >= d for d in (1, 2, 4)]
        neg = -vec_ref[2:3, :]
        softplus = jnp.maximum(neg, 0.0) + jnp.log1p(jnp.exp(-jnp.abs(neg)))
        rate = (-0.5 * LRU_C) * softplus
        hb_rg = 0.5 * vec_ref[0:1, :]
        hb_ig = 0.5 * vec_ref[1:2, :]

        for n in range(ntile):
            cols = slice(n * tn, (n + 1) * tn)
            k0 = e + starts[n] * LANES
            win = gx_ref[:, k0:k0 + kw].astype(BF16)
            tr = jnp.tanh(0.5 * _dot(win, w_ref[0, n]) + hb_rg[:, cols])
            ig = 0.5 * jnp.tanh(0.5 * _dot(win, w_ref[1, n]) + hb_ig[:, cols]) + 0.5
            a = jnp.exp(rate[:, cols] * tr + rate[:, cols])
            om = 1.0 - a * a
            mult = jnp.where(om > 0.0, om * lax.rsqrt(om), 0.0)
            b = mult * (ig * gx_ref[:, e + n * tn:e + (n + 1) * tn])

            c = hcar[:, cols]
            hs = []
            for m in range(nrow):
                rows = slice(m * SUBLANES, (m + 1) * SUBLANES)
                av, bv = a[rows], b[rows]
                for d, keep in zip((1, 2, 4), masks):
                    bv = bv + jnp.where(keep, av * pltpu.roll(bv, d, axis=0), 0.0)
                    av = jnp.where(keep, av * pltpu.roll(av, d, axis=0), av)
                h = bv + av * c
                c = h[SUBLANES - 1:SUBLANES]
                hs.append(h)
            hcar[:, cols] = c
            y_ref[:, cols] = (jnp.concatenate(hs, axis=0) * gx_ref[:, cols]).astype(BF16)

    return kern


def _lru_core(gx, w_rg, b_rg, w_ig, b_ig, lam, *, seq, tm):
    rows, e2 = gx.shape
    e = e2 // 2
    blk = w_rg.shape[1]
    tn = MXU_N
    assert e % tn == 0 and seq % tm == 0 and tm % SUBLANES == 0
    starts, kw = _band_windows(e, blk, tn)
    gates = _banded(jnp.stack([w_rg, w_ig]), starts, kw, tn)
    vecs = jnp.stack([b_rg, b_ig, lam])
    ntile = e // tn
    return pl.pallas_call(
        _make_lru_kernel(e, tm, tn, starts, kw, seq // tm),
        grid=(rows // tm,),
        in_specs=[
            pl.BlockSpec((tm, e2), lambda m: (m, 0)),
            pl.BlockSpec((2, ntile, kw, tn), lambda m: (0, 0, 0, 0)),
            pl.BlockSpec((3, e), lambda m: (0, 0)),
        ],
        out_specs=pl.BlockSpec((tm, e), lambda m: (m, 0)),
        out_shape=jax.ShapeDtypeStruct((rows, e), BF16),
        scratch_shapes=[pltpu.VMEM((1, e), F32)],
        compiler_params=_params(("arbitrary",)),
        name="lru_core",
    )(gx, gates, vecs)


def _tile(n, pref):
    t = min(n, pref)
    while n % t:
        t //= 2
    return t


def kernel(x, c, norm_g, w_ada, b_ada, s5_w_in, s5_lam_re, s5_lam_im, s5_log_dt, s5_b_re, s5_b_im, s5_c_re, s5_c_im, s5_d, s5_w_glu, lru_w_in, lru_conv_w, lru_conv_b, lru_w_rg, lru_b_rg, lru_w_ig, lru_b_ig, lru_lam, lru_w_out, ffn_w_gu, ffn_w_down, final_g):
    bsz, seq, d = x.shape
    depth = w_ada.shape[0]
    rows = bsz * seq
    x2 = x.reshape(rows, d)
    mods = _modulation(c, w_ada, b_ada)
    mods = mods.reshape(depth, mods.shape[1], 6, 1, d)
    gains = norm_g.reshape(depth, 2, 1, d)
    tm_in = _tile(seq, 512)
    tm_res = _tile(seq, 256)
    tm_up = _tile(rows, 1024)
    hidden = ffn_w_down.shape[1]

    h = None
    for i in range(depth):
        sh1, sc1, g1, sh2, sc2, g2 = [(mods, i, q) for q in range(6)]
        ffn_norm = ((gains, i, 1), sc2, sh2)
        j = i // 2
        if i % 2 == 0:
            u3 = _ln_matmul_tiled(x2, (gains, i, 0), sc1, sh1, s5_w_in, j, seq=seq, tm=tm_in,
                                  name="s5_in")
            ops = _s5_operators(s5_lam_re[j], s5_lam_im[j], s5_log_dt[j], s5_b_re[j], s5_b_im[j],
                                s5_c_re[j], s5_c_im[j])
            y3 = _s5_core(u3, ops, s5_d[j], nb=bsz, seq=seq)
            x2, h = _matmul_residual(y3, s5_w_glu, j, x2, g1, seq=seq, tm=tm_res, glu=True,
                                     tiled_a=True, next_norm=ffn_norm, name="s5_out")
        else:
            gx = _lru_in(h, lru_w_in, j, lru_conv_w[j], lru_conv_b[j], seq=seq, tm=tm_res,
                         name="lru_in")
            y = _lru_core(gx, lru_w_rg[j], lru_b_rg[j], lru_w_ig[j], lru_b_ig[j], lru_lam[j],
                          seq=seq, tm=tm_res)
            x2, h = _matmul_residual(y, lru_w_out, j, x2, g1, seq=seq, tm=tm_res,
                                     next_norm=ffn_norm, name="lru_out")
        act = _swiglu_up(h, ffn_w_gu, i, tm=tm_up, tn=_tile(hidden, 512), name="ffn_up")
        if i + 1 == depth:
            out = _matmul_residual(act, ffn_w_down, i, x2, g2, seq=seq, tm=tm_res, final_gain=final_g,
                                   name="ffn_down")
            return out.reshape(bsz, seq, d)
        nxt = None
        if (i + 1) % 2 == 1:
            nxt = ((gains, i + 1, 0), (mods, i + 1, 1), (mods, i + 1, 0))
        x2, h = _matmul_residual(act, ffn_w_down, i, x2, g2, seq=seq, tm=tm_res, next_norm=nxt,
                                 name="ffn_down")
```

```python
import jax
import jax.numpy as jnp
from jax import lax
from jax.experimental import pallas as pl
from jax.experimental.pallas import tpu as pltpu

F32 = jnp.float32
BF16 = jnp.bfloat16

EPS = 1e-6
LRU_C = 8.0
GELU_C1 = 0.7978845608028654
GELU_C3 = 0.7978845608028654 * 0.044715
LANES = 128
SUBLANES = 8
MXU_N = 256
S5_CHUNK = 8
S5_TILE_GROUPS = 8
VMEM_LIMIT = 56 * 1024 * 1024
W_CHUNK_BYTES = 8 * 1024 * 1024


def _params(semantics, vmem=VMEM_LIMIT):
    return pltpu.CompilerParams(dimension_semantics=semantics, vmem_limit_bytes=vmem)


def _dot(a, b):
    return jnp.dot(a, b, preferred_element_type=F32)


def _rms_mod(x, gain, scale, shift):
    ms = jnp.mean(x * x, axis=-1, keepdims=True)
    y = x * lax.rsqrt(ms + EPS) * gain
    return y * (1.0 + scale) + shift


def _gelu(x):
    inner = x * (GELU_C1 + GELU_C3 * (x * x))
    hx = 0.5 * x
    return hx + hx * jnp.tanh(inner)


def _cmul(ar, ai, br, bi):
    return ar * br - ai * bi, ar * bi + ai * br


def _mod_kernel(c_ref, w_ref, b_ref, o_ref):
    c = c_ref[...]
    cond = c * jax.nn.sigmoid(c)
    o_ref[...] = _dot(cond.astype(BF16), w_ref[...].astype(BF16)) + b_ref[...]


def _modulation(c, w_ada, b_ada):
    depth, d, n = w_ada.shape
    b = c.shape[0]
    rows = -(-b // SUBLANES) * SUBLANES
    cp = jnp.zeros((rows, d), F32).at[:b].set(c)
    tn = 1024 if n % 1024 == 0 else n
    out = pl.pallas_call(
        _mod_kernel,
        grid=(depth, n // tn),
        in_specs=[
            pl.BlockSpec((rows, d), lambda i, j: (0, 0)),
            pl.BlockSpec((None, d, tn), lambda i, j: (i, 0, j)),
            pl.BlockSpec((None, 1, tn), lambda i, j: (i, 0, j)),
        ],
        out_specs=pl.BlockSpec((None, rows, tn), lambda i, j: (i, 0, j)),
        out_shape=jax.ShapeDtypeStruct((depth, rows, n), F32),
        compiler_params=_params(("parallel", "parallel")),
        name="adaln_mod",
    )(cp, w_ada, b_ada.reshape(depth, 1, n))
    return out


class _Resident:
    def __init__(self, w_all, layer, rows, tm, seq):
        _, self.k, self.n = w_all.shape
        assert rows % tm == 0 and seq % tm == 0
        ck = self.k
        while ck * self.n * 4 > W_CHUNK_BYTES and ck % 32 == 0:
            ck //= 2
        self.ck, self.nc = ck, self.k // ck
        self.layer, self.tm, self.seq = layer, tm, seq
        self.grid = (self.nc + rows // tm,)
        self.scratch = pltpu.VMEM((self.k, self.n), BF16)

    def tile(self, m):
        return jnp.maximum(m - self.nc, 0)

    def weight(self):
        return pl.BlockSpec((None, self.ck, self.n),
                            lambda m: (self.layer, jnp.minimum(m, self.nc - 1), 0))

    def rows(self, width):
        return pl.BlockSpec((self.tm, width), lambda m: (self.tile(m), 0))

    def whole(self, shape):
        return pl.BlockSpec(shape, lambda m: (0,) * len(shape))

    def mod_vec(self, sel):
        table, layer, which = sel
        return pl.BlockSpec(
            (None, 1, None, 1, table.shape[-1]),
            lambda m: (layer, (self.tile(m) * self.tm) // self.seq, which, 0, 0))

    def gain_vec(self, sel):
        table, i, j = sel
        return pl.BlockSpec((None, None, 1, table.shape[-1]), lambda m: (i, j, 0, 0))

    def kernel(self, body, w_pos):
        nc, ck = self.nc, self.ck

        def kern(*refs):
            w_s = refs[-1]
            refs = list(refs[:-1])
            m = pl.program_id(0)

            @pl.when(m < nc)
            def _():
                w_s[pl.ds(pl.multiple_of(m * ck, ck), ck), :] = refs[w_pos][...].astype(BF16)

            @pl.when(m >= nc)
            def _():
                body(*refs[:w_pos], w_s, *refs[w_pos + 1:])

        return kern


def _ln_mm_kernel(x_ref, g_ref, sc_ref, sh_ref, w_ref, o_ref):
    h = _rms_mod(x_ref[...], g_ref[...], sc_ref[0], sh_ref[0]).astype(BF16)
    acc = _dot(h, w_ref[...])
    for q in range(o_ref.shape[0]):
        o_ref[q] = acc[:, q * LANES:(q + 1) * LANES]


def _ln_matmul_tiled(x2d, gain, scale, shift, w_all, layer, *, seq, tm, name):
    rows, d = x2d.shape
    n_out = w_all.shape[-1]
    assert n_out % LANES == 0
    plan = _Resident(w_all, layer, rows, tm, seq)
    return pl.pallas_call(
        plan.kernel(_ln_mm_kernel, 4),
        grid=plan.grid,
        in_specs=[plan.rows(d), plan.gain_vec(gain), plan.mod_vec(scale), plan.mod_vec(shift),
                  plan.weight()],
        out_specs=pl.BlockSpec((n_out // LANES, tm, LANES), lambda m: (0, plan.tile(m), 0)),
        out_shape=jax.ShapeDtypeStruct((n_out // LANES, rows, LANES), F32),
        scratch_shapes=[plan.scratch],
        compiler_params=_params(("arbitrary",)),
        name=name,
    )(x2d, gain[0], scale[0], shift[0], w_all)


def _make_lru_in_kernel(e, tm, tiles_per_seq, first_step):
    halo = SUBLANES

    def kern(a_ref, w_ref, cw_ref, cb_ref, o_ref, xpad):
        @pl.when((pl.program_id(0) - first_step) % tiles_per_seq == 0)
        def _():
            xpad[0:halo, :] = jnp.zeros((halo, e), F32)

        acc = _dot(a_ref[...], w_ref[...])
        o_ref[:, 0:e] = _gelu(acc[:, 0:e])
        xpad[halo:halo + tm, :] = acc[:, e:2 * e]
        nk = cw_ref.shape[0]
        xc = cb_ref[...] + cw_ref[nk - 1:nk, :] * xpad[halo:halo + tm, :]
        for k in range(nk - 1):
            off = halo - (nk - 1) + k
            xc = xc + cw_ref[k:k + 1, :] * xpad[off:off + tm, :]
        xpad[0:halo, :] = xpad[tm:tm + halo, :]
        o_ref[:, e:2 * e] = xc

    return kern


def _lru_in(a, w_all, layer, conv_w, conv_b, *, seq, tm, name):
    rows, k = a.shape
    e = w_all.shape[-1] // 2
    nk = conv_w.shape[0]
    assert nk - 1 <= SUBLANES
    plan = _Resident(w_all, layer, rows, tm, seq)
    return pl.pallas_call(
        plan.kernel(_make_lru_in_kernel(e, tm, seq // tm, plan.nc), 1),
        grid=plan.grid,
        in_specs=[plan.rows(k), plan.weight(), plan.whole((nk, e)), plan.whole((1, e))],
        out_specs=plan.rows(2 * e),
        out_shape=jax.ShapeDtypeStruct((rows, 2 * e), F32),
        scratch_shapes=[pltpu.VMEM((tm + SUBLANES, e), F32), plan.scratch],
        compiler_params=_params(("arbitrary",)),
        name=name,
    )(a, w_all, conv_w.reshape(nk, e), conv_b.reshape(1, e))


def _swiglu_kernel(h_ref, wg_ref, wu_ref, o_ref, wg_s, wu_s):
    @pl.when(pl.program_id(1) == 0)
    def _():
        wg_s[...] = wg_ref[...].astype(BF16)
        wu_s[...] = wu_ref[...].astype(BF16)

    h = h_ref[...]
    g = _dot(h, wg_s[...])
    o_ref[...] = (g * jax.nn.sigmoid(g) * _dot(h, wu_s[...])).astype(BF16)


def _swiglu_up(h, w_all, layer, *, tm, tn, name):
    rows, d = h.shape
    n_out = w_all.shape[-1] // 2
    assert rows % tm == 0 and n_out % tn == 0
    nt = n_out // tn
    return pl.pallas_call(
        _swiglu_kernel,
        grid=(nt, rows // tm),
        in_specs=[
            pl.BlockSpec((tm, d), lambda n, m: (m, 0)),
            pl.BlockSpec((None, d, tn), lambda n, m: (layer, 0, n)),
            pl.BlockSpec((None, d, tn), lambda n, m: (layer, 0, n + nt)),
        ],
        out_specs=pl.BlockSpec((tm, tn), lambda n, m: (m, n)),
        out_shape=jax.ShapeDtypeStruct((rows, n_out), BF16),
        scratch_shapes=[pltpu.VMEM((d, tn), BF16), pltpu.VMEM((d, tn), BF16)],
        compiler_params=_params(("parallel", "arbitrary")),
        name=name,
    )(h, w_all, w_all)


def _make_mm_res_kernel(glu, tiled_a, n_out, mode):
    def kern(*refs):
        a_ref, w_ref, res_ref, gate_ref = refs[:4]
        if tiled_a:
            a = jnp.concatenate([a_ref[q] for q in range(a_ref.shape[0])], axis=1)
        else:
            a = a_ref[...]
        acc = _dot(a, w_ref[...])
        if glu:
            acc = acc[:, :n_out] * jax.nn.sigmoid(acc[:, n_out:])
        x = res_ref[...] + gate_ref[0] * acc
        if mode == "final":
            g_ref, o_ref = refs[4:]
            ms = jnp.mean(x * x, axis=-1, keepdims=True)
            o_ref[...] = x * lax.rsqrt(ms + EPS) * g_ref[...]
        elif mode == "x+h":
            g_ref, sc_ref, sh_ref, x_out, h_out = refs[4:]
            x_out[...] = x
            h_out[...] = _rms_mod(x, g_ref[...], sc_ref[0], sh_ref[0]).astype(BF16)
        else:
            (x_out,) = refs[4:]
            x_out[...] = x

    return kern


def _matmul_residual(a, w_all, layer, res2d, gate, *, seq, tm, glu=False, tiled_a=False,
                     next_norm=None, final_gain=None, name):
    rows, n_out = res2d.shape
    k = w_all.shape[1]
    plan = _Resident(w_all, layer, rows, tm, seq)
    if tiled_a:
        a_spec = pl.BlockSpec((k // LANES, tm, LANES), lambda m: (0, plan.tile(m), 0))
    else:
        a_spec = plan.rows(k)
    row_spec = plan.rows(n_out)
    in_specs = [a_spec, plan.weight(), row_spec, plan.mod_vec(gate)]
    args = [a, w_all, res2d, gate[0]]
    x_shape = jax.ShapeDtypeStruct((rows, n_out), F32)
    if final_gain is not None:
        mode = "final"
        in_specs.append(plan.whole((1, n_out)))
        args.append(final_gain.reshape(1, n_out))
        out_specs, out_shape = row_spec, x_shape
    elif next_norm is not None:
        mode = "x+h"
        gain, scale, shift = next_norm
        in_specs += [plan.gain_vec(gain), plan.mod_vec(scale), plan.mod_vec(shift)]
        args += [gain[0], scale[0], shift[0]]
        out_specs = [row_spec, row_spec]
        out_shape = [x_shape, jax.ShapeDtypeStruct((rows, n_out), BF16)]
    else:
        mode = "x"
        out_specs, out_shape = [row_spec], [x_shape]
    out = pl.pallas_call(
        plan.kernel(_make_mm_res_kernel(glu, tiled_a, n_out, mode), 1),
        grid=plan.grid,
        in_specs=in_specs,
        out_specs=out_specs,
        out_shape=out_shape,
        scratch_shapes=[plan.scratch],
        compiler_params=_params(("arbitrary",)),
        name=name,
    )(*args)
    if mode == "final":
        return out
    return (out[0], out[1]) if mode == "x+h" else (out[0], None)


def _make_s5_prep_kernel(tg, p):
    tc = S5_CHUNK
    half = tg * p

    def kern(lr_ref, li_ref, ldt_ref, br_ref, bi_ref, cr_ref, ci_ref,
             wio_ref, wis_ref, wso_ref, tab_ref):
        lr = lr_ref[...]
        li = li_ref[...]
        dt = jnp.exp(ldt_ref[...])
        mag = jnp.exp(lr * dt)
        ab_re = mag * jnp.cos(li * dt)
        ab_im = mag * jnp.sin(li * dt)
        nr, ni = ab_re - 1.0, ab_im
        den = lr * lr + li * li
        f_re = (nr * lr + ni * li) / den
        f_im = (ni * lr - nr * li) / den
        br, bi = br_ref[...], bi_ref[...]
        bb_re = f_re * br - f_im * bi
        bb_im = f_re * bi + f_im * br
        cr, ci = cr_ref[...], ci_ref[...]
        gid = lax.broadcasted_iota(jnp.int32, (1, half), 1) // p

        def expand(v):
            return jnp.concatenate([jnp.where(gid == g, v, 0.0) for g in range(tg)], axis=0)

        pows = [(jnp.ones_like(ab_re), jnp.zeros_like(ab_im))]
        for _ in range(tc):
            pows.append(_cmul(pows[-1][0], pows[-1][1], ab_re, ab_im))

        for t in range(tc):
            pr, pi = pows[tc - 1 - t]
            wr, wi = _cmul(pr, pi, bb_re, bb_im)
            rows = slice(t * LANES, (t + 1) * LANES)
            wis_ref[rows, 0:half] = expand(wr).astype(BF16)
            wis_ref[rows, half:2 * half] = expand(wi).astype(BF16)

        def split(v):
            hi = v.astype(BF16)
            return hi, (v - hi.astype(F32)).astype(BF16)

        def nt_dot(a, b):
            return lax.dot_general(a, b, (((1,), (1,)), ((), ())), preferred_element_type=F32)

        b_hi, b_lo = split(jnp.concatenate([expand(bb_re), expand(bb_im)], axis=1))
        q_all = []
        for tau in range(tc + 1):
            pr, pi = pows[tau]
            qr, qi = _cmul(cr, ci, pr, pi)
            er, ei = expand(qr), expand(-qi)
            if tau >= 1:
                cols = slice((tau - 1) * LANES, tau * LANES)
                wso_ref[0:half, cols] = er.T.astype(BF16)
                wso_ref[half:2 * half, cols] = ei.T.astype(BF16)
            if tau < tc:
                q_all.append(jnp.concatenate([er, ei], axis=1))
        q_hi, q_lo = split(jnp.concatenate(q_all, axis=0))
        kt = nt_dot(q_hi, b_hi) + (nt_dot(q_lo, b_hi) + nt_dot(q_hi, b_lo))
        blocks = [kt[tau * LANES:(tau + 1) * LANES, :].T.astype(BF16) for tau in range(tc)]
        zero = jnp.zeros((LANES, LANES), BF16)
        for t in range(tc):
            for t2 in range(tc):
                wio_ref[t * LANES:(t + 1) * LANES, t2 * LANES:(t2 + 1) * LANES] = (
                    blocks[t2 - t] if t2 >= t else zero)

        lam = [(jnp.ones_like(ab_re), jnp.zeros_like(ab_im))]
        for _ in range(SUBLANES):
            lam.append(_cmul(lam[-1][0], lam[-1][1], pows[tc][0], pows[tc][1]))
        zrow = jnp.zeros_like(ab_re)
        groups = [[lam[r] for r in range(SUBLANES)]]
        for d in (1, 2, 4):
            groups.append([lam[d] if r >= d else (zrow, zrow) for r in range(SUBLANES)])
        groups.append([lam[SUBLANES]] * SUBLANES)
        for gi, grp in enumerate(groups):
            rows = slice(gi * SUBLANES, (gi + 1) * SUBLANES)
            tab_ref[rows, 0:half] = jnp.concatenate([v[0] for v in grp], axis=0)
            tab_ref[rows, half:2 * half] = jnp.concatenate([v[1] for v in grp], axis=0)

    return kern


def _s5_operators(lam_re, lam_im, log_dt, b_re, b_im, c_re, c_im):
    g, p = lam_re.shape
    gc = b_re.shape[-1]
    gp = g * p
    tg = S5_TILE_GROUPS
    assert gc * tg == LANES and g % tg == 0
    nj = g // tg
    half = tg * p
    kdim = S5_CHUNK * LANES
    lr = lam_re.reshape(1, gp)
    li = lam_im.reshape(1, gp)
    ldt = jnp.repeat(log_dt, p).reshape(1, gp)
    brt = b_re.transpose(2, 0, 1).reshape(gc, gp)
    bit = b_im.transpose(2, 0, 1).reshape(gc, gp)
    crt = c_re.transpose(1, 0, 2).reshape(gc, gp)
    cit = c_im.transpose(1, 0, 2).reshape(gc, gp)
    row = pl.BlockSpec((1, half), lambda j: (0, j))
    mat = pl.BlockSpec((gc, half), lambda j: (0, j))
    return pl.pallas_call(
        _make_s5_prep_kernel(tg, p),
        grid=(nj,),
        in_specs=[row, row, row, mat, mat, mat, mat],
        out_specs=[
            pl.BlockSpec((None, kdim, kdim), lambda j: (j, 0, 0)),
            pl.BlockSpec((None, kdim, 2 * half), lambda j: (j, 0, 0)),
            pl.BlockSpec((None, 2 * half, kdim), lambda j: (j, 0, 0)),
            pl.BlockSpec((None, 5 * SUBLANES, 2 * half), lambda j: (j, 0, 0)),
        ],
        out_shape=[
            jax.ShapeDtypeStruct((nj, kdim, kdim), BF16),
            jax.ShapeDtypeStruct((nj, kdim, 2 * half), BF16),
            jax.ShapeDtypeStruct((nj, 2 * half, kdim), BF16),
            jax.ShapeDtypeStruct((nj, 5 * SUBLANES, 2 * half), F32),
        ],
        compiler_params=_params(("parallel",)),
        name="s5_prep",
    )(lr, li, ldt, brt, bit, crt, cit)


def _make_s5_kernel(nb, chunks_per_batch, half):
    tc = S5_CHUNK
    nchunk = nb * chunks_per_batch
    tiles_per_batch = chunks_per_batch // SUBLANES
    ntile = nchunk // SUBLANES

    def kern(u_ref, wio_ref, wis_ref, wso_ref, tab_ref, d_ref, y_ref, s_scr, y_scr):
        lhs = jnp.concatenate(
            [u_ref[pl.ds(t, nchunk, stride=tc), :].astype(BF16) for t in range(tc)], axis=1)
        s_scr[...] = _dot(lhs, wis_ref[...])

        tab = tab_ref[...]
        pw_r, pw_i = tab[0:8, :half], tab[0:8, half:]
        steps = [(d, tab[8 * i:8 * i + 8, :half], tab[8 * i:8 * i + 8, half:])
                 for i, d in ((1, 1), (2, 2), (3, 4))]
        l8_r, l8_i = tab[32:33, :half], tab[32:33, half:]
        first_row = lax.broadcasted_iota(jnp.int32, (SUBLANES, half), 0) == 0

        for m in range(ntile):
            rows = slice(m * SUBLANES, (m + 1) * SUBLANES)
            s = s_scr[rows, :]
            er, ei = s[:, :half], s[:, half:]
            for d, mr, mi in steps:
                rr = pltpu.roll(er, d, axis=0)
                ri = pltpu.roll(ei, d, axis=0)
                er, ei = er + (mr * rr - mi * ri), ei + (mr * ri + mi * rr)
            hr = jnp.where(first_row, 0.0, pltpu.roll(er, 1, axis=0))
            hi = jnp.where(first_row, 0.0, pltpu.roll(ei, 1, axis=0))
            if m % tiles_per_batch == 0:
                cr, ci = er[7:8], ei[7:8]
            else:
                hr = hr + (pw_r * cr - pw_i * ci)
                hi = hi + (pw_r * ci + pw_i * cr)
                cr, ci = er[7:8] + (l8_r * cr - l8_i * ci), ei[7:8] + (l8_r * ci + l8_i * cr)
            s_scr[rows, :] = jnp.concatenate([hr, hi], axis=1)

        y_io = jnp.concatenate(
            [_dot(lhs[:, :(p + 1) * MXU_N], wio_ref[0:(p + 1) * MXU_N, p * MXU_N:(p + 1) * MXU_N])
             for p in range(tc * LANES // MXU_N)], axis=1)
        y = y_io + _dot(s_scr[...].astype(BF16), wso_ref[...])
        for t in range(tc):
            y_scr[pl.ds(t, nchunk, stride=tc), :] = y[:, t * LANES:(t + 1) * LANES]
        y_ref[...] = _gelu(y_scr[...] + d_ref[...] * u_ref[...]).astype(BF16)

    return kern


def _s5_core(u3, ops, d_skip, *, nb, seq):
    w_io, w_is, w_so, tab = ops
    nj, rows, _ = u3.shape
    kdim = S5_CHUNK * LANES
    sdim = w_is.shape[-1]
    chunks_per_batch = seq // S5_CHUNK
    assert rows == nb * seq and chunks_per_batch % SUBLANES == 0
    tok = pl.BlockSpec((None, rows, LANES), lambda j: (j, 0, 0))
    return pl.pallas_call(
        _make_s5_kernel(nb, chunks_per_batch, sdim // 2),
        grid=(nj,),
        in_specs=[
            tok,
            pl.BlockSpec((None, kdim, kdim), lambda j: (j, 0, 0)),
            pl.BlockSpec((None, kdim, sdim), lambda j: (j, 0, 0)),
            pl.BlockSpec((None, sdim, kdim), lambda j: (j, 0, 0)),
            pl.BlockSpec((None, 5 * SUBLANES, sdim), lambda j: (j, 0, 0)),
            pl.BlockSpec((1, LANES), lambda j: (0, j)),
        ],
        out_specs=tok,
        out_shape=jax.ShapeDtypeStruct((nj, rows, LANES), BF16),
        scratch_shapes=[pltpu.VMEM((rows // S5_CHUNK, sdim), F32), pltpu.VMEM((rows, LANES), F32)],
        compiler_params=_params(("parallel",)),
        name="s5_core",
    )(u3, w_io, w_is, w_so, tab, d_skip.reshape(1, nj * LANES))


def _band_windows(width, blk, tn):
    starts, ends = [], []
    for n in range(width // tn):
        h_lo = (n * tn) // blk
        h_hi = (n * tn + tn - 1) // blk
        starts.append((h_lo * blk) // LANES)
        ends.append(-(-((h_hi + 1) * blk) // LANES))
    kw = max(e - s for s, e in zip(starts, ends))
    total = width // LANES
    starts = [min(s, total - kw) for s in starts]
    return starts, kw * LANES


def _banded(w, starts, kw, tn):
    blk = w.shape[-1]
    tiles = []
    for n, s in enumerate(starts):
        pieces = []
        for h in range((n * tn) // blk, (n * tn + tn - 1) // blk + 1):
            j0 = max(0, n * tn - h * blk)
            j1 = min(blk, (n + 1) * tn - h * blk)
            r0 = h * blk - s * LANES
            pieces.append(jnp.pad(w[:, h, :, j0:j1], ((0, 0), (r0, kw - r0 - blk), (0, 0))))
        tiles.append(jnp.concatenate(pieces, axis=2))
    return jnp.stack(tiles, axis=1).astype(BF16)


def _make_lru_kernel(e, tm, tn, starts, kw, tiles_per_seq):
    ntile = e // tn
    nrow = tm // SUBLANES

    def kern(gx_ref, w_ref, vec_ref, y_ref, hcar):
        @pl.when(pl.program_id(0) % tiles_per_seq == 0)
        def _():
            hcar[...] = jnp.zeros((1, e), F32)

        row = lax.broadcasted_iota(jnp.int32, (SUBLANES, tn), 0)
        masks = [row >= d for d in (1, 2, 4)]
        neg = -vec_ref[2:3, :]
        softplus = jnp.maximum(neg, 0.0) + jnp.log1p(jnp.exp(-jnp.abs(neg)))
        rate = (-0.5 * LRU_C) * softplus
        hb_rg = 0.5 * vec_ref[0:1, :]
        hb_ig = 0.5 * vec_ref[1:2, :]

        for n in range(ntile):
            cols = slice(n * tn, (n + 1) * tn)
            k0 = e + starts[n] * LANES
            win = gx_ref[:, k0:k0 + kw].astype(BF16)
            tr = jnp.tanh(0.5 * _dot(win, w_ref[0, n]) + hb_rg[:, cols])
            ig = 0.5 * jnp.tanh(0.5 * _dot(win, w_ref[1, n]) + hb_ig[:, cols]) + 0.5
            a = jnp.exp(rate[:, cols] * tr + rate[:, cols])
            om = 1.0 - a * a
            mult = jnp.where(om > 0.0, om * lax.rsqrt(om), 0.0)
            b = mult * (ig * gx_ref[:, e + n * tn:e + (n + 1) * tn])

            c = hcar[:, cols]
            hs = []
            for m in range(nrow):
                rows = slice(m * SUBLANES, (m + 1) * SUBLANES)
                av, bv = a[rows], b[rows]
                for d, keep in zip((1, 2, 4), masks):
                    bv = bv + jnp.where(keep, av * pltpu.roll(bv, d, axis=0), 0.0)
                    av = jnp.where(keep, av * pltpu.roll(av, d, axis=0), av)
                h = bv + av * c
                c = h[SUBLANES - 1:SUBLANES]
                hs.append(h)
            hcar[:, cols] = c
            y_ref[:, cols] = (jnp.concatenate(hs, axis=0) * gx_ref[:, cols]).astype(BF16)

    return kern


def _lru_core(gx, w_rg, b_rg, w_ig, b_ig, lam, *, seq, tm):
    rows, e2 = gx.shape
    e = e2 // 2
    blk = w_rg.shape[1]
    tn = MXU_N
    assert e % tn == 0 and seq % tm == 0 and tm % SUBLANES == 0
    starts, kw = _band_windows(e, blk, tn)
    gates = _banded(jnp.stack([w_rg, w_ig]), starts, kw, tn)
    vecs = jnp.stack([b_rg, b_ig, lam])
    ntile = e // tn
    return pl.pallas_call(
        _make_lru_kernel(e, tm, tn, starts, kw, seq // tm),
        grid=(rows // tm,),
        in_specs=[
            pl.BlockSpec((tm, e2), lambda m: (m, 0)),
            pl.BlockSpec((2, ntile, kw, tn), lambda m: (0, 0, 0, 0)),
            pl.BlockSpec((3, e), lambda m: (0, 0)),
        ],
        out_specs=pl.BlockSpec((tm, e), lambda m: (m, 0)),
        out_shape=jax.ShapeDtypeStruct((rows, e), BF16),
        scratch_shapes=[pltpu.VMEM((1, e), F32)],
        compiler_params=_params(("arbitrary",)),
        name="lru_core",
    )(gx, gates, vecs)


def _tile(n, pref):
    t = min(n, pref)
    while n % t:
        t //= 2
    return t


def kernel(x, c, norm_g, w_ada, b_ada, s5_w_in, s5_lam_re, s5_lam_im, s5_log_dt, s5_b_re, s5_b_im, s5_c_re, s5_c_im, s5_d, s5_w_glu, lru_w_in, lru_conv_w, lru_conv_b, lru_w_rg, lru_b_rg, lru_w_ig, lru_b_ig, lru_lam, lru_w_out, ffn_w_gu, ffn_w_down, final_g):
    bsz, seq, d = x.shape
    depth = w_ada.shape[0]
    rows = bsz * seq
    x2 = x.reshape(rows, d)
    mods = _modulation(c, w_ada, b_ada)
    mods = mods.reshape(depth, mods.shape[1], 6, 1, d)
    gains = norm_g.reshape(depth, 2, 1, d)
    tm_in = _tile(seq, 512)
    tm_res = _tile(seq, 256)
    tm_up = _tile(rows, 1024)
    hidden = ffn_w_down.shape[1]

    h = None
    for i in range(depth):
        sh1, sc1, g1, sh2, sc2, g2 = [(mods, i, q) for q in range(6)]
        ffn_norm = ((gains, i, 1), sc2, sh2)
        j = i // 2
        if i % 2 == 0:
            u3 = _ln_matmul_tiled(x2, (gains, i, 0), sc1, sh1, s5_w_in, j, seq=seq, tm=tm_in,
                                  name="s5_in")
            ops = _s5_operators(s5_lam_re[j], s5_lam_im[j], s5_log_dt[j], s5_b_re[j], s5_b_im[j],
                                s5_c_re[j], s5_c_im[j])
            y3 = _s5_core(u3, ops, s5_d[j], nb=bsz, seq=seq)
            x2, h = _matmul_residual(y3, s5_w_glu, j, x2, g1, seq=seq, tm=tm_res, glu=True,
                                     tiled_a=True, next_norm=ffn_norm, name="s5_out")
        else:
            gx = _lru_in(h, lru_w_in, j, lru_conv_w[j], lru_conv_b[j], seq=seq, tm=tm_res,
                         name="lru_in")
            y = _lru_core(gx, lru_w_rg[j], lru_b_rg[j], lru_w_ig[j], lru_b_ig[j], lru_lam[j],
                          seq=seq, tm=tm_res)
            x2, h = _matmul_residual(y, lru_w_out, j, x2, g1, seq=seq, tm=tm_res,
                                     next_norm=ffn_norm, name="lru_out")
        act = _swiglu_up(h, ffn_w_gu, i, tm=tm_up, tn=_tile(hidden, 512), name="ffn_up")
        if i + 1 == depth:
            out = _matmul_residual(act, ffn_w_down, i, x2, g2, seq=seq, tm=tm_res, final_gain=final_g,
                                   name="ffn_down")
            return out.reshape(bsz, seq, d)
        nxt = None
        if (i + 1) % 2 == 1:
            nxt = ((gains, i + 1, 0), (mods, i + 1, 1), (mods, i + 1, 0))
        x2, h = _matmul_residual(act, ffn_w_down, i, x2, g2, seq=seq, tm=tm_res, next_norm=nxt,
                                 name="ffn_down")
```

```python
import jax
import jax.numpy as jnp
from jax import lax
from jax.experimental import pallas as pl
from jax.experimental.pallas import tpu as pltpu

F32 = jnp.float32
BF16 = jnp.bfloat16

EPS = 1e-6
LRU_C = 8.0
GELU_C1 = 0.7978845608028654
GELU_C3 = 0.7978845608028654 * 0.044715
LANES = 128
SUBLANES = 8
MXU_N = 256
S5_CHUNK = 8
S5_TILE_GROUPS = 8
VMEM_LIMIT = 56 * 1024 * 1024
W_CHUNK_BYTES = 8 * 1024 * 1024


def _params(semantics, vmem=VMEM_LIMIT):
    return pltpu.CompilerParams(dimension_semantics=semantics, vmem_limit_bytes=vmem)


def _dot(a, b):
    return jnp.dot(a, b, preferred_element_type=F32)


def _rms_mod(x, gain, scale, shift):
    ms = jnp.mean(x * x, axis=-1, keepdims=True)
    y = x * lax.rsqrt(ms + EPS) * gain
    return y * (1.0 + scale) + shift


def _gelu(x):
    inner = x * (GELU_C1 + GELU_C3 * (x * x))
    hx = 0.5 * x
    return hx + hx * jnp.tanh(inner)


def _cmul(ar, ai, br, bi):
    return ar * br - ai * bi, ar * bi + ai * br


def _mod_kernel(c_ref, w_ref, b_ref, o_ref):
    c = c_ref[...]
    cond = c * jax.nn.sigmoid(c)
    o_ref[...] = _dot(cond.astype(BF16), w_ref[...].astype(BF16)) + b_ref[...]


def _modulation(c, w_ada, b_ada):
    depth, d, n = w_ada.shape
    b = c.shape[0]
    rows = -(-b // SUBLANES) * SUBLANES
    cp = jnp.zeros((rows, d), F32).at[:b].set(c)
    tn = 1024 if n % 1024 == 0 else n
    out = pl.pallas_call(
        _mod_kernel,
        grid=(depth, n // tn),
        in_specs=[
            pl.BlockSpec((rows, d), lambda i, j: (0, 0)),
            pl.BlockSpec((None, d, tn), lambda i, j: (i, 0, j)),
            pl.BlockSpec((None, 1, tn), lambda i, j: (i, 0, j)),
        ],
        out_specs=pl.BlockSpec((None, rows, tn), lambda i, j: (i, 0, j)),
        out_shape=jax.ShapeDtypeStruct((depth, rows, n), F32),
        compiler_params=_params(("parallel", "parallel")),
        name="adaln_mod",
    )(cp, w_ada, b_ada.reshape(depth, 1, n))
    return out


class _Resident:
    def __init__(self, w_all, layer, rows, tm, seq):
        _, self.k, self.n = w_all.shape
        assert rows % tm == 0 and seq % tm == 0
        ck = self.k
        while ck * self.n * 4 > W_CHUNK_BYTES and ck % 32 == 0:
            ck //= 2
        self.ck, self.nc = ck, self.k // ck
        self.layer, self.tm, self.seq = layer, tm, seq
        self.grid = (self.nc + rows // tm,)
        self.scratch = pltpu.VMEM((self.k, self.n), BF16)

    def tile(self, m):
        return jnp.maximum(m - self.nc, 0)

    def weight(self):
        return pl.BlockSpec((None, self.ck, self.n),
                            lambda m: (self.layer, jnp.minimum(m, self.nc - 1), 0))

    def rows(self, width):
        return pl.BlockSpec((self.tm, width), lambda m: (self.tile(m), 0))

    def whole(self, shape):
        return pl.BlockSpec(shape, lambda m: (0,) * len(shape))

    def mod_vec(self, sel):
        table, layer, which = sel
        return pl.BlockSpec(
            (None, 1, None, 1, table.shape[-1]),
            lambda m: (layer, (self.tile(m) * self.tm) // self.seq, which, 0, 0))

    def gain_vec(self, sel):
        table, i, j = sel
        return pl.BlockSpec((None, None, 1, table.shape[-1]), lambda m: (i, j, 0, 0))

    def kernel(self, body, w_pos):
        nc, ck = self.nc, self.ck

        def kern(*refs):
            w_s = refs[-1]
            refs = list(refs[:-1])
            m = pl.program_id(0)

            @pl.when(m < nc)
            def _():
                w_s[pl.ds(pl.multiple_of(m * ck, ck), ck), :] = refs[w_pos][...].astype(BF16)

            @pl.when(m >= nc)
            def _():
                body(*refs[:w_pos], w_s, *refs[w_pos + 1:])

        return kern


def _ln_mm_kernel(x_ref, g_ref, sc_ref, sh_ref, w_ref, o_ref):
    h = _rms_mod(x_ref[...], g_ref[...], sc_ref[0], sh_ref[0]).astype(BF16)
    acc = _dot(h, w_ref[...])
    for q in range(o_ref.shape[0]):
        o_ref[q] = acc[:, q * LANES:(q + 1) * LANES]


def _ln_matmul_tiled(x2d, gain, scale, shift, w_all, layer, *, seq, tm, name):
    rows, d = x2d.shape
    n_out = w_all.shape[-1]
    assert n_out % LANES == 0
    plan = _Resident(w_all, layer, rows, tm, seq)
    return pl.pallas_call(
        plan.kernel(_ln_mm_kernel, 4),
        grid=plan.grid,
        in_specs=[plan.rows(d), plan.gain_vec(gain), plan.mod_vec(scale), plan.mod_vec(shift),
                  plan.weight()],
        out_specs=pl.BlockSpec((n_out // LANES, tm, LANES), lambda m: (0, plan.tile(m), 0)),
        out_shape=jax.ShapeDtypeStruct((n_out // LANES, rows, LANES), F32),
        scratch_shapes=[plan.scratch],
        compiler_params=_params(("arbitrary",)),
        name=name,
    )(x2d, gain[0], scale[0], shift[0], w_all)


def _make_lru_in_kernel(e, tm, tiles_per_seq, first_step):
    halo = SUBLANES

    def kern(a_ref, w_ref, cw_ref, cb_ref, o_ref, xpad):
        @pl.when((pl.program_id(0) - first_step) % tiles_per_seq == 0)
        def _():
            xpad[0:halo, :] = jnp.zeros((halo, e), F32)

        acc = _dot(a_ref[...], w_ref[...])
        o_ref[:, 0:e] = _gelu(acc[:, 0:e])
        xpad[halo:halo + tm, :] = acc[:, e:2 * e]
        nk = cw_ref.shape[0]
        xc = cb_ref[...] + cw_ref[nk - 1:nk, :] * xpad[halo:halo + tm, :]
        for k in range(nk - 1):
            off = halo - (nk - 1) + k
            xc = xc + cw_ref[k:k + 1, :] * xpad[off:off + tm, :]
        xpad[0:halo, :] = xpad[tm:tm + halo, :]
        o_ref[:, e:2 * e] = xc

    return kern


def _lru_in(a, w_all, layer, conv_w, conv_b, *, seq, tm, name):
    rows, k = a.shape
    e = w_all.shape[-1] // 2
    nk = conv_w.shape[0]
    assert nk - 1 <= SUBLANES
    plan = _Resident(w_all, layer, rows, tm, seq)
    return pl.pallas_call(
        plan.kernel(_make_lru_in_kernel(e, tm, seq // tm, plan.nc), 1),
        grid=plan.grid,
        in_specs=[plan.rows(k), plan.weight(), plan.whole((nk, e)), plan.whole((1, e))],
        out_specs=plan.rows(2 * e),
        out_shape=jax.ShapeDtypeStruct((rows, 2 * e), F32),
        scratch_shapes=[pltpu.VMEM((tm + SUBLANES, e), F32), plan.scratch],
        compiler_params=_params(("arbitrary",)),
        name=name,
    )(a, w_all, conv_w.reshape(nk, e), conv_b.reshape(1, e))


def _swiglu_kernel(h_ref, wg_ref, wu_ref, o_ref, wg_s, wu_s):
    @pl.when(pl.program_id(1) == 0)
    def _():
        wg_s[...] = wg_ref[...].astype(BF16)
        wu_s[...] = wu_ref[...].astype(BF16)

    h = h_ref[...]
    g = _dot(h, wg_s[...])
    o_ref[...] = (g * jax.nn.sigmoid(g) * _dot(h, wu_s[...])).astype(BF16)


def _swiglu_up(h, w_all, layer, *, tm, tn, name):
    rows, d = h.shape
    n_out = w_all.shape[-1] // 2
    assert rows % tm == 0 and n_out % tn == 0
    nt = n_out // tn
    return pl.pallas_call(
        _swiglu_kernel,
        grid=(nt, rows // tm),
        in_specs=[
            pl.BlockSpec((tm, d), lambda n, m: (m, 0)),
            pl.BlockSpec((None, d, tn), lambda n, m: (layer, 0, n)),
            pl.BlockSpec((None, d, tn), lambda n, m: (layer, 0, n + nt)),
        ],
        out_specs=pl.BlockSpec((tm, tn), lambda n, m: (m, n)),
        out_shape=jax.ShapeDtypeStruct((rows, n_out), BF16),
        scratch_shapes=[pltpu.VMEM((d, tn), BF16), pltpu.VMEM((d, tn), BF16)],
        compiler_params=_params(("parallel", "arbitrary")),
        name=name,
    )(h, w_all, w_all)


def _make_mm_res_kernel(glu, tiled_a, n_out, mode):
    def kern(*refs):
        a_ref, w_ref, res_ref, gate_ref = refs[:4]
        if tiled_a:
            a = jnp.concatenate([a_ref[q] for q in range(a_ref.shape[0])], axis=1)
        else:
            a = a_ref[...]
        acc = _dot(a, w_ref[...])
        if glu:
            acc = acc[:, :n_out] * jax.nn.sigmoid(acc[:, n_out:])
        x = res_ref[...] + gate_ref[0] * acc
        if mode == "final":
            g_ref, o_ref = refs[4:]
            ms = jnp.mean(x * x, axis=-1, keepdims=True)
            o_ref[...] = x * lax.rsqrt(ms + EPS) * g_ref[...]
        elif mode == "x+h":
            g_ref, sc_ref, sh_ref, x_out, h_out = refs[4:]
            x_out[...] = x
            h_out[...] = _rms_mod(x, g_ref[...], sc_ref[0], sh_ref[0]).astype(BF16)
        else:
            (x_out,) = refs[4:]
            x_out[...] = x

    return kern


def _matmul_residual(a, w_all, layer, res2d, gate, *, seq, tm, glu=False, tiled_a=False,
                     next_norm=None, final_gain=None, name):
    rows, n_out = res2d.shape
    k = w_all.shape[1]
    plan = _Resident(w_all, layer, rows, tm, seq)
    if tiled_a:
        a_spec = pl.BlockSpec((k // LANES, tm, LANES), lambda m: (0, plan.tile(m), 0))
    else:
        a_spec = plan.rows(k)
    row_spec = plan.rows(n_out)
    in_specs = [a_spec, plan.weight(), row_spec, plan.mod_vec(gate)]
    args = [a, w_all, res2d, gate[0]]
    x_shape = jax.ShapeDtypeStruct((rows, n_out), F32)
    if final_gain is not None:
        mode = "final"
        in_specs.append(plan.whole((1, n_out)))
        args.append(final_gain.reshape(1, n_out))
        out_specs, out_shape = row_spec, x_shape
    elif next_norm is not None:
        mode = "x+h"
        gain, scale, shift = next_norm
        in_specs += [plan.gain_vec(gain), plan.mod_vec(scale), plan.mod_vec(shift)]
        args += [gain[0], scale[0], shift[0]]
        out_specs = [row_spec, row_spec]
        out_shape = [x_shape, jax.ShapeDtypeStruct((rows, n_out), BF16)]
    else:
        mode = "x"
        out_specs, out_shape = [row_spec], [x_shape]
    out = pl.pallas_call(
        plan.kernel(_make_mm_res_kernel(glu, tiled_a, n_out, mode), 1),
        grid=plan.grid,
        in_specs=in_specs,
        out_specs=out_specs,
        out_shape=out_shape,
        scratch_shapes=[plan.scratch],
        compiler_params=_params(("arbitrary",)),
        name=name,
    )(*args)
    if mode == "final":
        return out
    return (out[0], out[1]) if mode == "x+h" else (out[0], None)


def _make_s5_prep_kernel(tg, p):
    tc = S5_CHUNK
    half = tg * p

    def kern(lr_ref, li_ref, ldt_ref, br_ref, bi_ref, cr_ref, ci_ref,
             wio_ref, wis_ref, wso_ref, tab_ref):
        lr = lr_ref[...]
        li = li_ref[...]
        dt = jnp.exp(ldt_ref[...])
        mag = jnp.exp(lr * dt)
        ab_re = mag * jnp.cos(li * dt)
        ab_im = mag * jnp.sin(li * dt)
        nr, ni = ab_re - 1.0, ab_im
        den = lr * lr + li * li
        f_re = (nr * lr + ni * li) / den
        f_im = (ni * lr - nr * li) / den
        br, bi = br_ref[...], bi_ref[...]
        gc = br.shape[0]
        bb_re = f_re * br - f_im * bi
        bb_im = f_re * bi + f_im * br
        cr, ci = cr_ref[...], ci_ref[...]
        gid = lax.broadcasted_iota(jnp.int32, (1, half), 1) // p

        def expand(v):
            return jnp.concatenate([jnp.where(gid == g, v, 0.0) for g in range(tg)], axis=0)

        pows = [(jnp.ones_like(ab_re), jnp.zeros_like(ab_im))]
        for _ in range(tc):
            pows.append(_cmul(pows[-1][0], pows[-1][1], ab_re, ab_im))

        for t in range(tc):
            pr, pi = pows[tc - 1 - t]
            wr, wi = _cmul(pr, pi, bb_re, bb_im)
            rows = slice(t * gc, (t + 1) * gc)
            wis_ref[rows, 0:half] = wr.astype(BF16)
            wis_ref[rows, half:2 * half] = wi.astype(BF16)

        def split(v):
            hi = v.astype(BF16)
            return hi, (v - hi.astype(F32)).astype(BF16)

        def nt_dot(a, b):
            return lax.dot_general(a, b, (((1,), (1,)), ((), ())), preferred_element_type=F32)

        b_hi, b_lo = split(jnp.concatenate([expand(bb_re), expand(bb_im)], axis=1))
        q_all = []
        for tau in range(tc + 1):
            pr, pi = pows[tau]
            qr, qi = _cmul(cr, ci, pr, pi)
            er, ei = expand(qr), expand(-qi)
            if tau >= 1:
                cols = slice((tau - 1) * LANES, tau * LANES)
                wso_ref[0:half, cols] = er.T.astype(BF16)
                wso_ref[half:2 * half, cols] = ei.T.astype(BF16)
            if tau < tc:
                q_all.append(jnp.concatenate([er, ei], axis=1))
        q_hi, q_lo = split(jnp.concatenate(q_all, axis=0))
        kt = nt_dot(q_hi, b_hi) + (nt_dot(q_lo, b_hi) + nt_dot(q_hi, b_lo))
        for tau in range(tc):
            wio_ref[tau] = kt[tau * LANES:(tau + 1) * LANES, :].T.astype(BF16)

        lam = [(jnp.ones_like(ab_re), jnp.zeros_like(ab_im))]
        for _ in range(SUBLANES):
            lam.append(_cmul(lam[-1][0], lam[-1][1], pows[tc][0], pows[tc][1]))
        zrow = jnp.zeros_like(ab_re)
        groups = [[lam[r] for r in range(SUBLANES)]]
        for d in (1, 2, 4):
            groups.append([lam[d] if r >= d else (zrow, zrow) for r in range(SUBLANES)])
        groups.append([lam[SUBLANES]] * SUBLANES)
        for gi, grp in enumerate(groups):
            rows = slice(gi * SUBLANES, (gi + 1) * SUBLANES)
            tab_ref[rows, 0:half] = jnp.concatenate([v[0] for v in grp], axis=0)
            tab_ref[rows, half:2 * half] = jnp.concatenate([v[1] for v in grp], axis=0)

    return kern


def _s5_operators(lam_re, lam_im, log_dt, b_re, b_im, c_re, c_im):
    g, p = lam_re.shape
    gc = b_re.shape[-1]
    gp = g * p
    tg = S5_TILE_GROUPS
    assert gc * tg == LANES and g % tg == 0
    nj = g // tg
    half = tg * p
    kdim = S5_CHUNK * LANES
    lr = lam_re.reshape(1, gp)
    li = lam_im.reshape(1, gp)
    ldt = jnp.repeat(log_dt, p).reshape(1, gp)
    brt = b_re.transpose(2, 0, 1).reshape(gc, gp)
    bit = b_im.transpose(2, 0, 1).reshape(gc, gp)
    crt = c_re.transpose(1, 0, 2).reshape(gc, gp)
    cit = c_im.transpose(1, 0, 2).reshape(gc, gp)
    row = pl.BlockSpec((1, half), lambda j: (0, j))
    mat = pl.BlockSpec((gc, half), lambda j: (0, j))
    return pl.pallas_call(
        _make_s5_prep_kernel(tg, p),
        grid=(nj,),
        in_specs=[row, row, row, mat, mat, mat, mat],
        out_specs=[
            pl.BlockSpec((None, S5_CHUNK, LANES, LANES), lambda j: (j, 0, 0, 0)),
            pl.BlockSpec((None, S5_CHUNK * gc, 2 * half), lambda j: (j, 0, 0)),
            pl.BlockSpec((None, 2 * half, kdim), lambda j: (j, 0, 0)),
            pl.BlockSpec((None, 5 * SUBLANES, 2 * half), lambda j: (j, 0, 0)),
        ],
        out_shape=[
            jax.ShapeDtypeStruct((nj, S5_CHUNK, LANES, LANES), BF16),
            jax.ShapeDtypeStruct((nj, S5_CHUNK * gc, 2 * half), BF16),
            jax.ShapeDtypeStruct((nj, 2 * half, kdim), BF16),
            jax.ShapeDtypeStruct((nj, 5 * SUBLANES, 2 * half), F32),
        ],
        compiler_params=_params(("parallel",)),
        name="s5_prep",
    )(lr, li, ldt, brt, bit, crt, cit)


def _make_s5_kernel(nb, chunks_per_batch, half, gc):
    tc = S5_CHUNK
    nchunk = nb * chunks_per_batch
    tiles_per_batch = chunks_per_batch // SUBLANES
    ntile = nchunk // SUBLANES

    def kern(u_ref, wio_ref, wis_ref, wso_ref, tab_ref, d_ref, y_ref, s_scr, y_scr, wio_s, wis_s):
        tg = LANES // gc
        gid = (lax.broadcasted_iota(jnp.int32, (1, 2 * half), 1) % half) // (half // tg)
        zero_rows = jnp.zeros((gc, 2 * half), BF16)
        for t in range(tc):
            blk = wis_ref[t * gc:(t + 1) * gc, :]
            for g in range(tg):
                r0 = (t * tg + g) * gc
                wis_s[r0:r0 + gc, :] = jnp.where(gid == g, blk, zero_rows)
        zero_blk = jnp.zeros((LANES, LANES), BF16)
        for t in range(tc):
            for t2 in range(tc):
                wio_s[t * LANES:(t + 1) * LANES, t2 * LANES:(t2 + 1) * LANES] = (
                    wio_ref[t2 - t] if t2 >= t else zero_blk)

        lhs = jnp.concatenate(
            [u_ref[pl.ds(t, nchunk, stride=tc), :].astype(BF16) for t in range(tc)], axis=1)
        s_scr[...] = _dot(lhs, wis_s[...])

        tab = tab_ref[...]
        pw_r, pw_i = tab[0:8, :half], tab[0:8, half:]
        steps = [(d, tab[8 * i:8 * i + 8, :half], tab[8 * i:8 * i + 8, half:])
                 for i, d in ((1, 1), (2, 2), (3, 4))]
        l8_r, l8_i = tab[32:33, :half], tab[32:33, half:]
        first_row = lax.broadcasted_iota(jnp.int32, (SUBLANES, half), 0) == 0

        for m in range(ntile):
            rows = slice(m * SUBLANES, (m + 1) * SUBLANES)
            s = s_scr[rows, :]
            er, ei = s[:, :half], s[:, half:]
            for d, mr, mi in steps:
                rr = pltpu.roll(er, d, axis=0)
                ri = pltpu.roll(ei, d, axis=0)
                er, ei = er + (mr * rr - mi * ri), ei + (mr * ri + mi * rr)
            hr = jnp.where(first_row, 0.0, pltpu.roll(er, 1, axis=0))
            hi = jnp.where(first_row, 0.0, pltpu.roll(ei, 1, axis=0))
            if m % tiles_per_batch == 0:
                cr, ci = er[7:8], ei[7:8]
            else:
                hr = hr + (pw_r * cr - pw_i * ci)
                hi = hi + (pw_r * ci + pw_i * cr)
                cr, ci = er[7:8] + (l8_r * cr - l8_i * ci), ei[7:8] + (l8_r * ci + l8_i * cr)
            s_scr[rows, :] = jnp.concatenate([hr, hi], axis=1)

        y_io = jnp.concatenate(
            [_dot(lhs[:, :(p + 1) * MXU_N], wio_s[0:(p + 1) * MXU_N, p * MXU_N:(p + 1) * MXU_N])
             for p in range(tc * LANES // MXU_N)], axis=1)
        y = y_io + _dot(s_scr[...].astype(BF16), wso_ref[...])
        for t in range(tc):
            y_scr[pl.ds(t, nchunk, stride=tc), :] = y[:, t * LANES:(t + 1) * LANES]
        y_ref[...] = _gelu(y_scr[...] + d_ref[...] * u_ref[...]).astype(BF16)

    return kern


def _s5_core(u3, ops, d_skip, *, nb, seq):
    w_io, w_is, w_so, tab = ops
    nj, rows, _ = u3.shape
    kdim = S5_CHUNK * LANES
    sdim = w_is.shape[-1]
    chunks_per_batch = seq // S5_CHUNK
    gc = w_is.shape[1] // S5_CHUNK
    assert rows == nb * seq and chunks_per_batch % SUBLANES == 0
    tok = pl.BlockSpec((None, rows, LANES), lambda j: (j, 0, 0))
    return pl.pallas_call(
        _make_s5_kernel(nb, chunks_per_batch, sdim // 2, gc),
        grid=(nj,),
        in_specs=[
            tok,
            pl.BlockSpec((None, S5_CHUNK, LANES, LANES), lambda j: (j, 0, 0, 0)),
            pl.BlockSpec((None, S5_CHUNK * gc, sdim), lambda j: (j, 0, 0)),
            pl.BlockSpec((None, sdim, kdim), lambda j: (j, 0, 0)),
            pl.BlockSpec((None, 5 * SUBLANES, sdim), lambda j: (j, 0, 0)),
            pl.BlockSpec((1, LANES), lambda j: (0, j)),
        ],
        out_specs=tok,
        out_shape=jax.ShapeDtypeStruct((nj, rows, LANES), BF16),
        scratch_shapes=[pltpu.VMEM((rows // S5_CHUNK, sdim), F32), pltpu.VMEM((rows, LANES), F32),
                        pltpu.VMEM((kdim, kdim), BF16), pltpu.VMEM((kdim, sdim), BF16)],
        compiler_params=_params(("parallel",)),
        name="s5_core",
    )(u3, w_io, w_is, w_so, tab, d_skip.reshape(1, nj * LANES))


def _band_windows(width, blk, tn):
    starts, ends = [], []
    for n in range(width // tn):
        h_lo = (n * tn) // blk
        h_hi = (n * tn + tn - 1) // blk
        starts.append((h_lo * blk) // LANES)
        ends.append(-(-((h_hi + 1) * blk) // LANES))
    kw = max(e - s for s, e in zip(starts, ends))
    total = width // LANES
    starts = [min(s, total - kw) for s in starts]
    return starts, kw * LANES


def _banded(w, starts, kw, tn):
    blk = w.shape[-1]
    tiles = []
    for n, s in enumerate(starts):
        pieces = []
        for h in range((n * tn) // blk, (n * tn + tn - 1) // blk + 1):
            j0 = max(0, n * tn - h * blk)
            j1 = min(blk, (n + 1) * tn - h * blk)
            r0 = h * blk - s * LANES
            pieces.append(jnp.pad(w[:, h, :, j0:j1], ((0, 0), (r0, kw - r0 - blk), (0, 0))))
        tiles.append(jnp.concatenate(pieces, axis=2))
    return jnp.stack(tiles, axis=1).astype(BF16)


def _make_lru_kernel(e, tm, tn, starts, kw, tiles_per_seq):
    ntile = e // tn
    nrow = tm // SUBLANES

    def kern(gx_ref, w_ref, vec_ref, y_ref, hcar):
        @pl.when(pl.program_id(0) % tiles_per_seq == 0)
        def _():
            hcar[...] = jnp.zeros((1, e), F32)

        row = lax.broadcasted_iota(jnp.int32, (SUBLANES, tn), 0)
        masks = [row >= d for d in (1, 2, 4)]
        neg = -vec_ref[2:3, :]
        softplus = jnp.maximum(neg, 0.0) + jnp.log1p(jnp.exp(-jnp.abs(neg)))
        rate = (-0.5 * LRU_C) * softplus
        hb_rg = 0.5 * vec_ref[0:1, :]
        hb_ig = 0.5 * vec_ref[1:2, :]

        for n in range(ntile):
            cols = slice(n * tn, (n + 1) * tn)
            k0 = e + starts[n] * LANES
            win = gx_ref[:, k0:k0 + kw].astype(BF16)
            tr = jnp.tanh(0.5 * _dot(win, w_ref[0, n]) + hb_rg[:, cols])
            ig = 0.5 * jnp.tanh(0.5 * _dot(win, w_ref[1, n]) + hb_ig[:, cols]) + 0.5
            a = jnp.exp(rate[:, cols] * tr + rate[:, cols])
            om = 1.0 - a * a
            mult = jnp.where(om > 0.0, om * lax.rsqrt(om), 0.0)
            b = mult * (ig * gx_ref[:, e + n * tn:e + (n + 1) * tn])

            c = hcar[:, cols]
            hs = []
            for m in range(nrow):
                rows = slice(m * SUBLANES, (m + 1) * SUBLANES)
                av, bv = a[rows], b[rows]
                for d, keep in zip((1, 2, 4), masks):
                    bv = bv + jnp.where(keep, av * pltpu.roll(bv, d, axis=0), 0.0)
                    av = jnp.where(keep, av * pltpu.roll(av, d, axis=0), av)
                h = bv + av * c
                c = h[SUBLANES - 1:SUBLANES]
                hs.append(h)
            hcar[:, cols] = c
            y_ref[:, cols] = (jnp.concatenate(hs, axis=0) * gx_ref[:, cols]).astype(BF16)

    return kern


def _lru_core(gx, w_rg, b_rg, w_ig, b_ig, lam, *, seq, tm):
    rows, e2 = gx.shape
    e = e2 // 2
    blk = w_rg.shape[1]
    tn = MXU_N
    assert e % tn == 0 and seq % tm == 0 and tm % SUBLANES == 0
    starts, kw = _band_windows(e, blk, tn)
    gates = _banded(jnp.stack([w_rg, w_ig]), starts, kw, tn)
    vecs = jnp.stack([b_rg, b_ig, lam])
    ntile = e // tn
    return pl.pallas_call(
        _make_lru_kernel(e, tm, tn, starts, kw, seq // tm),
        grid=(rows // tm,),
        in_specs=[
            pl.BlockSpec((tm, e2), lambda m: (m, 0)),
            pl.BlockSpec((2, ntile, kw, tn), lambda m: (0, 0, 0, 0)),
            pl.BlockSpec((3, e), lambda m: (0, 0)),
        ],
        out_specs=pl.BlockSpec((tm, e), lambda m: (m, 0)),
        out_shape=jax.ShapeDtypeStruct((rows, e), BF16),
        scratch_shapes=[pltpu.VMEM((1, e), F32)],
        compiler_params=_params(("arbitrary",)),
        name="lru_core",
    )(gx, gates, vecs)


def _tile(n, pref):
    t = min(n, pref)
    while n % t:
        t //= 2
    return t


def kernel(x, c, norm_g, w_ada, b_ada, s5_w_in, s5_lam_re, s5_lam_im, s5_log_dt, s5_b_re, s5_b_im, s5_c_re, s5_c_im, s5_d, s5_w_glu, lru_w_in, lru_conv_w, lru_conv_b, lru_w_rg, lru_b_rg, lru_w_ig, lru_b_ig, lru_lam, lru_w_out, ffn_w_gu, ffn_w_down, final_g):
    bsz, seq, d = x.shape
    depth = w_ada.shape[0]
    rows = bsz * seq
    x2 = x.reshape(rows, d)
    mods = _modulation(c, w_ada, b_ada)
    mods = mods.reshape(depth, mods.shape[1], 6, 1, d)
    gains = norm_g.reshape(depth, 2, 1, d)
    tm_in = _tile(seq, 512)
    tm_res = _tile(seq, 256)
    tm_up = _tile(rows, 1024)
    hidden = ffn_w_down.shape[1]

    h = None
    for i in range(depth):
        sh1, sc1, g1, sh2, sc2, g2 = [(mods, i, q) for q in range(6)]
        ffn_norm = ((gains, i, 1), sc2, sh2)
        j = i // 2
        if i % 2 == 0:
            u3 = _ln_matmul_tiled(x2, (gains, i, 0), sc1, sh1, s5_w_in, j, seq=seq, tm=tm_in,
                                  name="s5_in")
            ops = _s5_operators(s5_lam_re[j], s5_lam_im[j], s5_log_dt[j], s5_b_re[j], s5_b_im[j],
                                s5_c_re[j], s5_c_im[j])
            y3 = _s5_core(u3, ops, s5_d[j], nb=bsz, seq=seq)
            x2, h = _matmul_residual(y3, s5_w_glu, j, x2, g1, seq=seq, tm=tm_res, glu=True,
                                     tiled_a=True, next_norm=ffn_norm, name="s5_out")
        else:
            gx = _lru_in(h, lru_w_in, j, lru_conv_w[j], lru_conv_b[j], seq=seq, tm=tm_res,
                         name="lru_in")
            y = _lru_core(gx, lru_w_rg[j], lru_b_rg[j], lru_w_ig[j], lru_b_ig[j], lru_lam[j],
                          seq=seq, tm=tm_res)
            x2, h = _matmul_residual(y, lru_w_out, j, x2, g1, seq=seq, tm=tm_res,
                                     next_norm=ffn_norm, name="lru_out")
        act = _swiglu_up(h, ffn_w_gu, i, tm=tm_up, tn=_tile(hidden, 512), name="ffn_up")
        if i + 1 == depth:
            out = _matmul_residual(act, ffn_w_down, i, x2, g2, seq=seq, tm=tm_res, final_gain=final_g,
                                   name="ffn_down")
            return out.reshape(bsz, seq, d)
        nxt = None
        if (i + 1) % 2 == 1:
            nxt = ((gains, i + 1, 0), (mods, i + 1, 1), (mods, i + 1, 0))
        x2, h = _matmul_residual(act, ffn_w_down, i, x2, g2, seq=seq, tm=tm_res, next_norm=nxt,
                                 name="ffn_down")
```

```python
import jax
import jax.numpy as jnp
from jax import lax
from jax.experimental import pallas as pl
from jax.experimental.pallas import tpu as pltpu

F32 = jnp.float32
BF16 = jnp.bfloat16

EPS = 1e-6
LRU_C = 8.0
GELU_C1 = 0.7978845608028654
GELU_C3 = 0.7978845608028654 * 0.044715
LANES = 128
SUBLANES = 8
HALF_TILE = SUBLANES // 2
MXU_N = 256
S5_CHUNK = 8
S5_TILE_GROUPS = 8
VMEM_LIMIT = 56 * 1024 * 1024
W_CHUNK_BYTES = 8 * 1024 * 1024


def _params(semantics, vmem=VMEM_LIMIT):
    return pltpu.CompilerParams(dimension_semantics=semantics, vmem_limit_bytes=vmem)


def _dot(a, b):
    return jnp.dot(a, b, preferred_element_type=F32)


def _rms_mod(x, gain, scale, shift):
    ms = jnp.mean(x * x, axis=-1, keepdims=True)
    y = x * lax.rsqrt(ms + EPS) * gain
    return y * (1.0 + scale) + shift


def _gelu(x):
    inner = x * (GELU_C1 + GELU_C3 * (x * x))
    hx = 0.5 * x
    return hx + hx * jnp.tanh(inner)


def _cmul(ar, ai, br, bi):
    return ar * br - ai * bi, ar * bi + ai * br


def _mod_kernel(c_ref, w_ref, b_ref, o_ref):
    c = c_ref[...]
    cond = c * jax.nn.sigmoid(c)
    o_ref[...] = _dot(cond.astype(BF16), w_ref[...].astype(BF16)) + b_ref[...]


def _modulation(c, w_ada, b_ada):
    depth, d, n = w_ada.shape
    b = c.shape[0]
    rows = -(-b // SUBLANES) * SUBLANES
    cp = jnp.zeros((rows, d), F32).at[:b].set(c)
    tn = 1024 if n % 1024 == 0 else n
    out = pl.pallas_call(
        _mod_kernel,
        grid=(depth, n // tn),
        in_specs=[
            pl.BlockSpec((rows, d), lambda i, j: (0, 0)),
            pl.BlockSpec((None, d, tn), lambda i, j: (i, 0, j)),
            pl.BlockSpec((None, 1, tn), lambda i, j: (i, 0, j)),
        ],
        out_specs=pl.BlockSpec((None, rows, tn), lambda i, j: (i, 0, j)),
        out_shape=jax.ShapeDtypeStruct((depth, rows, n), F32),
        compiler_params=_params(("parallel", "parallel")),
        name="adaln_mod",
    )(cp, w_ada, b_ada.reshape(depth, 1, n))
    return out


class _Resident:
    def __init__(self, w_all, layer, rows, tm, seq):
        _, self.k, self.n = w_all.shape
        assert rows % tm == 0 and seq % tm == 0
        ck = self.k
        while ck * self.n * 4 > W_CHUNK_BYTES and ck % 32 == 0:
            ck //= 2
        self.ck, self.nc = ck, self.k // ck
        self.layer, self.tm, self.seq = layer, tm, seq
        self.grid = (self.nc + rows // tm,)
        self.scratch = pltpu.VMEM((self.k, self.n), BF16)

    def tile(self, m):
        return jnp.maximum(m - self.nc, 0)

    def weight(self):
        return pl.BlockSpec((None, self.ck, self.n),
                            lambda m: (self.layer, jnp.minimum(m, self.nc - 1), 0))

    def rows(self, width):
        return pl.BlockSpec((self.tm, width), lambda m: (self.tile(m), 0))

    def whole(self, shape):
        return pl.BlockSpec(shape, lambda m: (0,) * len(shape))

    def mod_vec(self, sel):
        table, layer, which = sel
        return pl.BlockSpec(
            (None, 1, None, 1, table.shape[-1]),
            lambda m: (layer, (self.tile(m) * self.tm) // self.seq, which, 0, 0))

    def gain_vec(self, sel):
        table, i, j = sel
        return pl.BlockSpec((None, None, 1, table.shape[-1]), lambda m: (i, j, 0, 0))

    def kernel(self, body, w_pos):
        nc, ck = self.nc, self.ck

        def kern(*refs):
            w_s = refs[-1]
            refs = list(refs[:-1])
            m = pl.program_id(0)

            @pl.when(m < nc)
            def _():
                w_s[pl.ds(pl.multiple_of(m * ck, ck), ck), :] = refs[w_pos][...].astype(BF16)

            @pl.when(m >= nc)
            def _():
                body(*refs[:w_pos], w_s, *refs[w_pos + 1:])

        return kern


def _ln_mm_kernel(x_ref, g_ref, sc_ref, sh_ref, w_ref, o_ref):
    h = _rms_mod(x_ref[...], g_ref[...], sc_ref[0], sh_ref[0]).astype(BF16)
    acc = _dot(h, w_ref[...])
    for q in range(o_ref.shape[0]):
        o_ref[q] = acc[:, q * LANES:(q + 1) * LANES]


def _ln_matmul_tiled(x2d, gain, scale, shift, w_all, layer, *, seq, tm, name):
    rows, d = x2d.shape
    n_out = w_all.shape[-1]
    assert n_out % LANES == 0
    plan = _Resident(w_all, layer, rows, tm, seq)
    return pl.pallas_call(
        plan.kernel(_ln_mm_kernel, 4),
        grid=plan.grid,
        in_specs=[plan.rows(d), plan.gain_vec(gain), plan.mod_vec(scale), plan.mod_vec(shift),
                  plan.weight()],
        out_specs=pl.BlockSpec((n_out // LANES, tm, LANES), lambda m: (0, plan.tile(m), 0)),
        out_shape=jax.ShapeDtypeStruct((n_out // LANES, rows, LANES), F32),
        scratch_shapes=[plan.scratch],
        compiler_params=_params(("arbitrary",)),
        name=name,
    )(x2d, gain[0], scale[0], shift[0], w_all)


def _make_lru_in_kernel(e, tm, tiles_per_seq, first_step):
    halo = SUBLANES

    def kern(a_ref, w_ref, cw_ref, cb_ref, o_ref, xpad):
        @pl.when((pl.program_id(0) - first_step) % tiles_per_seq == 0)
        def _():
            xpad[0:halo, :] = jnp.zeros((halo, e), F32)

        acc = _dot(a_ref[...], w_ref[...])
        o_ref[:, 0:e] = _gelu(acc[:, 0:e])
        xpad[halo:halo + tm, :] = acc[:, e:2 * e]
        nk = cw_ref.shape[0]
        xc = cb_ref[...] + cw_ref[nk - 1:nk, :] * xpad[halo:halo + tm, :]
        for k in range(nk - 1):
            off = halo - (nk - 1) + k
            xc = xc + cw_ref[k:k + 1, :] * xpad[off:off + tm, :]
        xpad[0:halo, :] = xpad[tm:tm + halo, :]
        o_ref[:, e:2 * e] = xc

    return kern


def _lru_in(a, w_all, layer, conv_w, conv_b, *, seq, tm, name):
    rows, k = a.shape
    e = w_all.shape[-1] // 2
    nk = conv_w.shape[0]
    assert nk - 1 <= SUBLANES
    plan = _Resident(w_all, layer, rows, tm, seq)
    return pl.pallas_call(
        plan.kernel(_make_lru_in_kernel(e, tm, seq // tm, plan.nc), 1),
        grid=plan.grid,
        in_specs=[plan.rows(k), plan.weight(), plan.whole((nk, e)), plan.whole((1, e))],
        out_specs=plan.rows(2 * e),
        out_shape=jax.ShapeDtypeStruct((rows, 2 * e), F32),
        scratch_shapes=[pltpu.VMEM((tm + SUBLANES, e), F32), plan.scratch],
        compiler_params=_params(("arbitrary",)),
        name=name,
    )(a, w_all, conv_w.reshape(nk, e), conv_b.reshape(1, e))


def _swiglu_kernel(h_ref, wg_ref, wu_ref, o_ref, wg_s, wu_s):
    @pl.when(pl.program_id(1) == 0)
    def _():
        wg_s[...] = wg_ref[...].astype(BF16)
        wu_s[...] = wu_ref[...].astype(BF16)

    h = h_ref[...]
    g = _dot(h, wg_s[...])
    o_ref[...] = (g * jax.nn.sigmoid(g) * _dot(h, wu_s[...])).astype(BF16)


def _swiglu_up(h, w_all, layer, *, tm, tn, name):
    rows, d = h.shape
    n_out = w_all.shape[-1] // 2
    assert rows % tm == 0 and n_out % tn == 0
    nt = n_out // tn
    return pl.pallas_call(
        _swiglu_kernel,
        grid=(nt, rows // tm),
        in_specs=[
            pl.BlockSpec((tm, d), lambda n, m: (m, 0)),
            pl.BlockSpec((None, d, tn), lambda n, m: (layer, 0, n)),
            pl.BlockSpec((None, d, tn), lambda n, m: (layer, 0, n + nt)),
        ],
        out_specs=pl.BlockSpec((tm, tn), lambda n, m: (m, n)),
        out_shape=jax.ShapeDtypeStruct((rows, n_out), BF16),
        scratch_shapes=[pltpu.VMEM((d, tn), BF16), pltpu.VMEM((d, tn), BF16)],
        compiler_params=_params(("parallel", "arbitrary")),
        name=name,
    )(h, w_all, w_all)


def _make_mm_res_kernel(glu, tiled_a, n_out, mode):
    def kern(*refs):
        a_ref, w_ref, res_ref, gate_ref = refs[:4]
        if tiled_a:
            a = jnp.concatenate([a_ref[q] for q in range(a_ref.shape[0])], axis=1)
        else:
            a = a_ref[...]
        acc = _dot(a, w_ref[...])
        if glu:
            acc = acc[:, :n_out] * jax.nn.sigmoid(acc[:, n_out:])
        x = res_ref[...] + gate_ref[0] * acc
        if mode == "final":
            g_ref, o_ref = refs[4:]
            ms = jnp.mean(x * x, axis=-1, keepdims=True)
            o_ref[...] = x * lax.rsqrt(ms + EPS) * g_ref[...]
        elif mode == "x+h":
            g_ref, sc_ref, sh_ref, x_out, h_out = refs[4:]
            x_out[...] = x
            h_out[...] = _rms_mod(x, g_ref[...], sc_ref[0], sh_ref[0]).astype(BF16)
        else:
            (x_out,) = refs[4:]
            x_out[...] = x

    return kern


def _matmul_residual(a, w_all, layer, res2d, gate, *, seq, tm, glu=False, tiled_a=False,
                     next_norm=None, final_gain=None, name):
    rows, n_out = res2d.shape
    k = w_all.shape[1]
    plan = _Resident(w_all, layer, rows, tm, seq)
    if tiled_a:
        a_spec = pl.BlockSpec((k // LANES, tm, LANES), lambda m: (0, plan.tile(m), 0))
    else:
        a_spec = plan.rows(k)
    row_spec = plan.rows(n_out)
    in_specs = [a_spec, plan.weight(), row_spec, plan.mod_vec(gate)]
    args = [a, w_all, res2d, gate[0]]
    x_shape = jax.ShapeDtypeStruct((rows, n_out), F32)
    if final_gain is not None:
        mode = "final"
        in_specs.append(plan.whole((1, n_out)))
        args.append(final_gain.reshape(1, n_out))
        out_specs, out_shape = row_spec, x_shape
    elif next_norm is not None:
        mode = "x+h"
        gain, scale, shift = next_norm
        in_specs += [plan.gain_vec(gain), plan.mod_vec(scale), plan.mod_vec(shift)]
        args += [gain[0], scale[0], shift[0]]
        out_specs = [row_spec, row_spec]
        out_shape = [x_shape, jax.ShapeDtypeStruct((rows, n_out), BF16)]
    else:
        mode = "x"
        out_specs, out_shape = [row_spec], [x_shape]
    out = pl.pallas_call(
        plan.kernel(_make_mm_res_kernel(glu, tiled_a, n_out, mode), 1),
        grid=plan.grid,
        in_specs=in_specs,
        out_specs=out_specs,
        out_shape=out_shape,
        scratch_shapes=[plan.scratch],
        compiler_params=_params(("arbitrary",)),
        name=name,
    )(*args)
    if mode == "final":
        return out
    return (out[0], out[1]) if mode == "x+h" else (out[0], None)


def _make_s5_prep_kernel(tg, p):
    tc = S5_CHUNK
    half = tg * p

    def kern(lr_ref, li_ref, ldt_ref, br_ref, bi_ref, cr_ref, ci_ref,
             wio_ref, wis_ref, wso_ref, tab_ref):
        lr = lr_ref[...]
        li = li_ref[...]
        dt = jnp.exp(ldt_ref[...])
        mag = jnp.exp(lr * dt)
        ab_re = mag * jnp.cos(li * dt)
        ab_im = mag * jnp.sin(li * dt)
        nr, ni = ab_re - 1.0, ab_im
        den = lr * lr + li * li
        f_re = (nr * lr + ni * li) / den
        f_im = (ni * lr - nr * li) / den
        br, bi = br_ref[...], bi_ref[...]
        gc = br.shape[0]
        bb_re = f_re * br - f_im * bi
        bb_im = f_re * bi + f_im * br
        cr, ci = cr_ref[...], ci_ref[...]
        gid = lax.broadcasted_iota(jnp.int32, (1, half), 1) // p

        def expand(v):
            return jnp.concatenate([jnp.where(gid == g, v, 0.0) for g in range(tg)], axis=0)

        pows = [(jnp.ones_like(ab_re), jnp.zeros_like(ab_im))]
        for _ in range(tc):
            pows.append(_cmul(pows[-1][0], pows[-1][1], ab_re, ab_im))

        for t in range(tc):
            pr, pi = pows[tc - 1 - t]
            wr, wi = _cmul(pr, pi, bb_re, bb_im)
            rows = slice(t * gc, (t + 1) * gc)
            wis_ref[rows, 0:half] = wr.astype(BF16)
            wis_ref[rows, half:2 * half] = wi.astype(BF16)

        def split(v):
            hi = v.astype(BF16)
            return hi, (v - hi.astype(F32)).astype(BF16)

        def nt_dot(a, b):
            return lax.dot_general(a, b, (((1,), (1,)), ((), ())), preferred_element_type=F32)

        b_hi, b_lo = split(jnp.concatenate([expand(bb_re), expand(bb_im)], axis=1))
        q_all = []
        for tau in range(tc + 1):
            pr, pi = pows[tau]
            qr, qi = _cmul(cr, ci, pr, pi)
            er, ei = expand(qr), expand(-qi)
            if tau >= 1:
                cols = slice((tau - 1) * LANES, tau * LANES)
                wso_ref[0:half, cols] = er.T.astype(BF16)
                wso_ref[half:2 * half, cols] = ei.T.astype(BF16)
            if tau < tc:
                q_all.append(jnp.concatenate([er, ei], axis=1))
        q_hi, q_lo = split(jnp.concatenate(q_all, axis=0))
        kt = nt_dot(q_hi, b_hi) + (nt_dot(q_lo, b_hi) + nt_dot(q_hi, b_lo))
        for tau in range(tc):
            wio_ref[tau] = kt[tau * LANES:(tau + 1) * LANES, :].T.astype(BF16)

        lam = [(jnp.ones_like(ab_re), jnp.zeros_like(ab_im))]
        for _ in range(SUBLANES):
            lam.append(_cmul(lam[-1][0], lam[-1][1], pows[tc][0], pows[tc][1]))
        zrow = jnp.zeros_like(ab_re)
        groups = [[lam[r] for r in range(SUBLANES)]]
        for d in (1, 2, 4):
            groups.append([lam[d] if r >= d else (zrow, zrow) for r in range(SUBLANES)])
        groups.append([lam[SUBLANES]] * SUBLANES)
        for gi, grp in enumerate(groups):
            rows = slice(gi * SUBLANES, (gi + 1) * SUBLANES)
            tab_ref[rows, 0:half] = jnp.concatenate([v[0] for v in grp], axis=0)
            tab_ref[rows, half:2 * half] = jnp.concatenate([v[1] for v in grp], axis=0)

    return kern


def _s5_operators(lam_re, lam_im, log_dt, b_re, b_im, c_re, c_im):
    g, p = lam_re.shape
    gc = b_re.shape[-1]
    gp = g * p
    tg = S5_TILE_GROUPS
    assert gc * tg == LANES and g % tg == 0
    nj = g // tg
    half = tg * p
    kdim = S5_CHUNK * LANES
    lr = lam_re.reshape(1, gp)
    li = lam_im.reshape(1, gp)
    ldt = jnp.repeat(log_dt, p).reshape(1, gp)
    brt = b_re.transpose(2, 0, 1).reshape(gc, gp)
    bit = b_im.transpose(2, 0, 1).reshape(gc, gp)
    crt = c_re.transpose(1, 0, 2).reshape(gc, gp)
    cit = c_im.transpose(1, 0, 2).reshape(gc, gp)
    row = pl.BlockSpec((1, half), lambda j: (0, j))
    mat = pl.BlockSpec((gc, half), lambda j: (0, j))
    return pl.pallas_call(
        _make_s5_prep_kernel(tg, p),
        grid=(nj,),
        in_specs=[row, row, row, mat, mat, mat, mat],
        out_specs=[
            pl.BlockSpec((None, S5_CHUNK, LANES, LANES), lambda j: (j, 0, 0, 0)),
            pl.BlockSpec((None, S5_CHUNK * gc, 2 * half), lambda j: (j, 0, 0)),
            pl.BlockSpec((None, 2 * half, kdim), lambda j: (j, 0, 0)),
            pl.BlockSpec((None, 5 * SUBLANES, 2 * half), lambda j: (j, 0, 0)),
        ],
        out_shape=[
            jax.ShapeDtypeStruct((nj, S5_CHUNK, LANES, LANES), BF16),
            jax.ShapeDtypeStruct((nj, S5_CHUNK * gc, 2 * half), BF16),
            jax.ShapeDtypeStruct((nj, 2 * half, kdim), BF16),
            jax.ShapeDtypeStruct((nj, 5 * SUBLANES, 2 * half), F32),
        ],
        compiler_params=_params(("parallel",)),
        name="s5_prep",
    )(lr, li, ldt, brt, bit, crt, cit)


def _make_s5_kernel(nb, chunks_per_batch, half, gc):
    tc = S5_CHUNK
    nchunk = nb * chunks_per_batch
    tiles_per_batch = chunks_per_batch // SUBLANES
    ntile = nchunk // SUBLANES

    def kern(u_ref, wio_ref, wis_ref, wso_ref, tab_ref, d_ref, y_ref, s_scr, y_scr, wio_s, wis_s):
        tg = LANES // gc
        gid = (lax.broadcasted_iota(jnp.int32, (1, 2 * half), 1) % half) // (half // tg)
        zero_rows = jnp.zeros((gc, 2 * half), BF16)
        for t in range(tc):
            blk = wis_ref[t * gc:(t + 1) * gc, :]
            for g in range(tg):
                r0 = (t * tg + g) * gc
                wis_s[r0:r0 + gc, :] = jnp.where(gid == g, blk, zero_rows)
        zero_blk = jnp.zeros((LANES, LANES), BF16)
        for t in range(tc):
            for t2 in range(tc):
                wio_s[t * LANES:(t + 1) * LANES, t2 * LANES:(t2 + 1) * LANES] = (
                    wio_ref[t2 - t] if t2 >= t else zero_blk)

        lhs = jnp.concatenate(
            [u_ref[pl.ds(t, nchunk, stride=tc), :].astype(BF16) for t in range(tc)], axis=1)
        s_scr[...] = _dot(lhs, wis_s[...])

        tab = tab_ref[...]
        pw_r, pw_i = tab[0:8, :half], tab[0:8, half:]
        steps = [(d, tab[8 * i:8 * i + 8, :half], tab[8 * i:8 * i + 8, half:])
                 for i, d in ((1, 1), (2, 2), (3, 4))]
        l8_r, l8_i = tab[32:33, :half], tab[32:33, half:]
        first_row = lax.broadcasted_iota(jnp.int32, (SUBLANES, half), 0) == 0

        for m in range(ntile):
            rows = slice(m * SUBLANES, (m + 1) * SUBLANES)
            s = s_scr[rows, :]
            er, ei = s[:, :half], s[:, half:]
            for d, mr, mi in steps:
                rr = pltpu.roll(er, d, axis=0)
                ri = pltpu.roll(ei, d, axis=0)
                er, ei = er + (mr * rr - mi * ri), ei + (mr * ri + mi * rr)
            hr = jnp.where(first_row, 0.0, pltpu.roll(er, 1, axis=0))
            hi = jnp.where(first_row, 0.0, pltpu.roll(ei, 1, axis=0))
            if m % tiles_per_batch == 0:
                cr, ci = er[7:8], ei[7:8]
            else:
                hr = hr + (pw_r * cr - pw_i * ci)
                hi = hi + (pw_r * ci + pw_i * cr)
                cr, ci = er[7:8] + (l8_r * cr - l8_i * ci), ei[7:8] + (l8_r * ci + l8_i * cr)
            s_scr[rows, :] = jnp.concatenate([hr, hi], axis=1)

        y_io = jnp.concatenate(
            [_dot(lhs[:, :(p + 1) * MXU_N], wio_s[0:(p + 1) * MXU_N, p * MXU_N:(p + 1) * MXU_N])
             for p in range(tc * LANES // MXU_N)], axis=1)
        y = y_io + _dot(s_scr[...].astype(BF16), wso_ref[...])
        for t in range(tc):
            y_scr[pl.ds(t, nchunk, stride=tc), :] = y[:, t * LANES:(t + 1) * LANES]
        y_ref[...] = _gelu(y_scr[...] + d_ref[...] * u_ref[...]).astype(BF16)

    return kern


def _s5_core(u3, ops, d_skip, *, nb, seq):
    w_io, w_is, w_so, tab = ops
    nj, rows, _ = u3.shape
    kdim = S5_CHUNK * LANES
    sdim = w_is.shape[-1]
    chunks_per_batch = seq // S5_CHUNK
    gc = w_is.shape[1] // S5_CHUNK
    assert rows == nb * seq and chunks_per_batch % SUBLANES == 0
    tok = pl.BlockSpec((None, rows, LANES), lambda j: (j, 0, 0))
    return pl.pallas_call(
        _make_s5_kernel(nb, chunks_per_batch, sdim // 2, gc),
        grid=(nj,),
        in_specs=[
            tok,
            pl.BlockSpec((None, S5_CHUNK, LANES, LANES), lambda j: (j, 0, 0, 0)),
            pl.BlockSpec((None, S5_CHUNK * gc, sdim), lambda j: (j, 0, 0)),
            pl.BlockSpec((None, sdim, kdim), lambda j: (j, 0, 0)),
            pl.BlockSpec((None, 5 * SUBLANES, sdim), lambda j: (j, 0, 0)),
            pl.BlockSpec((1, LANES), lambda j: (0, j)),
        ],
        out_specs=tok,
        out_shape=jax.ShapeDtypeStruct((nj, rows, LANES), BF16),
        scratch_shapes=[pltpu.VMEM((rows // S5_CHUNK, sdim), F32), pltpu.VMEM((rows, LANES), F32),
                        pltpu.VMEM((kdim, kdim), BF16), pltpu.VMEM((kdim, sdim), BF16)],
        compiler_params=_params(("parallel",)),
        name="s5_core",
    )(u3, w_io, w_is, w_so, tab, d_skip.reshape(1, nj * LANES))


def _band_windows(width, blk, tn):
    starts, ends = [], []
    for n in range(width // tn):
        h_lo = (n * tn) // blk
        h_hi = (n * tn + tn - 1) // blk
        starts.append((h_lo * blk) // LANES)
        ends.append(-(-((h_hi + 1) * blk) // LANES))
    kw = max(e - s for s, e in zip(starts, ends))
    total = width // LANES
    starts = [min(s, total - kw) for s in starts]
    return starts, kw * LANES


def _banded(w, starts, kw, tn):
    blk = w.shape[-1]
    tiles = []
    for n, s in enumerate(starts):
        pieces = []
        for h in range((n * tn) // blk, (n * tn + tn - 1) // blk + 1):
            j0 = max(0, n * tn - h * blk)
            j1 = min(blk, (n + 1) * tn - h * blk)
            r0 = h * blk - s * LANES
            pieces.append(jnp.pad(w[:, h, :, j0:j1], ((0, 0), (r0, kw - r0 - blk), (0, 0))))
        tiles.append(jnp.concatenate(pieces, axis=2))
    return jnp.stack(tiles, axis=1).astype(BF16)


def _make_lru_kernel(e, tm, tn, starts, kw, tiles_per_seq):
    ntile = e // tn
    nrow = tm // SUBLANES

    def kern(gx_ref, w_ref, vec_ref, y_ref, hcar, xb16):
        @pl.when(pl.program_id(0) % tiles_per_seq == 0)
        def _():
            hcar[...] = jnp.zeros((1, e), F32)

        xb16[...] = gx_ref[:, e:2 * e].astype(BF16)

        row = lax.broadcasted_iota(jnp.int32, (SUBLANES, tn), 0)
        masks = [row % HALF_TILE >= d for d in (1, 2)]
        in_upper = row < HALF_TILE
        neg = -vec_ref[2:3, :]
        softplus = jnp.maximum(neg, 0.0) + jnp.log1p(jnp.exp(-jnp.abs(neg)))
        rate = (-0.5 * LRU_C) * softplus
        hb_rg = 0.5 * vec_ref[0:1, :]
        hb_ig = 0.5 * vec_ref[1:2, :]

        for n in range(ntile):
            cols = slice(n * tn, (n + 1) * tn)
            win = xb16[:, starts[n] * LANES:starts[n] * LANES + kw]
            tr = jnp.tanh(0.5 * _dot(win, w_ref[0, n]) + hb_rg[:, cols])
            ig = 0.5 * jnp.tanh(0.5 * _dot(win, w_ref[1, n]) + hb_ig[:, cols]) + 0.5
            a = jnp.exp(rate[:, cols] * tr + rate[:, cols])
            om = 1.0 - a * a
            mult = jnp.where(om > 0.0, om * lax.rsqrt(om), 0.0)
            b = mult * (ig * gx_ref[:, e + n * tn:e + (n + 1) * tn])

            c = hcar[:, cols]
            hs = []
            for m in range(nrow):
                rows = slice(m * SUBLANES, (m + 1) * SUBLANES)
                av, bv = a[rows], b[rows]
                for d, keep in zip((1, 2), masks):
                    bv = bv + jnp.where(keep, av * pltpu.roll(bv, d, axis=0), 0.0)
                    av = jnp.where(keep, av * pltpu.roll(av, d, axis=0), av)
                upper = bv + av * c
                lower = bv + av * upper[HALF_TILE - 1:HALF_TILE]
                h = jnp.where(in_upper, upper, lower)
                c = h[SUBLANES - 1:SUBLANES]
                hs.append(h)
            hcar[:, cols] = c
            y_ref[:, cols] = (jnp.concatenate(hs, axis=0) * gx_ref[:, cols]).astype(BF16)

    return kern


def _lru_core(gx, w_rg, b_rg, w_ig, b_ig, lam, *, seq, tm):
    rows, e2 = gx.shape
    e = e2 // 2
    blk = w_rg.shape[1]
    tn = MXU_N
    assert e % tn == 0 and seq % tm == 0 and tm % SUBLANES == 0
    starts, kw = _band_windows(e, blk, tn)
    gates = _banded(jnp.stack([w_rg, w_ig]), starts, kw, tn)
    vecs = jnp.stack([b_rg, b_ig, lam])
    ntile = e // tn
    return pl.pallas_call(
        _make_lru_kernel(e, tm, tn, starts, kw, seq // tm),
        grid=(rows // tm,),
        in_specs=[
            pl.BlockSpec((tm, e2), lambda m: (m, 0)),
            pl.BlockSpec((2, ntile, kw, tn), lambda m: (0, 0, 0, 0)),
            pl.BlockSpec((3, e), lambda m: (0, 0)),
        ],
        out_specs=pl.BlockSpec((tm, e), lambda m: (m, 0)),
        out_shape=jax.ShapeDtypeStruct((rows, e), BF16),
        scratch_shapes=[pltpu.VMEM((1, e), F32), pltpu.VMEM((tm, e), BF16)],
        compiler_params=_params(("arbitrary",)),
        name="lru_core",
    )(gx, gates, vecs)


def _tile(n, pref):
    t = min(n, pref)
    while n % t:
        t //= 2
    return t


def kernel(x, c, norm_g, w_ada, b_ada, s5_w_in, s5_lam_re, s5_lam_im, s5_log_dt, s5_b_re, s5_b_im, s5_c_re, s5_c_im, s5_d, s5_w_glu, lru_w_in, lru_conv_w, lru_conv_b, lru_w_rg, lru_b_rg, lru_w_ig, lru_b_ig, lru_lam, lru_w_out, ffn_w_gu, ffn_w_down, final_g):
    bsz, seq, d = x.shape
    depth = w_ada.shape[0]
    rows = bsz * seq
    x2 = x.reshape(rows, d)
    mods = _modulation(c, w_ada, b_ada)
    mods = mods.reshape(depth, mods.shape[1], 6, 1, d)
    gains = norm_g.reshape(depth, 2, 1, d)
    tm_in = _tile(seq, 512)
    tm_res = _tile(seq, 256)
    tm_up = _tile(rows, 1024)
    hidden = ffn_w_down.shape[1]

    h = None
    for i in range(depth):
        sh1, sc1, g1, sh2, sc2, g2 = [(mods, i, q) for q in range(6)]
        ffn_norm = ((gains, i, 1), sc2, sh2)
        j = i // 2
        if i % 2 == 0:
            u3 = _ln_matmul_tiled(x2, (gains, i, 0), sc1, sh1, s5_w_in, j, seq=seq, tm=tm_in,
                                  name="s5_in")
            ops = _s5_operators(s5_lam_re[j], s5_lam_im[j], s5_log_dt[j], s5_b_re[j], s5_b_im[j],
                                s5_c_re[j], s5_c_im[j])
            y3 = _s5_core(u3, ops, s5_d[j], nb=bsz, seq=seq)
            x2, h = _matmul_residual(y3, s5_w_glu, j, x2, g1, seq=seq, tm=tm_res, glu=True,
                                     tiled_a=True, next_norm=ffn_norm, name="s5_out")
        else:
            gx = _lru_in(h, lru_w_in, j, lru_conv_w[j], lru_conv_b[j], seq=seq, tm=tm_res,
                         name="lru_in")
            y = _lru_core(gx, lru_w_rg[j], lru_b_rg[j], lru_w_ig[j], lru_b_ig[j], lru_lam[j],
                          seq=seq, tm=tm_res)
            x2, h = _matmul_residual(y, lru_w_out, j, x2, g1, seq=seq, tm=tm_res,
                                     next_norm=ffn_norm, name="lru_out")
        act = _swiglu_up(h, ffn_w_gu, i, tm=tm_up, tn=_tile(hidden, 512), name="ffn_up")
        if i + 1 == depth:
            out = _matmul_residual(act, ffn_w_down, i, x2, g2, seq=seq, tm=tm_res, final_gain=final_g,
                                   name="ffn_down")
            return out.reshape(bsz, seq, d)
        nxt = None
        if (i + 1) % 2 == 1:
            nxt = ((gains, i + 1, 0), (mods, i + 1, 1), (mods, i + 1, 0))
        x2, h = _matmul_residual(act, ffn_w_down, i, x2, g2, seq=seq, tm=tm_res, next_norm=nxt,
                                 name="ffn_down")
```

```python
import jax
import jax.numpy as jnp
from jax import lax
from jax.experimental import pallas as pl
from jax.experimental.pallas import tpu as pltpu

F32 = jnp.float32
BF16 = jnp.bfloat16

EPS = 1e-6
LRU_C = 8.0
LOG2_E = 1.4426950408889634
TINY = 1e-30
GELU_C1 = 0.7978845608028654
GELU_C3 = 0.7978845608028654 * 0.044715
LANES = 128
SUBLANES = 8
HALF_TILE = SUBLANES // 2
MXU_N = 256
S5_CHUNK = 8
S5_TILE_GROUPS = 8
VMEM_LIMIT = 56 * 1024 * 1024
W_CHUNK_BYTES = 8 * 1024 * 1024


def _params(semantics, vmem=VMEM_LIMIT):
    return pltpu.CompilerParams(dimension_semantics=semantics, vmem_limit_bytes=vmem)


def _dot(a, b):
    return jnp.dot(a, b, preferred_element_type=F32)


def _rms_mod(x, gain, scale, shift):
    ms = jnp.mean(x * x, axis=-1, keepdims=True)
    y = x * lax.rsqrt(ms + EPS) * gain
    return y * (1.0 + scale) + shift


def _gelu(x):
    inner = x * (GELU_C1 + GELU_C3 * (x * x))
    hx = 0.5 * x
    return hx + hx * jnp.tanh(inner)


def _cmul(ar, ai, br, bi):
    return ar * br - ai * bi, ar * bi + ai * br


def _mod_kernel(c_ref, w_ref, b_ref, o_ref):
    c = c_ref[...]
    cond = c * jax.nn.sigmoid(c)
    o_ref[...] = _dot(cond.astype(BF16), w_ref[...].astype(BF16)) + b_ref[...]


def _modulation(c, w_ada, b_ada):
    depth, d, n = w_ada.shape
    b = c.shape[0]
    rows = -(-b // SUBLANES) * SUBLANES
    cp = jnp.zeros((rows, d), F32).at[:b].set(c)
    tn = 1024 if n % 1024 == 0 else n
    out = pl.pallas_call(
        _mod_kernel,
        grid=(depth, n // tn),
        in_specs=[
            pl.BlockSpec((rows, d), lambda i, j: (0, 0)),
            pl.BlockSpec((None, d, tn), lambda i, j: (i, 0, j)),
            pl.BlockSpec((None, 1, tn), lambda i, j: (i, 0, j)),
        ],
        out_specs=pl.BlockSpec((None, rows, tn), lambda i, j: (i, 0, j)),
        out_shape=jax.ShapeDtypeStruct((depth, rows, n), F32),
        compiler_params=_params(("parallel", "parallel")),
        name="adaln_mod",
    )(cp, w_ada, b_ada.reshape(depth, 1, n))
    return out


class _Resident:
    def __init__(self, w_all, layer, rows, tm, seq):
        _, self.k, self.n = w_all.shape
        assert rows % tm == 0 and seq % tm == 0
        ck = self.k
        while ck * self.n * 4 > W_CHUNK_BYTES and ck % 32 == 0:
            ck //= 2
        self.ck, self.nc = ck, self.k // ck
        self.layer, self.tm, self.seq = layer, tm, seq
        self.grid = (self.nc + rows // tm,)
        self.scratch = pltpu.VMEM((self.k, self.n), BF16)

    def tile(self, m):
        return jnp.maximum(m - self.nc, 0)

    def weight(self):
        return pl.BlockSpec((None, self.ck, self.n),
                            lambda m: (self.layer, jnp.minimum(m, self.nc - 1), 0))

    def rows(self, width):
        return pl.BlockSpec((self.tm, width), lambda m: (self.tile(m), 0))

    def whole(self, shape):
        return pl.BlockSpec(shape, lambda m: (0,) * len(shape))

    def mod_vec(self, sel):
        table, layer, which = sel
        return pl.BlockSpec(
            (None, 1, None, 1, table.shape[-1]),
            lambda m: (layer, (self.tile(m) * self.tm) // self.seq, which, 0, 0))

    def gain_vec(self, sel):
        table, i, j = sel
        return pl.BlockSpec((None, None, 1, table.shape[-1]), lambda m: (i, j, 0, 0))

    def kernel(self, body, w_pos):
        nc, ck = self.nc, self.ck

        def kern(*refs):
            w_s = refs[-1]
            refs = list(refs[:-1])
            m = pl.program_id(0)

            @pl.when(m < nc)
            def _():
                w_s[pl.ds(pl.multiple_of(m * ck, ck), ck), :] = refs[w_pos][...].astype(BF16)

            @pl.when(m >= nc)
            def _():
                body(*refs[:w_pos], w_s, *refs[w_pos + 1:])

        return kern


def _ln_mm_kernel(x_ref, g_ref, sc_ref, sh_ref, w_ref, o_ref):
    h = _rms_mod(x_ref[...], g_ref[...], sc_ref[0], sh_ref[0]).astype(BF16)
    acc = _dot(h, w_ref[...])
    for q in range(o_ref.shape[0]):
        o_ref[q] = acc[:, q * LANES:(q + 1) * LANES]


def _ln_matmul_tiled(x2d, gain, scale, shift, w_all, layer, *, seq, tm, name):
    rows, d = x2d.shape
    n_out = w_all.shape[-1]
    assert n_out % LANES == 0
    plan = _Resident(w_all, layer, rows, tm, seq)
    return pl.pallas_call(
        plan.kernel(_ln_mm_kernel, 4),
        grid=plan.grid,
        in_specs=[plan.rows(d), plan.gain_vec(gain), plan.mod_vec(scale), plan.mod_vec(shift),
                  plan.weight()],
        out_specs=pl.BlockSpec((n_out // LANES, tm, LANES), lambda m: (0, plan.tile(m), 0)),
        out_shape=jax.ShapeDtypeStruct((n_out // LANES, rows, LANES), F32),
        scratch_shapes=[plan.scratch],
        compiler_params=_params(("arbitrary",)),
        name=name,
    )(x2d, gain[0], scale[0], shift[0], w_all)


def _make_lru_in_kernel(e, tm, tiles_per_seq, first_step):
    halo = SUBLANES

    def kern(a_ref, w_ref, cw_ref, cb_ref, o_ref, xpad):
        @pl.when((pl.program_id(0) - first_step) % tiles_per_seq == 0)
        def _():
            xpad[0:halo, :] = jnp.zeros((halo, e), F32)

        acc = _dot(a_ref[...], w_ref[...])
        o_ref[:, 0:e] = _gelu(acc[:, 0:e])
        xpad[halo:halo + tm, :] = acc[:, e:2 * e]
        nk = cw_ref.shape[0]
        xc = cb_ref[...] + cw_ref[nk - 1:nk, :] * xpad[halo:halo + tm, :]
        for k in range(nk - 1):
            off = halo - (nk - 1) + k
            xc = xc + cw_ref[k:k + 1, :] * xpad[off:off + tm, :]
        xpad[0:halo, :] = xpad[tm:tm + halo, :]
        o_ref[:, e:2 * e] = xc

    return kern


def _lru_in(a, w_all, layer, conv_w, conv_b, *, seq, tm, name):
    rows, k = a.shape
    e = w_all.shape[-1] // 2
    nk = conv_w.shape[0]
    assert nk - 1 <= SUBLANES
    plan = _Resident(w_all, layer, rows, tm, seq)
    return pl.pallas_call(
        plan.kernel(_make_lru_in_kernel(e, tm, seq // tm, plan.nc), 1),
        grid=plan.grid,
        in_specs=[plan.rows(k), plan.weight(), plan.whole((nk, e)), plan.whole((1, e))],
        out_specs=plan.rows(2 * e),
        out_shape=jax.ShapeDtypeStruct((rows, 2 * e), F32),
        scratch_shapes=[pltpu.VMEM((tm + SUBLANES, e), F32), plan.scratch],
        compiler_params=_params(("arbitrary",)),
        name=name,
    )(a, w_all, conv_w.reshape(nk, e), conv_b.reshape(1, e))


def _swiglu_kernel(h_ref, wg_ref, wu_ref, o_ref, wg_s, wu_s):
    @pl.when(pl.program_id(1) == 0)
    def _():
        wg_s[...] = wg_ref[...].astype(BF16)
        wu_s[...] = wu_ref[...].astype(BF16)

    h = h_ref[...]
    g = _dot(h, wg_s[...])
    o_ref[...] = (g * jax.nn.sigmoid(g) * _dot(h, wu_s[...])).astype(BF16)


def _swiglu_up(h, w_all, layer, *, tm, tn, name):
    rows, d = h.shape
    n_out = w_all.shape[-1] // 2
    assert rows % tm == 0 and n_out % tn == 0
    nt = n_out // tn
    return pl.pallas_call(
        _swiglu_kernel,
        grid=(nt, rows // tm),
        in_specs=[
            pl.BlockSpec((tm, d), lambda n, m: (m, 0)),
            pl.BlockSpec((None, d, tn), lambda n, m: (layer, 0, n)),
            pl.BlockSpec((None, d, tn), lambda n, m: (layer, 0, n + nt)),
        ],
        out_specs=pl.BlockSpec((tm, tn), lambda n, m: (m, n)),
        out_shape=jax.ShapeDtypeStruct((rows, n_out), BF16),
        scratch_shapes=[pltpu.VMEM((d, tn), BF16), pltpu.VMEM((d, tn), BF16)],
        compiler_params=_params(("parallel", "arbitrary")),
        name=name,
    )(h, w_all, w_all)


def _make_mm_res_kernel(glu, tiled_a, n_out, mode):
    def kern(*refs):
        a_ref, w_ref, res_ref, gate_ref = refs[:4]
        if tiled_a:
            a = jnp.concatenate([a_ref[q] for q in range(a_ref.shape[0])], axis=1)
        else:
            a = a_ref[...]
        acc = _dot(a, w_ref[...])
        if glu:
            acc = acc[:, :n_out] * jax.nn.sigmoid(acc[:, n_out:])
        x = res_ref[...] + gate_ref[0] * acc
        if mode == "final":
            g_ref, o_ref = refs[4:]
            ms = jnp.mean(x * x, axis=-1, keepdims=True)
            o_ref[...] = x * lax.rsqrt(ms + EPS) * g_ref[...]
        elif mode == "x+h":
            g_ref, sc_ref, sh_ref, x_out, h_out = refs[4:]
            x_out[...] = x
            h_out[...] = _rms_mod(x, g_ref[...], sc_ref[0], sh_ref[0]).astype(BF16)
        else:
            (x_out,) = refs[4:]
            x_out[...] = x

    return kern


def _matmul_residual(a, w_all, layer, res2d, gate, *, seq, tm, glu=False, tiled_a=False,
                     next_norm=None, final_gain=None, name):
    rows, n_out = res2d.shape
    k = w_all.shape[1]
    plan = _Resident(w_all, layer, rows, tm, seq)
    if tiled_a:
        a_spec = pl.BlockSpec((k // LANES, tm, LANES), lambda m: (0, plan.tile(m), 0))
    else:
        a_spec = plan.rows(k)
    row_spec = plan.rows(n_out)
    in_specs = [a_spec, plan.weight(), row_spec, plan.mod_vec(gate)]
    args = [a, w_all, res2d, gate[0]]
    x_shape = jax.ShapeDtypeStruct((rows, n_out), F32)
    if final_gain is not None:
        mode = "final"
        in_specs.append(plan.whole((1, n_out)))
        args.append(final_gain.reshape(1, n_out))
        out_specs, out_shape = row_spec, x_shape
    elif next_norm is not None:
        mode = "x+h"
        gain, scale, shift = next_norm
        in_specs += [plan.gain_vec(gain), plan.mod_vec(scale), plan.mod_vec(shift)]
        args += [gain[0], scale[0], shift[0]]
        out_specs = [row_spec, row_spec]
        out_shape = [x_shape, jax.ShapeDtypeStruct((rows, n_out), BF16)]
    else:
        mode = "x"
        out_specs, out_shape = [row_spec], [x_shape]
    out = pl.pallas_call(
        plan.kernel(_make_mm_res_kernel(glu, tiled_a, n_out, mode), 1),
        grid=plan.grid,
        in_specs=in_specs,
        out_specs=out_specs,
        out_shape=out_shape,
        scratch_shapes=[plan.scratch],
        compiler_params=_params(("arbitrary",)),
        name=name,
    )(*args)
    if mode == "final":
        return out
    return (out[0], out[1]) if mode == "x+h" else (out[0], None)


def _make_s5_prep_kernel(tg, p):
    tc = S5_CHUNK
    half = tg * p

    def kern(lr_ref, li_ref, ldt_ref, br_ref, bi_ref, cr_ref, ci_ref,
             wio_ref, wis_ref, wso_ref, tab_ref):
        lr = lr_ref[...]
        li = li_ref[...]
        dt = jnp.exp(ldt_ref[...])
        mag = jnp.exp(lr * dt)
        ab_re = mag * jnp.cos(li * dt)
        ab_im = mag * jnp.sin(li * dt)
        nr, ni = ab_re - 1.0, ab_im
        den = lr * lr + li * li
        f_re = (nr * lr + ni * li) / den
        f_im = (ni * lr - nr * li) / den
        br, bi = br_ref[...], bi_ref[...]
        gc = br.shape[0]
        bb_re = f_re * br - f_im * bi
        bb_im = f_re * bi + f_im * br
        cr, ci = cr_ref[...], ci_ref[...]
        gid = lax.broadcasted_iota(jnp.int32, (1, half), 1) // p

        def expand(v):
            return jnp.concatenate([jnp.where(gid == g, v, 0.0) for g in range(tg)], axis=0)

        pows = [(jnp.ones_like(ab_re), jnp.zeros_like(ab_im))]
        for _ in range(tc):
            pows.append(_cmul(pows[-1][0], pows[-1][1], ab_re, ab_im))

        for t in range(tc):
            pr, pi = pows[tc - 1 - t]
            wr, wi = _cmul(pr, pi, bb_re, bb_im)
            rows = slice(t * gc, (t + 1) * gc)
            wis_ref[rows, 0:half] = wr.astype(BF16)
            wis_ref[rows, half:2 * half] = wi.astype(BF16)

        def split(v):
            hi = v.astype(BF16)
            return hi, (v - hi.astype(F32)).astype(BF16)

        def nt_dot(a, b):
            return lax.dot_general(a, b, (((1,), (1,)), ((), ())), preferred_element_type=F32)

        b_hi, b_lo = split(jnp.concatenate([expand(bb_re), expand(bb_im)], axis=1))
        q_all = []
        for tau in range(tc + 1):
            pr, pi = pows[tau]
            qr, qi = _cmul(cr, ci, pr, pi)
            er, ei = expand(qr), expand(-qi)
            if tau >= 1:
                cols = slice((tau - 1) * LANES, tau * LANES)
                wso_ref[0:half, cols] = er.T.astype(BF16)
                wso_ref[half:2 * half, cols] = ei.T.astype(BF16)
            if tau < tc:
                q_all.append(jnp.concatenate([er, ei], axis=1))
        q_hi, q_lo = split(jnp.concatenate(q_all, axis=0))
        kt = nt_dot(q_hi, b_hi) + (nt_dot(q_lo, b_hi) + nt_dot(q_hi, b_lo))
        for tau in range(tc):
            wio_ref[tau] = kt[tau * LANES:(tau + 1) * LANES, :].T.astype(BF16)

        lam = [(jnp.ones_like(ab_re), jnp.zeros_like(ab_im))]
        for _ in range(SUBLANES):
            lam.append(_cmul(lam[-1][0], lam[-1][1], pows[tc][0], pows[tc][1]))
        zrow = jnp.zeros_like(ab_re)
        groups = [[lam[r] for r in range(SUBLANES)]]
        for d in (1, 2, 4):
            groups.append([lam[d] if r >= d else (zrow, zrow) for r in range(SUBLANES)])
        groups.append([lam[SUBLANES]] * SUBLANES)
        for gi, grp in enumerate(groups):
            rows = slice(gi * SUBLANES, (gi + 1) * SUBLANES)
            tab_ref[rows, 0:half] = jnp.concatenate([v[0] for v in grp], axis=0)
            tab_ref[rows, half:2 * half] = jnp.concatenate([v[1] for v in grp], axis=0)

    return kern


def _s5_operators(lam_re, lam_im, log_dt, b_re, b_im, c_re, c_im):
    g, p = lam_re.shape
    gc = b_re.shape[-1]
    gp = g * p
    tg = S5_TILE_GROUPS
    assert gc * tg == LANES and g % tg == 0
    nj = g // tg
    half = tg * p
    kdim = S5_CHUNK * LANES
    lr = lam_re.reshape(1, gp)
    li = lam_im.reshape(1, gp)
    ldt = jnp.repeat(log_dt, p).reshape(1, gp)
    brt = b_re.transpose(2, 0, 1).reshape(gc, gp)
    bit = b_im.transpose(2, 0, 1).reshape(gc, gp)
    crt = c_re.transpose(1, 0, 2).reshape(gc, gp)
    cit = c_im.transpose(1, 0, 2).reshape(gc, gp)
    row = pl.BlockSpec((1, half), lambda j: (0, j))
    mat = pl.BlockSpec((gc, half), lambda j: (0, j))
    return pl.pallas_call(
        _make_s5_prep_kernel(tg, p),
        grid=(nj,),
        in_specs=[row, row, row, mat, mat, mat, mat],
        out_specs=[
            pl.BlockSpec((None, S5_CHUNK, LANES, LANES), lambda j: (j, 0, 0, 0)),
            pl.BlockSpec((None, S5_CHUNK * gc, 2 * half), lambda j: (j, 0, 0)),
            pl.BlockSpec((None, 2 * half, kdim), lambda j: (j, 0, 0)),
            pl.BlockSpec((None, 5 * SUBLANES, 2 * half), lambda j: (j, 0, 0)),
        ],
        out_shape=[
            jax.ShapeDtypeStruct((nj, S5_CHUNK, LANES, LANES), BF16),
            jax.ShapeDtypeStruct((nj, S5_CHUNK * gc, 2 * half), BF16),
            jax.ShapeDtypeStruct((nj, 2 * half, kdim), BF16),
            jax.ShapeDtypeStruct((nj, 5 * SUBLANES, 2 * half), F32),
        ],
        compiler_params=_params(("parallel",)),
        name="s5_prep",
    )(lr, li, ldt, brt, bit, crt, cit)


def _make_s5_kernel(nb, chunks_per_batch, half, gc):
    tc = S5_CHUNK
    nchunk = nb * chunks_per_batch
    tiles_per_batch = chunks_per_batch // SUBLANES
    ntile = nchunk // SUBLANES

    def kern(u_ref, wio_ref, wis_ref, wso_ref, tab_ref, d_ref, y_ref, s_scr, y_scr, wio_s, wis_s):
        tg = LANES // gc
        gid = (lax.broadcasted_iota(jnp.int32, (1, 2 * half), 1) % half) // (half // tg)
        zero_rows = jnp.zeros((gc, 2 * half), BF16)
        for t in range(tc):
            blk = wis_ref[t * gc:(t + 1) * gc, :]
            for g in range(tg):
                r0 = (t * tg + g) * gc
                wis_s[r0:r0 + gc, :] = jnp.where(gid == g, blk, zero_rows)
        zero_blk = jnp.zeros((LANES, LANES), BF16)
        for t in range(tc):
            for t2 in range(tc):
                wio_s[t * LANES:(t + 1) * LANES, t2 * LANES:(t2 + 1) * LANES] = (
                    wio_ref[t2 - t] if t2 >= t else zero_blk)

        lhs = jnp.concatenate(
            [u_ref[pl.ds(t, nchunk, stride=tc), :].astype(BF16) for t in range(tc)], axis=1)
        s_scr[...] = _dot(lhs, wis_s[...])

        tab = tab_ref[...]
        pw_r, pw_i = tab[0:8, :half], tab[0:8, half:]
        steps = [(d, tab[8 * i:8 * i + 8, :half], tab[8 * i:8 * i + 8, half:])
                 for i, d in ((1, 1), (2, 2), (3, 4))]
        l8_r, l8_i = tab[32:33, :half], tab[32:33, half:]
        first_row = lax.broadcasted_iota(jnp.int32, (SUBLANES, half), 0) == 0

        for m in range(ntile):
            rows = slice(m * SUBLANES, (m + 1) * SUBLANES)
            s = s_scr[rows, :]
            er, ei = s[:, :half], s[:, half:]
            for d, mr, mi in steps:
                rr = pltpu.roll(er, d, axis=0)
                ri = pltpu.roll(ei, d, axis=0)
                er, ei = er + (mr * rr - mi * ri), ei + (mr * ri + mi * rr)
            hr = jnp.where(first_row, 0.0, pltpu.roll(er, 1, axis=0))
            hi = jnp.where(first_row, 0.0, pltpu.roll(ei, 1, axis=0))
            if m % tiles_per_batch == 0:
                cr, ci = er[7:8], ei[7:8]
            else:
                hr = hr + (pw_r * cr - pw_i * ci)
                hi = hi + (pw_r * ci + pw_i * cr)
                cr, ci = er[7:8] + (l8_r * cr - l8_i * ci), ei[7:8] + (l8_r * ci + l8_i * cr)
            s_scr[rows, :] = jnp.concatenate([hr, hi], axis=1)

        y_io = jnp.concatenate(
            [_dot(lhs[:, :(p + 1) * MXU_N], wio_s[0:(p + 1) * MXU_N, p * MXU_N:(p + 1) * MXU_N])
             for p in range(tc * LANES // MXU_N)], axis=1)
        y = y_io + _dot(s_scr[...].astype(BF16), wso_ref[...])
        for t in range(tc):
            y_scr[pl.ds(t, nchunk, stride=tc), :] = y[:, t * LANES:(t + 1) * LANES]
        y_ref[...] = _gelu(y_scr[...] + d_ref[...] * u_ref[...]).astype(BF16)

    return kern


def _s5_core(u3, ops, d_skip, *, nb, seq):
    w_io, w_is, w_so, tab = ops
    nj, rows, _ = u3.shape
    kdim = S5_CHUNK * LANES
    sdim = w_is.shape[-1]
    chunks_per_batch = seq // S5_CHUNK
    gc = w_is.shape[1] // S5_CHUNK
    assert rows == nb * seq and chunks_per_batch % SUBLANES == 0
    tok = pl.BlockSpec((None, rows, LANES), lambda j: (j, 0, 0))
    return pl.pallas_call(
        _make_s5_kernel(nb, chunks_per_batch, sdim // 2, gc),
        grid=(nj,),
        in_specs=[
            tok,
            pl.BlockSpec((None, S5_CHUNK, LANES, LANES), lambda j: (j, 0, 0, 0)),
            pl.BlockSpec((None, S5_CHUNK * gc, sdim), lambda j: (j, 0, 0)),
            pl.BlockSpec((None, sdim, kdim), lambda j: (j, 0, 0)),
            pl.BlockSpec((None, 5 * SUBLANES, sdim), lambda j: (j, 0, 0)),
            pl.BlockSpec((1, LANES), lambda j: (0, j)),
        ],
        out_specs=tok,
        out_shape=jax.ShapeDtypeStruct((nj, rows, LANES), BF16),
        scratch_shapes=[pltpu.VMEM((rows // S5_CHUNK, sdim), F32), pltpu.VMEM((rows, LANES), F32),
                        pltpu.VMEM((kdim, kdim), BF16), pltpu.VMEM((kdim, sdim), BF16)],
        compiler_params=_params(("parallel",)),
        name="s5_core",
    )(u3, w_io, w_is, w_so, tab, d_skip.reshape(1, nj * LANES))


def _band_windows(width, blk, tn):
    starts, ends = [], []
    for n in range(width // tn):
        h_lo = (n * tn) // blk
        h_hi = (n * tn + tn - 1) // blk
        starts.append((h_lo * blk) // LANES)
        ends.append(-(-((h_hi + 1) * blk) // LANES))
    kw = max(e - s for s, e in zip(starts, ends))
    total = width // LANES
    starts = [min(s, total - kw) for s in starts]
    return starts, kw * LANES


def _banded(w, starts, kw, tn):
    blk = w.shape[-1]
    tiles = []
    for n, s in enumerate(starts):
        pieces = []
        for h in range((n * tn) // blk, (n * tn + tn - 1) // blk + 1):
            j0 = max(0, n * tn - h * blk)
            j1 = min(blk, (n + 1) * tn - h * blk)
            r0 = h * blk - s * LANES
            pieces.append(jnp.pad(w[:, h, :, j0:j1], ((0, 0), (r0, kw - r0 - blk), (0, 0))))
        tiles.append(jnp.concatenate(pieces, axis=2))
    return jnp.stack(tiles, axis=1).astype(BF16)


def _make_lru_kernel(e, tm, tn, starts, kw, tiles_per_seq):
    ntile = e // tn
    nrow = tm // SUBLANES

    def kern(gx_ref, w_ref, vec_ref, y_ref, hcar, xb16):
        @pl.when(pl.program_id(0) % tiles_per_seq == 0)
        def _():
            hcar[...] = jnp.zeros((1, e), F32)

        xb16[...] = gx_ref[:, e:2 * e].astype(BF16)

        row = lax.broadcasted_iota(jnp.int32, (SUBLANES, tn), 0)
        masks = [row % HALF_TILE >= d for d in (1, 2)]
        in_upper = row < HALF_TILE
        neg = -vec_ref[2:3, :]
        softplus = jnp.maximum(neg, 0.0) + jnp.log1p(jnp.exp(-jnp.abs(neg)))
        rate = (-0.5 * LRU_C * LOG2_E) * softplus
        hb_rg = 0.5 * vec_ref[0:1, :]
        hb_ig = 0.5 * vec_ref[1:2, :]

        for n in range(ntile):
            cols = slice(n * tn, (n + 1) * tn)
            win = xb16[:, starts[n] * LANES:starts[n] * LANES + kw]
            tr = jnp.tanh(_dot(win, w_ref[0, n]) + hb_rg[:, cols])
            ig = 0.5 * jnp.tanh(_dot(win, w_ref[1, n]) + hb_ig[:, cols]) + 0.5
            a = jnp.exp2(rate[:, cols] * tr + rate[:, cols])
            om = 1.0 - a * a
            mult = om * lax.rsqrt(jnp.maximum(om, TINY))
            b = mult * (ig * gx_ref[:, e + n * tn:e + (n + 1) * tn])

            c = hcar[:, cols]
            hs = []
            for m in range(nrow):
                rows = slice(m * SUBLANES, (m + 1) * SUBLANES)
                av, bv = a[rows], b[rows]
                for d, keep in zip((1, 2), masks):
                    bv = bv + jnp.where(keep, av * pltpu.roll(bv, d, axis=0), 0.0)
                    av = jnp.where(keep, av * pltpu.roll(av, d, axis=0), av)
                upper = bv + av * c
                lower = bv + av * upper[HALF_TILE - 1:HALF_TILE]
                h = jnp.where(in_upper, upper, lower)
                c = h[SUBLANES - 1:SUBLANES]
                hs.append(h)
            hcar[:, cols] = c
            y_ref[:, cols] = (jnp.concatenate(hs, axis=0) * gx_ref[:, cols]).astype(BF16)

    return kern


def _lru_core(gx, w_rg, b_rg, w_ig, b_ig, lam, *, seq, tm):
    rows, e2 = gx.shape
    e = e2 // 2
    blk = w_rg.shape[1]
    tn = MXU_N
    assert e % tn == 0 and seq % tm == 0 and tm % SUBLANES == 0
    starts, kw = _band_windows(e, blk, tn)
    gates = _banded(0.5 * jnp.stack([w_rg, w_ig]), starts, kw, tn)
    vecs = jnp.stack([b_rg, b_ig, lam])
    ntile = e // tn
    return pl.pallas_call(
        _make_lru_kernel(e, tm, tn, starts, kw, seq // tm),
        grid=(rows // tm,),
        in_specs=[
            pl.BlockSpec((tm, e2), lambda m: (m, 0)),
            pl.BlockSpec((2, ntile, kw, tn), lambda m: (0, 0, 0, 0)),
            pl.BlockSpec((3, e), lambda m: (0, 0)),
        ],
        out_specs=pl.BlockSpec((tm, e), lambda m: (m, 0)),
        out_shape=jax.ShapeDtypeStruct((rows, e), BF16),
        scratch_shapes=[pltpu.VMEM((1, e), F32), pltpu.VMEM((tm, e), BF16)],
        compiler_params=_params(("arbitrary",)),
        name="lru_core",
    )(gx, gates, vecs)


def _tile(n, pref):
    t = min(n, pref)
    while n % t:
        t //= 2
    return t


def kernel(x, c, norm_g, w_ada, b_ada, s5_w_in, s5_lam_re, s5_lam_im, s5_log_dt, s5_b_re, s5_b_im, s5_c_re, s5_c_im, s5_d, s5_w_glu, lru_w_in, lru_conv_w, lru_conv_b, lru_w_rg, lru_b_rg, lru_w_ig, lru_b_ig, lru_lam, lru_w_out, ffn_w_gu, ffn_w_down, final_g):
    bsz, seq, d = x.shape
    depth = w_ada.shape[0]
    rows = bsz * seq
    x2 = x.reshape(rows, d)
    mods = _modulation(c, w_ada, b_ada)
    mods = mods.reshape(depth, mods.shape[1], 6, 1, d)
    gains = norm_g.reshape(depth, 2, 1, d)
    tm_in = _tile(seq, 512)
    tm_res = _tile(seq, 256)
    tm_up = _tile(rows, 1024)
    hidden = ffn_w_down.shape[1]

    h = None
    for i in range(depth):
        sh1, sc1, g1, sh2, sc2, g2 = [(mods, i, q) for q in range(6)]
        ffn_norm = ((gains, i, 1), sc2, sh2)
        j = i // 2
        if i % 2 == 0:
            u3 = _ln_matmul_tiled(x2, (gains, i, 0), sc1, sh1, s5_w_in, j, seq=seq, tm=tm_in,
                                  name="s5_in")
            ops = _s5_operators(s5_lam_re[j], s5_lam_im[j], s5_log_dt[j], s5_b_re[j], s5_b_im[j],
                                s5_c_re[j], s5_c_im[j])
            y3 = _s5_core(u3, ops, s5_d[j], nb=bsz, seq=seq)
            x2, h = _matmul_residual(y3, s5_w_glu, j, x2, g1, seq=seq, tm=tm_res, glu=True,
                                     tiled_a=True, next_norm=ffn_norm, name="s5_out")
        else:
            gx = _lru_in(h, lru_w_in, j, lru_conv_w[j], lru_conv_b[j], seq=seq, tm=tm_res,
                         name="lru_in")
            y = _lru_core(gx, lru_w_rg[j], lru_b_rg[j], lru_w_ig[j], lru_b_ig[j], lru_lam[j],
                          seq=seq, tm=tm_res)
            x2, h = _matmul_residual(y, lru_w_out, j, x2, g1, seq=seq, tm=tm_res,
                                     next_norm=ffn_norm, name="lru_out")
        act = _swiglu_up(h, ffn_w_gu, i, tm=tm_up, tn=_tile(hidden, 512), name="ffn_up")
        if i + 1 == depth:
            out = _matmul_residual(act, ffn_w_down, i, x2, g2, seq=seq, tm=tm_res, final_gain=final_g,
                                   name="ffn_down")
            return out.reshape(bsz, seq, d)
        nxt = None
        if (i + 1) % 2 == 1:
            nxt = ((gains, i + 1, 0), (mods, i + 1, 1), (mods, i + 1, 0))
        x2, h = _matmul_residual(act, ffn_w_down, i, x2, g2, seq=seq, tm=tm_res, next_norm=nxt,
                                 name="ffn_down")
```

```python
import jax
import jax.numpy as jnp
from jax import lax
from jax.experimental import pallas as pl
from jax.experimental.pallas import tpu as pltpu

F32 = jnp.float32
BF16 = jnp.bfloat16

EPS = 1e-6
LRU_C = 8.0
LOG2_E = 1.4426950408889634
TINY = 1e-30
GELU_C1 = 0.7978845608028654
GELU_C3 = 0.7978845608028654 * 0.044715
LANES = 128
SUBLANES = 8
HALF_TILE = SUBLANES // 2
MXU_N = 256
S5_CHUNK = 8
S5_TILE_GROUPS = 8
VMEM_LIMIT = 56 * 1024 * 1024
W_CHUNK_BYTES = 8 * 1024 * 1024


def _params(semantics, vmem=VMEM_LIMIT):
    return pltpu.CompilerParams(dimension_semantics=semantics, vmem_limit_bytes=vmem)


def _dot(a, b):
    return jnp.dot(a, b, preferred_element_type=F32)


def _rms_mod(x, gain, scale, shift):
    ms = jnp.mean(x * x, axis=-1, keepdims=True)
    y = x * lax.rsqrt(ms + EPS) * gain
    return y * (1.0 + scale) + shift


def _gelu(x):
    inner = x * (GELU_C1 + GELU_C3 * (x * x))
    hx = 0.5 * x
    return hx + hx * jnp.tanh(inner)


def _cmul(ar, ai, br, bi):
    return ar * br - ai * bi, ar * bi + ai * br


def _mod_kernel(c_ref, w_ref, b_ref, o_ref):
    c = c_ref[...]
    cond = c * jax.nn.sigmoid(c)
    o_ref[...] = _dot(cond.astype(BF16), w_ref[...].astype(BF16)) + b_ref[...]


def _modulation(c, w_ada, b_ada):
    depth, d, n = w_ada.shape
    b = c.shape[0]
    rows = -(-b // SUBLANES) * SUBLANES
    cp = jnp.zeros((rows, d), F32).at[:b].set(c)
    tn = 1024 if n % 1024 == 0 else n
    out = pl.pallas_call(
        _mod_kernel,
        grid=(depth, n // tn),
        in_specs=[
            pl.BlockSpec((rows, d), lambda i, j: (0, 0)),
            pl.BlockSpec((None, d, tn), lambda i, j: (i, 0, j)),
            pl.BlockSpec((None, 1, tn), lambda i, j: (i, 0, j)),
        ],
        out_specs=pl.BlockSpec((None, rows, tn), lambda i, j: (i, 0, j)),
        out_shape=jax.ShapeDtypeStruct((depth, rows, n), F32),
        compiler_params=_params(("parallel", "parallel")),
        name="adaln_mod",
    )(cp, w_ada, b_ada.reshape(depth, 1, n))
    return out


class _Resident:
    def __init__(self, w_all, layer, rows, tm, seq):
        _, self.k, self.n = w_all.shape
        assert rows % tm == 0 and seq % tm == 0
        ck = self.k
        while ck * self.n * 4 > W_CHUNK_BYTES and ck % 32 == 0:
            ck //= 2
        self.ck, self.nc = ck, self.k // ck
        self.layer, self.tm, self.seq = layer, tm, seq
        self.grid = (self.nc + rows // tm,)
        self.scratch = pltpu.VMEM((self.k, self.n), BF16)

    def tile(self, m):
        return jnp.maximum(m - self.nc, 0)

    def weight(self):
        return pl.BlockSpec((None, self.ck, self.n),
                            lambda m: (self.layer, jnp.minimum(m, self.nc - 1), 0))

    def rows(self, width):
        return pl.BlockSpec((self.tm, width), lambda m: (self.tile(m), 0))

    def whole(self, shape):
        return pl.BlockSpec(shape, lambda m: (0,) * len(shape))

    def mod_vec(self, sel):
        table, layer, which = sel
        return pl.BlockSpec(
            (None, 1, None, 1, table.shape[-1]),
            lambda m: (layer, (self.tile(m) * self.tm) // self.seq, which, 0, 0))

    def gain_vec(self, sel):
        table, i, j = sel
        return pl.BlockSpec((None, None, 1, table.shape[-1]), lambda m: (i, j, 0, 0))

    def kernel(self, body, w_pos):
        nc, ck = self.nc, self.ck

        def kern(*refs):
            w_s = refs[-1]
            refs = list(refs[:-1])
            m = pl.program_id(0)

            @pl.when(m < nc)
            def _():
                w_s[pl.ds(pl.multiple_of(m * ck, ck), ck), :] = refs[w_pos][...].astype(BF16)

            @pl.when(m >= nc)
            def _():
                body(*refs[:w_pos], w_s, *refs[w_pos + 1:])

        return kern


def _ln_mm_kernel(x_ref, g_ref, sc_ref, sh_ref, w_ref, o_ref):
    h = _rms_mod(x_ref[...], g_ref[...], sc_ref[0], sh_ref[0]).astype(BF16)
    acc = _dot(h, w_ref[...])
    for q in range(o_ref.shape[0]):
        o_ref[q] = acc[:, q * LANES:(q + 1) * LANES]


def _ln_matmul_tiled(x2d, gain, scale, shift, w_all, layer, *, seq, tm, name):
    rows, d = x2d.shape
    n_out = w_all.shape[-1]
    assert n_out % LANES == 0
    plan = _Resident(w_all, layer, rows, tm, seq)
    return pl.pallas_call(
        plan.kernel(_ln_mm_kernel, 4),
        grid=plan.grid,
        in_specs=[plan.rows(d), plan.gain_vec(gain), plan.mod_vec(scale), plan.mod_vec(shift),
                  plan.weight()],
        out_specs=pl.BlockSpec((n_out // LANES, tm, LANES), lambda m: (0, plan.tile(m), 0)),
        out_shape=jax.ShapeDtypeStruct((n_out // LANES, rows, LANES), F32),
        scratch_shapes=[plan.scratch],
        compiler_params=_params(("arbitrary",)),
        name=name,
    )(x2d, gain[0], scale[0], shift[0], w_all)


def _make_lru_in_kernel(e, tm, tiles_per_seq, first_step):
    halo = SUBLANES

    def kern(a_ref, w_ref, cw_ref, cb_ref, o_ref, xpad):
        @pl.when((pl.program_id(0) - first_step) % tiles_per_seq == 0)
        def _():
            xpad[0:halo, :] = jnp.zeros((halo, e), F32)

        acc = _dot(a_ref[...], w_ref[...])
        o_ref[:, 0:e] = _gelu(acc[:, 0:e])
        xpad[halo:halo + tm, :] = acc[:, e:2 * e]
        nk = cw_ref.shape[0]
        xc = cb_ref[...] + cw_ref[nk - 1:nk, :] * xpad[halo:halo + tm, :]
        for k in range(nk - 1):
            off = halo - (nk - 1) + k
            xc = xc + cw_ref[k:k + 1, :] * xpad[off:off + tm, :]
        xpad[0:halo, :] = xpad[tm:tm + halo, :]
        o_ref[:, e:2 * e] = xc

    return kern


def _lru_in(a, w_all, layer, conv_w, conv_b, *, seq, tm, name):
    rows, k = a.shape
    e = w_all.shape[-1] // 2
    nk = conv_w.shape[0]
    assert nk - 1 <= SUBLANES
    plan = _Resident(w_all, layer, rows, tm, seq)
    return pl.pallas_call(
        plan.kernel(_make_lru_in_kernel(e, tm, seq // tm, plan.nc), 1),
        grid=plan.grid,
        in_specs=[plan.rows(k), plan.weight(), plan.whole((nk, e)), plan.whole((1, e))],
        out_specs=plan.rows(2 * e),
        out_shape=jax.ShapeDtypeStruct((rows, 2 * e), F32),
        scratch_shapes=[pltpu.VMEM((tm + SUBLANES, e), F32), plan.scratch],
        compiler_params=_params(("arbitrary",)),
        name=name,
    )(a, w_all, conv_w.reshape(nk, e), conv_b.reshape(1, e))


def _swiglu_kernel(h_ref, wg_ref, wu_ref, o_ref, wg_s, wu_s):
    @pl.when(pl.program_id(1) == 0)
    def _():
        wg_s[...] = wg_ref[...].astype(BF16)
        wu_s[...] = wu_ref[...].astype(BF16)

    h = h_ref[...]
    g = _dot(h, wg_s[...])
    hg = 0.5 * g
    o_ref[...] = ((hg + hg * jnp.tanh(hg)) * _dot(h, wu_s[...])).astype(BF16)


def _swiglu_up(h, w_all, layer, *, tm, tn, name):
    rows, d = h.shape
    n_out = w_all.shape[-1] // 2
    assert rows % tm == 0 and n_out % tn == 0
    nt = n_out // tn
    return pl.pallas_call(
        _swiglu_kernel,
        grid=(nt, rows // tm),
        in_specs=[
            pl.BlockSpec((tm, d), lambda n, m: (m, 0)),
            pl.BlockSpec((None, d, tn), lambda n, m: (layer, 0, n)),
            pl.BlockSpec((None, d, tn), lambda n, m: (layer, 0, n + nt)),
        ],
        out_specs=pl.BlockSpec((tm, tn), lambda n, m: (m, n)),
        out_shape=jax.ShapeDtypeStruct((rows, n_out), BF16),
        scratch_shapes=[pltpu.VMEM((d, tn), BF16), pltpu.VMEM((d, tn), BF16)],
        compiler_params=_params(("parallel", "arbitrary")),
        name=name,
    )(h, w_all, w_all)


def _make_mm_res_kernel(glu, tiled_a, n_out, mode):
    def kern(*refs):
        a_ref, w_ref, res_ref, gate_ref = refs[:4]
        if tiled_a:
            a = jnp.concatenate([a_ref[q] for q in range(a_ref.shape[0])], axis=1)
        else:
            a = a_ref[...]
        acc = _dot(a, w_ref[...])
        if glu:
            acc = acc[:, :n_out] * (0.5 * jnp.tanh(0.5 * acc[:, n_out:]) + 0.5)
        x = res_ref[...] + gate_ref[0] * acc
        if mode == "final":
            g_ref, o_ref = refs[4:]
            ms = jnp.mean(x * x, axis=-1, keepdims=True)
            o_ref[...] = x * lax.rsqrt(ms + EPS) * g_ref[...]
        elif mode == "x+h":
            g_ref, sc_ref, sh_ref, x_out, h_out = refs[4:]
            x_out[...] = x
            h_out[...] = _rms_mod(x, g_ref[...], sc_ref[0], sh_ref[0]).astype(BF16)
        else:
            (x_out,) = refs[4:]
            x_out[...] = x

    return kern


def _matmul_residual(a, w_all, layer, res2d, gate, *, seq, tm, glu=False, tiled_a=False,
                     next_norm=None, final_gain=None, name):
    rows, n_out = res2d.shape
    k = w_all.shape[1]
    plan = _Resident(w_all, layer, rows, tm, seq)
    if tiled_a:
        a_spec = pl.BlockSpec((k // LANES, tm, LANES), lambda m: (0, plan.tile(m), 0))
    else:
        a_spec = plan.rows(k)
    row_spec = plan.rows(n_out)
    in_specs = [a_spec, plan.weight(), row_spec, plan.mod_vec(gate)]
    args = [a, w_all, res2d, gate[0]]
    x_shape = jax.ShapeDtypeStruct((rows, n_out), F32)
    if final_gain is not None:
        mode = "final"
        in_specs.append(plan.whole((1, n_out)))
        args.append(final_gain.reshape(1, n_out))
        out_specs, out_shape = row_spec, x_shape
    elif next_norm is not None:
        mode = "x+h"
        gain, scale, shift = next_norm
        in_specs += [plan.gain_vec(gain), plan.mod_vec(scale), plan.mod_vec(shift)]
        args += [gain[0], scale[0], shift[0]]
        out_specs = [row_spec, row_spec]
        out_shape = [x_shape, jax.ShapeDtypeStruct((rows, n_out), BF16)]
    else:
        mode = "x"
        out_specs, out_shape = [row_spec], [x_shape]
    out = pl.pallas_call(
        plan.kernel(_make_mm_res_kernel(glu, tiled_a, n_out, mode), 1),
        grid=plan.grid,
        in_specs=in_specs,
        out_specs=out_specs,
        out_shape=out_shape,
        scratch_shapes=[plan.scratch],
        compiler_params=_params(("arbitrary",)),
        name=name,
    )(*args)
    if mode == "final":
        return out
    return (out[0], out[1]) if mode == "x+h" else (out[0], None)


def _make_s5_prep_kernel(tg, p):
    tc = S5_CHUNK
    half = tg * p

    def kern(lr_ref, li_ref, ldt_ref, br_ref, bi_ref, cr_ref, ci_ref,
             wio_ref, wis_ref, wso_ref, tab_ref):
        lr = lr_ref[...]
        li = li_ref[...]
        dt = jnp.exp(ldt_ref[...])
        mag = jnp.exp(lr * dt)
        ab_re = mag * jnp.cos(li * dt)
        ab_im = mag * jnp.sin(li * dt)
        nr, ni = ab_re - 1.0, ab_im
        den = lr * lr + li * li
        f_re = (nr * lr + ni * li) / den
        f_im = (ni * lr - nr * li) / den
        br, bi = br_ref[...], bi_ref[...]
        gc = br.shape[0]
        bb_re = f_re * br - f_im * bi
        bb_im = f_re * bi + f_im * br
        cr, ci = cr_ref[...], ci_ref[...]
        gid = lax.broadcasted_iota(jnp.int32, (1, half), 1) // p

        def expand(v):
            return jnp.concatenate([jnp.where(gid == g, v, 0.0) for g in range(tg)], axis=0)

        pows = [(jnp.ones_like(ab_re), jnp.zeros_like(ab_im))]
        for _ in range(tc):
            pows.append(_cmul(pows[-1][0], pows[-1][1], ab_re, ab_im))

        for t in range(tc):
            pr, pi = pows[tc - 1 - t]
            wr, wi = _cmul(pr, pi, bb_re, bb_im)
            rows = slice(t * gc, (t + 1) * gc)
            wis_ref[rows, 0:half] = wr.astype(BF16)
            wis_ref[rows, half:2 * half] = wi.astype(BF16)

        def split(v):
            hi = v.astype(BF16)
            return hi, (v - hi.astype(F32)).astype(BF16)

        def nt_dot(a, b):
            return lax.dot_general(a, b, (((1,), (1,)), ((), ())), preferred_element_type=F32)

        b_hi, b_lo = split(jnp.concatenate([expand(bb_re), expand(bb_im)], axis=1))
        q_all = []
        for tau in range(tc + 1):
            pr, pi = pows[tau]
            qr, qi = _cmul(cr, ci, pr, pi)
            er, ei = expand(qr), expand(-qi)
            if tau >= 1:
                cols = slice((tau - 1) * LANES, tau * LANES)
                wso_ref[0:half, cols] = er.T.astype(BF16)
                wso_ref[half:2 * half, cols] = ei.T.astype(BF16)
            if tau < tc:
                q_all.append(jnp.concatenate([er, ei], axis=1))
        q_hi, q_lo = split(jnp.concatenate(q_all, axis=0))
        kt = nt_dot(q_hi, b_hi) + (nt_dot(q_lo, b_hi) + nt_dot(q_hi, b_lo))
        for tau in range(tc):
            wio_ref[tau] = kt[tau * LANES:(tau + 1) * LANES, :].T.astype(BF16)

        lam = [(jnp.ones_like(ab_re), jnp.zeros_like(ab_im))]
        for _ in range(SUBLANES):
            lam.append(_cmul(lam[-1][0], lam[-1][1], pows[tc][0], pows[tc][1]))
        zrow = jnp.zeros_like(ab_re)
        groups = [[lam[r] for r in range(SUBLANES)]]
        for d in (1, 2, 4):
            groups.append([lam[d] if r >= d else (zrow, zrow) for r in range(SUBLANES)])
        groups.append([lam[SUBLANES]] * SUBLANES)
        for gi, grp in enumerate(groups):
            rows = slice(gi * SUBLANES, (gi + 1) * SUBLANES)
            tab_ref[rows, 0:half] = jnp.concatenate([v[0] for v in grp], axis=0)
            tab_ref[rows, half:2 * half] = jnp.concatenate([v[1] for v in grp], axis=0)

    return kern


def _s5_operators(lam_re, lam_im, log_dt, b_re, b_im, c_re, c_im):
    g, p = lam_re.shape
    gc = b_re.shape[-1]
    gp = g * p
    tg = S5_TILE_GROUPS
    assert gc * tg == LANES and g % tg == 0
    nj = g // tg
    half = tg * p
    kdim = S5_CHUNK * LANES
    lr = lam_re.reshape(1, gp)
    li = lam_im.reshape(1, gp)
    ldt = jnp.repeat(log_dt, p).reshape(1, gp)
    brt = b_re.transpose(2, 0, 1).reshape(gc, gp)
    bit = b_im.transpose(2, 0, 1).reshape(gc, gp)
    crt = c_re.transpose(1, 0, 2).reshape(gc, gp)
    cit = c_im.transpose(1, 0, 2).reshape(gc, gp)
    row = pl.BlockSpec((1, half), lambda j: (0, j))
    mat = pl.BlockSpec((gc, half), lambda j: (0, j))
    return pl.pallas_call(
        _make_s5_prep_kernel(tg, p),
        grid=(nj,),
        in_specs=[row, row, row, mat, mat, mat, mat],
        out_specs=[
            pl.BlockSpec((None, S5_CHUNK, LANES, LANES), lambda j: (j, 0, 0, 0)),
            pl.BlockSpec((None, S5_CHUNK * gc, 2 * half), lambda j: (j, 0, 0)),
            pl.BlockSpec((None, 2 * half, kdim), lambda j: (j, 0, 0)),
            pl.BlockSpec((None, 5 * SUBLANES, 2 * half), lambda j: (j, 0, 0)),
        ],
        out_shape=[
            jax.ShapeDtypeStruct((nj, S5_CHUNK, LANES, LANES), BF16),
            jax.ShapeDtypeStruct((nj, S5_CHUNK * gc, 2 * half), BF16),
            jax.ShapeDtypeStruct((nj, 2 * half, kdim), BF16),
            jax.ShapeDtypeStruct((nj, 5 * SUBLANES, 2 * half), F32),
        ],
        compiler_params=_params(("parallel",)),
        name="s5_prep",
    )(lr, li, ldt, brt, bit, crt, cit)


def _make_s5_kernel(nb, chunks_per_batch, half, gc):
    tc = S5_CHUNK
    nchunk = nb * chunks_per_batch
    tiles_per_batch = chunks_per_batch // SUBLANES
    ntile = nchunk // SUBLANES

    def kern(u_ref, wio_ref, wis_ref, wso_ref, tab_ref, d_ref, y_ref, s_scr, y_scr, wio_s, wis_s):
        tg = LANES // gc
        gid = (lax.broadcasted_iota(jnp.int32, (1, 2 * half), 1) % half) // (half // tg)
        zero_rows = jnp.zeros((gc, 2 * half), BF16)
        for t in range(tc):
            blk = wis_ref[t * gc:(t + 1) * gc, :]
            for g in range(tg):
                r0 = (t * tg + g) * gc
                wis_s[r0:r0 + gc, :] = jnp.where(gid == g, blk, zero_rows)
        zero_blk = jnp.zeros((LANES, LANES), BF16)
        for t in range(tc):
            for t2 in range(tc):
                wio_s[t * LANES:(t + 1) * LANES, t2 * LANES:(t2 + 1) * LANES] = (
                    wio_ref[t2 - t] if t2 >= t else zero_blk)

        lhs = jnp.concatenate(
            [u_ref[pl.ds(t, nchunk, stride=tc), :].astype(BF16) for t in range(tc)], axis=1)
        s_scr[...] = _dot(lhs, wis_s[...])

        tab = tab_ref[...]
        pw_r, pw_i = tab[0:8, :half], tab[0:8, half:]
        steps = [(d, tab[8 * i:8 * i + 8, :half], tab[8 * i:8 * i + 8, half:])
                 for i, d in ((1, 1), (2, 2), (3, 4))]
        l8_r, l8_i = tab[32:33, :half], tab[32:33, half:]
        first_row = lax.broadcasted_iota(jnp.int32, (SUBLANES, half), 0) == 0

        for m in range(ntile):
            rows = slice(m * SUBLANES, (m + 1) * SUBLANES)
            s = s_scr[rows, :]
            er, ei = s[:, :half], s[:, half:]
            for d, mr, mi in steps:
                rr = pltpu.roll(er, d, axis=0)
                ri = pltpu.roll(ei, d, axis=0)
                er, ei = er + (mr * rr - mi * ri), ei + (mr * ri + mi * rr)
            hr = jnp.where(first_row, 0.0, pltpu.roll(er, 1, axis=0))
            hi = jnp.where(first_row, 0.0, pltpu.roll(ei, 1, axis=0))
            if m % tiles_per_batch == 0:
                cr, ci = er[7:8], ei[7:8]
            else:
                hr = hr + (pw_r * cr - pw_i * ci)
                hi = hi + (pw_r * ci + pw_i * cr)
                cr, ci = er[7:8] + (l8_r * cr - l8_i * ci), ei[7:8] + (l8_r * ci + l8_i * cr)
            s_scr[rows, :] = jnp.concatenate([hr, hi], axis=1)

        y_io = jnp.concatenate(
            [_dot(lhs[:, :(p + 1) * MXU_N], wio_s[0:(p + 1) * MXU_N, p * MXU_N:(p + 1) * MXU_N])
             for p in range(tc * LANES // MXU_N)], axis=1)
        y = y_io + _dot(s_scr[...].astype(BF16), wso_ref[...])
        for t in range(tc):
            y_scr[pl.ds(t, nchunk, stride=tc), :] = y[:, t * LANES:(t + 1) * LANES]
        y_ref[...] = _gelu(y_scr[...] + d_ref[...] * u_ref[...]).astype(BF16)

    return kern


def _s5_core(u3, ops, d_skip, *, nb, seq):
    w_io, w_is, w_so, tab = ops
    nj, rows, _ = u3.shape
    kdim = S5_CHUNK * LANES
    sdim = w_is.shape[-1]
    chunks_per_batch = seq // S5_CHUNK
    gc = w_is.shape[1] // S5_CHUNK
    assert rows == nb * seq and chunks_per_batch % SUBLANES == 0
    tok = pl.BlockSpec((None, rows, LANES), lambda j: (j, 0, 0))
    return pl.pallas_call(
        _make_s5_kernel(nb, chunks_per_batch, sdim // 2, gc),
        grid=(nj,),
        in_specs=[
            tok,
            pl.BlockSpec((None, S5_CHUNK, LANES, LANES), lambda j: (j, 0, 0, 0)),
            pl.BlockSpec((None, S5_CHUNK * gc, sdim), lambda j: (j, 0, 0)),
            pl.BlockSpec((None, sdim, kdim), lambda j: (j, 0, 0)),
            pl.BlockSpec((None, 5 * SUBLANES, sdim), lambda j: (j, 0, 0)),
            pl.BlockSpec((1, LANES), lambda j: (0, j)),
        ],
        out_specs=tok,
        out_shape=jax.ShapeDtypeStruct((nj, rows, LANES), BF16),
        scratch_shapes=[pltpu.VMEM((rows // S5_CHUNK, sdim), F32), pltpu.VMEM((rows, LANES), F32),
                        pltpu.VMEM((kdim, kdim), BF16), pltpu.VMEM((kdim, sdim), BF16)],
        compiler_params=_params(("parallel",)),
        name="s5_core",
    )(u3, w_io, w_is, w_so, tab, d_skip.reshape(1, nj * LANES))


def _band_windows(width, blk, tn):
    starts, ends = [], []
    for n in range(width // tn):
        h_lo = (n * tn) // blk
        h_hi = (n * tn + tn - 1) // blk
        starts.append((h_lo * blk) // LANES)
        ends.append(-(-((h_hi + 1) * blk) // LANES))
    kw = max(e - s for s, e in zip(starts, ends))
    total = width // LANES
    starts = [min(s, total - kw) for s in starts]
    return starts, kw * LANES


def _banded(w, starts, kw, tn):
    blk = w.shape[-1]
    tiles = []
    for n, s in enumerate(starts):
        pieces = []
        for h in range((n * tn) // blk, (n * tn + tn - 1) // blk + 1):
            j0 = max(0, n * tn - h * blk)
            j1 = min(blk, (n + 1) * tn - h * blk)
            r0 = h * blk - s * LANES
            pieces.append(jnp.pad(w[:, h, :, j0:j1], ((0, 0), (r0, kw - r0 - blk), (0, 0))))
        tiles.append(jnp.concatenate(pieces, axis=2))
    return jnp.stack(tiles, axis=1).astype(BF16)


def _make_lru_kernel(e, tm, tn, starts, kw, tiles_per_seq):
    ntile = e // tn
    nrow = tm // SUBLANES

    def kern(gx_ref, w_ref, vec_ref, y_ref, hcar, xb16):
        @pl.when(pl.program_id(0) % tiles_per_seq == 0)
        def _():
            hcar[...] = jnp.zeros((1, e), F32)

        xb16[...] = gx_ref[:, e:2 * e].astype(BF16)

        row = lax.broadcasted_iota(jnp.int32, (SUBLANES, tn), 0)
        masks = [row % HALF_TILE >= d for d in (1, 2)]
        in_upper = row < HALF_TILE
        neg = -vec_ref[2:3, :]
        softplus = jnp.maximum(neg, 0.0) + jnp.log1p(jnp.exp(-jnp.abs(neg)))
        rate = (-0.5 * LRU_C * LOG2_E) * softplus
        hb_rg = 0.5 * vec_ref[0:1, :]
        hb_ig = 0.5 * vec_ref[1:2, :]

        for n in range(ntile):
            cols = slice(n * tn, (n + 1) * tn)
            win = xb16[:, starts[n] * LANES:starts[n] * LANES + kw]
            tr = jnp.tanh(_dot(win, w_ref[0, n]) + hb_rg[:, cols])
            ig = 0.5 * jnp.tanh(_dot(win, w_ref[1, n]) + hb_ig[:, cols]) + 0.5
            a = jnp.exp2(rate[:, cols] * tr + rate[:, cols])
            om = 1.0 - a * a
            mult = om * lax.rsqrt(jnp.maximum(om, TINY))
            b = mult * (ig * gx_ref[:, e + n * tn:e + (n + 1) * tn])

            c = hcar[:, cols]
            hs = []
            for m in range(nrow):
                rows = slice(m * SUBLANES, (m + 1) * SUBLANES)
                av, bv = a[rows], b[rows]
                for d, keep in zip((1, 2), masks):
                    bv = bv + jnp.where(keep, av * pltpu.roll(bv, d, axis=0), 0.0)
                    av = jnp.where(keep, av * pltpu.roll(av, d, axis=0), av)
                upper = bv + av * c
                lower = bv + av * upper[HALF_TILE - 1:HALF_TILE]
                h = jnp.where(in_upper, upper, lower)
                c = h[SUBLANES - 1:SUBLANES]
                hs.append(h)
            hcar[:, cols] = c
            y_ref[:, cols] = (jnp.concatenate(hs, axis=0) * gx_ref[:, cols]).astype(BF16)

    return kern


def _lru_core(gx, w_rg, b_rg, w_ig, b_ig, lam, *, seq, tm):
    rows, e2 = gx.shape
    e = e2 // 2
    blk = w_rg.shape[1]
    tn = MXU_N
    assert e % tn == 0 and seq % tm == 0 and tm % SUBLANES == 0
    starts, kw = _band_windows(e, blk, tn)
    gates = _banded(0.5 * jnp.stack([w_rg, w_ig]), starts, kw, tn)
    vecs = jnp.stack([b_rg, b_ig, lam])
    ntile = e // tn
    return pl.pallas_call(
        _make_lru_kernel(e, tm, tn, starts, kw, seq // tm),
        grid=(rows // tm,),
        in_specs=[
            pl.BlockSpec((tm, e2), lambda m: (m, 0)),
            pl.BlockSpec((2, ntile, kw, tn), lambda m: (0, 0, 0, 0)),
            pl.BlockSpec((3, e), lambda m: (0, 0)),
        ],
        out_specs=pl.BlockSpec((tm, e), lambda m: (m, 0)),
        out_shape=jax.ShapeDtypeStruct((rows, e), BF16),
        scratch_shapes=[pltpu.VMEM((1, e), F32), pltpu.VMEM((tm, e), BF16)],
        compiler_params=_params(("arbitrary",)),
        name="lru_core",
    )(gx, gates, vecs)


def _tile(n, pref):
    t = min(n, pref)
    while n % t:
        t //= 2
    return t


def kernel(x, c, norm_g, w_ada, b_ada, s5_w_in, s5_lam_re, s5_lam_im, s5_log_dt, s5_b_re, s5_b_im, s5_c_re, s5_c_im, s5_d, s5_w_glu, lru_w_in, lru_conv_w, lru_conv_b, lru_w_rg, lru_b_rg, lru_w_ig, lru_b_ig, lru_lam, lru_w_out, ffn_w_gu, ffn_w_down, final_g):
    bsz, seq, d = x.shape
    depth = w_ada.shape[0]
    rows = bsz * seq
    x2 = x.reshape(rows, d)
    mods = _modulation(c, w_ada, b_ada)
    mods = mods.reshape(depth, mods.shape[1], 6, 1, d)
    gains = norm_g.reshape(depth, 2, 1, d)
    tm_in = _tile(seq, 512)
    tm_res = _tile(seq, 256)
    tm_up = _tile(rows, 1024)
    hidden = ffn_w_down.shape[1]

    h = None
    for i in range(depth):
        sh1, sc1, g1, sh2, sc2, g2 = [(mods, i, q) for q in range(6)]
        ffn_norm = ((gains, i, 1), sc2, sh2)
        j = i // 2
        if i % 2 == 0:
            u3 = _ln_matmul_tiled(x2, (gains, i, 0), sc1, sh1, s5_w_in, j, seq=seq, tm=tm_in,
                                  name="s5_in")
            ops = _s5_operators(s5_lam_re[j], s5_lam_im[j], s5_log_dt[j], s5_b_re[j], s5_b_im[j],
                                s5_c_re[j], s5_c_im[j])
            y3 = _s5_core(u3, ops, s5_d[j], nb=bsz, seq=seq)
            x2, h = _matmul_residual(y3, s5_w_glu, j, x2, g1, seq=seq, tm=tm_res, glu=True,
                                     tiled_a=True, next_norm=ffn_norm, name="s5_out")
        else:
            gx = _lru_in(h, lru_w_in, j, lru_conv_w[j], lru_conv_b[j], seq=seq, tm=tm_res,
                         name="lru_in")
            y = _lru_core(gx, lru_w_rg[j], lru_b_rg[j], lru_w_ig[j], lru_b_ig[j], lru_lam[j],
                          seq=seq, tm=tm_res)
            x2, h = _matmul_residual(y, lru_w_out, j, x2, g1, seq=seq, tm=tm_res,
                                     next_norm=ffn_norm, name="lru_out")
        act = _swiglu_up(h, ffn_w_gu, i, tm=tm_up, tn=_tile(hidden, 512), name="ffn_up")
        if i + 1 == depth:
            out = _matmul_residual(act, ffn_w_down, i, x2, g2, seq=seq, tm=tm_res, final_gain=final_g,
                                   name="ffn_down")
            return out.reshape(bsz, seq, d)
        nxt = None
        if (i + 1) % 2 == 1:
            nxt = ((gains, i + 1, 0), (mods, i + 1, 1), (mods, i + 1, 0))
        x2, h = _matmul_residual(act, ffn_w_down, i, x2, g2, seq=seq, tm=tm_res, next_norm=nxt,
                                 name="ffn_down")
```

```python
import jax
import jax.numpy as jnp
from jax import lax
from jax.experimental import pallas as pl
from jax.experimental.pallas import tpu as pltpu

F32 = jnp.float32
BF16 = jnp.bfloat16

EPS = 1e-6
LRU_C = 8.0
LOG2_E = 1.4426950408889634
TINY = 1e-30
GELU_C1 = 0.7978845608028654
GELU_C3 = 0.7978845608028654 * 0.044715
LANES = 128
SUBLANES = 8
HALF_TILE = SUBLANES // 2
MXU_N = 256
S5_CHUNK = 8
S5_TILE_GROUPS = 8
VMEM_LIMIT = 56 * 1024 * 1024
W_CHUNK_BYTES = 8 * 1024 * 1024


def _params(semantics, vmem=VMEM_LIMIT):
    return pltpu.CompilerParams(dimension_semantics=semantics, vmem_limit_bytes=vmem)


def _dot(a, b):
    return jnp.dot(a, b, preferred_element_type=F32)


def _rms_mod(x, gain, scale, shift):
    ms = jnp.mean(x * x, axis=-1, keepdims=True)
    y = x * lax.rsqrt(ms + EPS) * gain
    return y * (1.0 + scale) + shift


def _gelu(x):
    inner = x * (GELU_C1 + GELU_C3 * (x * x))
    hx = 0.5 * x
    return hx + hx * jnp.tanh(inner)


def _cmul(ar, ai, br, bi):
    return ar * br - ai * bi, ar * bi + ai * br


def _mod_kernel(c_ref, w_ref, b_ref, o_ref):
    c = c_ref[...]
    cond = c * jax.nn.sigmoid(c)
    o_ref[...] = _dot(cond.astype(BF16), w_ref[...].astype(BF16)) + b_ref[...]


def _modulation(c, w_ada, b_ada):
    depth, d, n = w_ada.shape
    b = c.shape[0]
    rows = -(-b // SUBLANES) * SUBLANES
    cp = jnp.zeros((rows, d), F32).at[:b].set(c)
    tn = 1024 if n % 1024 == 0 else n
    out = pl.pallas_call(
        _mod_kernel,
        grid=(depth, n // tn),
        in_specs=[
            pl.BlockSpec((rows, d), lambda i, j: (0, 0)),
            pl.BlockSpec((None, d, tn), lambda i, j: (i, 0, j)),
            pl.BlockSpec((None, 1, tn), lambda i, j: (i, 0, j)),
        ],
        out_specs=pl.BlockSpec((None, rows, tn), lambda i, j: (i, 0, j)),
        out_shape=jax.ShapeDtypeStruct((depth, rows, n), F32),
        compiler_params=_params(("parallel", "parallel")),
        name="adaln_mod",
    )(cp, w_ada, b_ada.reshape(depth, 1, n))
    return out


class _Resident:
    def __init__(self, w_all, layer, rows, tm, seq):
        _, self.k, self.n = w_all.shape
        assert rows % tm == 0 and seq % tm == 0
        ck = self.k
        while ck * self.n * 4 > W_CHUNK_BYTES and ck % 32 == 0:
            ck //= 2
        self.ck, self.nc = ck, self.k // ck
        self.layer, self.tm, self.seq = layer, tm, seq
        self.grid = (self.nc + rows // tm,)
        self.scratch = pltpu.VMEM((self.k, self.n), BF16)

    def tile(self, m):
        return jnp.maximum(m - self.nc, 0)

    def weight(self):
        return pl.BlockSpec((None, self.ck, self.n),
                            lambda m: (self.layer, jnp.minimum(m, self.nc - 1), 0))

    def rows(self, width):
        return pl.BlockSpec((self.tm, width), lambda m: (self.tile(m), 0))

    def whole(self, shape):
        return pl.BlockSpec(shape, lambda m: (0,) * len(shape))

    def mod_vec(self, sel):
        table, layer, which = sel
        return pl.BlockSpec(
            (None, 1, None, 1, table.shape[-1]),
            lambda m: (layer, (self.tile(m) * self.tm) // self.seq, which, 0, 0))

    def gain_vec(self, sel):
        table, i, j = sel
        return pl.BlockSpec((None, None, 1, table.shape[-1]), lambda m: (i, j, 0, 0))

    def kernel(self, body, w_pos):
        nc, ck = self.nc, self.ck

        def kern(*refs):
            w_s = refs[-1]
            refs = list(refs[:-1])
            m = pl.program_id(0)

            @pl.when(m < nc)
            def _():
                w_s[pl.ds(pl.multiple_of(m * ck, ck), ck), :] = refs[w_pos][...].astype(BF16)

            @pl.when(m >= nc)
            def _():
                body(*refs[:w_pos], w_s, *refs[w_pos + 1:])

        return kern


def _ln_mm_kernel(x_ref, g_ref, sc_ref, sh_ref, w_ref, o_ref):
    h = _rms_mod(x_ref[...], g_ref[...], sc_ref[0], sh_ref[0]).astype(BF16)
    acc = _dot(h, w_ref[...])
    for q in range(o_ref.shape[0]):
        o_ref[q] = acc[:, q * LANES:(q + 1) * LANES]


def _ln_matmul_tiled(x2d, gain, scale, shift, w_all, layer, *, seq, tm, name):
    rows, d = x2d.shape
    n_out = w_all.shape[-1]
    assert n_out % LANES == 0
    plan = _Resident(w_all, layer, rows, tm, seq)
    return pl.pallas_call(
        plan.kernel(_ln_mm_kernel, 4),
        grid=plan.grid,
        in_specs=[plan.rows(d), plan.gain_vec(gain), plan.mod_vec(scale), plan.mod_vec(shift),
                  plan.weight()],
        out_specs=pl.BlockSpec((n_out // LANES, tm, LANES), lambda m: (0, plan.tile(m), 0)),
        out_shape=jax.ShapeDtypeStruct((n_out // LANES, rows, LANES), F32),
        scratch_shapes=[plan.scratch],
        compiler_params=_params(("arbitrary",)),
        name=name,
    )(x2d, gain[0], scale[0], shift[0], w_all)


def _make_lru_in_kernel(e, tm, tiles_per_seq, first_step):
    halo = SUBLANES

    def kern(a_ref, w_ref, cw_ref, cb_ref, o_ref, xpad):
        @pl.when((pl.program_id(0) - first_step) % tiles_per_seq == 0)
        def _():
            xpad[0:halo, :] = jnp.zeros((halo, e), F32)

        acc = _dot(a_ref[...], w_ref[...])
        o_ref[:, 0:e] = _gelu(acc[:, 0:e])
        xpad[halo:halo + tm, :] = acc[:, e:2 * e]
        nk = cw_ref.shape[0]
        xc = cb_ref[...] + cw_ref[nk - 1:nk, :] * xpad[halo:halo + tm, :]
        for k in range(nk - 1):
            off = halo - (nk - 1) + k
            xc = xc + cw_ref[k:k + 1, :] * xpad[off:off + tm, :]
        xpad[0:halo, :] = xpad[tm:tm + halo, :]
        o_ref[:, e:2 * e] = xc

    return kern


def _lru_in(a, w_all, layer, conv_w, conv_b, *, seq, tm, name):
    rows, k = a.shape
    e = w_all.shape[-1] // 2
    nk = conv_w.shape[0]
    assert nk - 1 <= SUBLANES
    plan = _Resident(w_all, layer, rows, tm, seq)
    return pl.pallas_call(
        plan.kernel(_make_lru_in_kernel(e, tm, seq // tm, plan.nc), 1),
        grid=plan.grid,
        in_specs=[plan.rows(k), plan.weight(), plan.whole((nk, e)), plan.whole((1, e))],
        out_specs=plan.rows(2 * e),
        out_shape=jax.ShapeDtypeStruct((rows, 2 * e), F32),
        scratch_shapes=[pltpu.VMEM((tm + SUBLANES, e), F32), plan.scratch],
        compiler_params=_params(("arbitrary",)),
        name=name,
    )(a, w_all, conv_w.reshape(nk, e), conv_b.reshape(1, e))


def _swiglu_kernel(h_ref, wg_ref, wu_ref, o_ref, wg_s, wu_s):
    @pl.when(pl.program_id(1) == 0)
    def _():
        wg_s[...] = wg_ref[...].astype(BF16)
        wu_s[...] = wu_ref[...].astype(BF16)

    h = h_ref[...]
    g = _dot(h, wg_s[...])
    o_ref[...] = (g * jax.nn.sigmoid(g) * _dot(h, wu_s[...])).astype(BF16)


def _swiglu_up(h, w_all, layer, *, tm, tn, name):
    rows, d = h.shape
    n_out = w_all.shape[-1] // 2
    assert rows % tm == 0 and n_out % tn == 0
    nt = n_out // tn
    return pl.pallas_call(
        _swiglu_kernel,
        grid=(nt, rows // tm),
        in_specs=[
            pl.BlockSpec((tm, d), lambda n, m: (m, 0)),
            pl.BlockSpec((None, d, tn), lambda n, m: (layer, 0, n)),
            pl.BlockSpec((None, d, tn), lambda n, m: (layer, 0, n + nt)),
        ],
        out_specs=pl.BlockSpec((tm, tn), lambda n, m: (m, n)),
        out_shape=jax.ShapeDtypeStruct((rows, n_out), BF16),
        scratch_shapes=[pltpu.VMEM((d, tn), BF16), pltpu.VMEM((d, tn), BF16)],
        compiler_params=_params(("parallel", "arbitrary")),
        name=name,
    )(h, w_all, w_all)


def _make_mm_res_kernel(glu, tiled_a, n_out, mode):
    def kern(*refs):
        a_ref, w_ref, res_ref, gate_ref = refs[:4]
        if tiled_a:
            a = jnp.concatenate([a_ref[q] for q in range(a_ref.shape[0])], axis=1)
        else:
            a = a_ref[...]
        acc = _dot(a, w_ref[...])
        if glu:
            acc = acc[:, :n_out] * jax.nn.sigmoid(acc[:, n_out:])
        x = res_ref[...] + gate_ref[0] * acc
        if mode == "final":
            g_ref, o_ref = refs[4:]
            ms = jnp.mean(x * x, axis=-1, keepdims=True)
            o_ref[...] = x * lax.rsqrt(ms + EPS) * g_ref[...]
        elif mode == "x+h":
            g_ref, sc_ref, sh_ref, x_out, h_out = refs[4:]
            x_out[...] = x
            h_out[...] = _rms_mod(x, g_ref[...], sc_ref[0], sh_ref[0]).astype(BF16)
        else:
            (x_out,) = refs[4:]
            x_out[...] = x

    return kern


def _matmul_residual(a, w_all, layer, res2d, gate, *, seq, tm, glu=False, tiled_a=False,
                     next_norm=None, final_gain=None, name):
    rows, n_out = res2d.shape
    k = w_all.shape[1]
    plan = _Resident(w_all, layer, rows, tm, seq)
    if tiled_a:
        a_spec = pl.BlockSpec((k // LANES, tm, LANES), lambda m: (0, plan.tile(m), 0))
    else:
        a_spec = plan.rows(k)
    row_spec = plan.rows(n_out)
    in_specs = [a_spec, plan.weight(), row_spec, plan.mod_vec(gate)]
    args = [a, w_all, res2d, gate[0]]
    x_shape = jax.ShapeDtypeStruct((rows, n_out), F32)
    if final_gain is not None:
        mode = "final"
        in_specs.append(plan.whole((1, n_out)))
        args.append(final_gain.reshape(1, n_out))
        out_specs, out_shape = row_spec, x_shape
    elif next_norm is not None:
        mode = "x+h"
        gain, scale, shift = next_norm
        in_specs += [plan.gain_vec(gain), plan.mod_vec(scale), plan.mod_vec(shift)]
        args += [gain[0], scale[0], shift[0]]
        out_specs = [row_spec, row_spec]
        out_shape = [x_shape, jax.ShapeDtypeStruct((rows, n_out), BF16)]
    else:
        mode = "x"
        out_specs, out_shape = [row_spec], [x_shape]
    out = pl.pallas_call(
        plan.kernel(_make_mm_res_kernel(glu, tiled_a, n_out, mode), 1),
        grid=plan.grid,
        in_specs=in_specs,
        out_specs=out_specs,
        out_shape=out_shape,
        scratch_shapes=[plan.scratch],
        compiler_params=_params(("arbitrary",)),
        name=name,
    )(*args)
    if mode == "final":
        return out
    return (out[0], out[1]) if mode == "x+h" else (out[0], None)


def _make_s5_prep_kernel(tg, p):
    tc = S5_CHUNK
    half = tg * p

    def kern(lr_ref, li_ref, ldt_ref, br_ref, bi_ref, cr_ref, ci_ref,
             wio_ref, wis_ref, wso_ref, tab_ref):
        lr = lr_ref[...]
        li = li_ref[...]
        dt = jnp.exp(ldt_ref[...])
        mag = jnp.exp(lr * dt)
        ab_re = mag * jnp.cos(li * dt)
        ab_im = mag * jnp.sin(li * dt)
        nr, ni = ab_re - 1.0, ab_im
        den = lr * lr + li * li
        f_re = (nr * lr + ni * li) / den
        f_im = (ni * lr - nr * li) / den
        br, bi = br_ref[...], bi_ref[...]
        gc = br.shape[0]
        bb_re = f_re * br - f_im * bi
        bb_im = f_re * bi + f_im * br
        cr, ci = cr_ref[...], ci_ref[...]
        gid = lax.broadcasted_iota(jnp.int32, (1, half), 1) // p

        def expand(v):
            return jnp.concatenate([jnp.where(gid == g, v, 0.0) for g in range(tg)], axis=0)

        pows = [(jnp.ones_like(ab_re), jnp.zeros_like(ab_im))]
        for _ in range(tc):
            pows.append(_cmul(pows[-1][0], pows[-1][1], ab_re, ab_im))

        for t in range(tc):
            pr, pi = pows[tc - 1 - t]
            wr, wi = _cmul(pr, pi, bb_re, bb_im)
            rows = slice(t * gc, (t + 1) * gc)
            wis_ref[rows, 0:half] = wr.astype(BF16)
            wis_ref[rows, half:2 * half] = wi.astype(BF16)

        def split(v):
            hi = v.astype(BF16)
            return hi, (v - hi.astype(F32)).astype(BF16)

        def nt_dot(a, b):
            return lax.dot_general(a, b, (((1,), (1,)), ((), ())), preferred_element_type=F32)

        b_hi, b_lo = split(jnp.concatenate([expand(bb_re), expand(bb_im)], axis=1))
        q_all = []
        for tau in range(tc + 1):
            pr, pi = pows[tau]
            qr, qi = _cmul(cr, ci, pr, pi)
            er, ei = expand(qr), expand(-qi)
            if tau >= 1:
                cols = slice((tau - 1) * LANES, tau * LANES)
                wso_ref[0:half, cols] = er.T.astype(BF16)
                wso_ref[half:2 * half, cols] = ei.T.astype(BF16)
            if tau < tc:
                q_all.append(jnp.concatenate([er, ei], axis=1))
        q_hi, q_lo = split(jnp.concatenate(q_all, axis=0))
        kt = nt_dot(q_hi, b_hi) + (nt_dot(q_lo, b_hi) + nt_dot(q_hi, b_lo))
        for tau in range(tc):
            wio_ref[tau] = kt[tau * LANES:(tau + 1) * LANES, :].T.astype(BF16)

        lam = [(jnp.ones_like(ab_re), jnp.zeros_like(ab_im))]
        for _ in range(SUBLANES):
            lam.append(_cmul(lam[-1][0], lam[-1][1], pows[tc][0], pows[tc][1]))
        zrow = jnp.zeros_like(ab_re)
        groups = [[lam[r] for r in range(SUBLANES)]]
        for d in (1, 2, 4):
            groups.append([lam[d] if r >= d else (zrow, zrow) for r in range(SUBLANES)])
        groups.append([lam[SUBLANES]] * SUBLANES)
        for gi, grp in enumerate(groups):
            rows = slice(gi * SUBLANES, (gi + 1) * SUBLANES)
            tab_ref[rows, 0:half] = jnp.concatenate([v[0] for v in grp], axis=0)
            tab_ref[rows, half:2 * half] = jnp.concatenate([v[1] for v in grp], axis=0)

    return kern


def _s5_operators(lam_re, lam_im, log_dt, b_re, b_im, c_re, c_im):
    g, p = lam_re.shape
    gc = b_re.shape[-1]
    gp = g * p
    tg = S5_TILE_GROUPS
    assert gc * tg == LANES and g % tg == 0
    nj = g // tg
    half = tg * p
    kdim = S5_CHUNK * LANES
    lr = lam_re.reshape(1, gp)
    li = lam_im.reshape(1, gp)
    ldt = jnp.repeat(log_dt, p).reshape(1, gp)
    brt = b_re.transpose(2, 0, 1).reshape(gc, gp)
    bit = b_im.transpose(2, 0, 1).reshape(gc, gp)
    crt = c_re.transpose(1, 0, 2).reshape(gc, gp)
    cit = c_im.transpose(1, 0, 2).reshape(gc, gp)
    row = pl.BlockSpec((1, half), lambda j: (0, j))
    mat = pl.BlockSpec((gc, half), lambda j: (0, j))
    return pl.pallas_call(
        _make_s5_prep_kernel(tg, p),
        grid=(nj,),
        in_specs=[row, row, row, mat, mat, mat, mat],
        out_specs=[
            pl.BlockSpec((None, S5_CHUNK, LANES, LANES), lambda j: (j, 0, 0, 0)),
            pl.BlockSpec((None, S5_CHUNK * gc, 2 * half), lambda j: (j, 0, 0)),
            pl.BlockSpec((None, 2 * half, kdim), lambda j: (j, 0, 0)),
            pl.BlockSpec((None, 5 * SUBLANES, 2 * half), lambda j: (j, 0, 0)),
        ],
        out_shape=[
            jax.ShapeDtypeStruct((nj, S5_CHUNK, LANES, LANES), BF16),
            jax.ShapeDtypeStruct((nj, S5_CHUNK * gc, 2 * half), BF16),
            jax.ShapeDtypeStruct((nj, 2 * half, kdim), BF16),
            jax.ShapeDtypeStruct((nj, 5 * SUBLANES, 2 * half), F32),
        ],
        compiler_params=_params(("parallel",)),
        name="s5_prep",
    )(lr, li, ldt, brt, bit, crt, cit)


def _make_s5_kernel(nb, chunks_per_batch, half, gc):
    tc = S5_CHUNK
    nchunk = nb * chunks_per_batch
    tiles_per_batch = chunks_per_batch // SUBLANES
    ntile = nchunk // SUBLANES

    def kern(u_ref, wio_ref, wis_ref, wso_ref, tab_ref, d_ref, y_ref, s_scr, y_scr, wio_s, wis_s):
        tg = LANES // gc
        gid = (lax.broadcasted_iota(jnp.int32, (1, 2 * half), 1) % half) // (half // tg)
        zero_rows = jnp.zeros((gc, 2 * half), BF16)
        for t in range(tc):
            blk = wis_ref[t * gc:(t + 1) * gc, :]
            for g in range(tg):
                r0 = (t * tg + g) * gc
                wis_s[r0:r0 + gc, :] = jnp.where(gid == g, blk, zero_rows)
        zero_blk = jnp.zeros((LANES, LANES), BF16)
        for t in range(tc):
            for t2 in range(tc):
                wio_s[t * LANES:(t + 1) * LANES, t2 * LANES:(t2 + 1) * LANES] = (
                    wio_ref[t2 - t] if t2 >= t else zero_blk)

        lhs = jnp.concatenate(
            [u_ref[pl.ds(t, nchunk, stride=tc), :].astype(BF16) for t in range(tc)], axis=1)
        s_scr[...] = _dot(lhs, wis_s[...])

        tab = tab_ref[...]
        pw_r, pw_i = tab[0:8, :half], tab[0:8, half:]
        steps = [(d, tab[8 * i:8 * i + 8, :half], tab[8 * i:8 * i + 8, half:])
                 for i, d in ((1, 1), (2, 2), (3, 4))]
        l8_r, l8_i = tab[32:33, :half], tab[32:33, half:]
        first_row = lax.broadcasted_iota(jnp.int32, (SUBLANES, half), 0) == 0

        for m in range(ntile):
            rows = slice(m * SUBLANES, (m + 1) * SUBLANES)
            s = s_scr[rows, :]
            er, ei = s[:, :half], s[:, half:]
            for d, mr, mi in steps:
                rr = pltpu.roll(er, d, axis=0)
                ri = pltpu.roll(ei, d, axis=0)
                er, ei = er + (mr * rr - mi * ri), ei + (mr * ri + mi * rr)
            hr = jnp.where(first_row, 0.0, pltpu.roll(er, 1, axis=0))
            hi = jnp.where(first_row, 0.0, pltpu.roll(ei, 1, axis=0))
            if m % tiles_per_batch == 0:
                cr, ci = er[7:8], ei[7:8]
            else:
                hr = hr + (pw_r * cr - pw_i * ci)
                hi = hi + (pw_r * ci + pw_i * cr)
                cr, ci = er[7:8] + (l8_r * cr - l8_i * ci), ei[7:8] + (l8_r * ci + l8_i * cr)
            s_scr[rows, :] = jnp.concatenate([hr, hi], axis=1)

        y_io = jnp.concatenate(
            [_dot(lhs[:, :(p + 1) * MXU_N], wio_s[0:(p + 1) * MXU_N, p * MXU_N:(p + 1) * MXU_N])
             for p in range(tc * LANES // MXU_N)], axis=1)
        y = y_io + _dot(s_scr[...].astype(BF16), wso_ref[...])
        for t in range(tc):
            y_scr[pl.ds(t, nchunk, stride=tc), :] = y[:, t * LANES:(t + 1) * LANES]
        y_ref[...] = _gelu(y_scr[...] + d_ref[...] * u_ref[...]).astype(BF16)

    return kern


def _s5_core(u3, ops, d_skip, *, nb, seq):
    w_io, w_is, w_so, tab = ops
    nj, rows, _ = u3.shape
    kdim = S5_CHUNK * LANES
    sdim = w_is.shape[-1]
    chunks_per_batch = seq // S5_CHUNK
    gc = w_is.shape[1] // S5_CHUNK
    assert rows == nb * seq and chunks_per_batch % SUBLANES == 0
    tok = pl.BlockSpec((None, rows, LANES), lambda j: (j, 0, 0))
    return pl.pallas_call(
        _make_s5_kernel(nb, chunks_per_batch, sdim // 2, gc),
        grid=(nj,),
        in_specs=[
            tok,
            pl.BlockSpec((None, S5_CHUNK, LANES, LANES), lambda j: (j, 0, 0, 0)),
            pl.BlockSpec((None, S5_CHUNK * gc, sdim), lambda j: (j, 0, 0)),
            pl.BlockSpec((None, sdim, kdim), lambda j: (j, 0, 0)),
            pl.BlockSpec((None, 5 * SUBLANES, sdim), lambda j: (j, 0, 0)),
            pl.BlockSpec((1, LANES), lambda j: (0, j)),
        ],
        out_specs=tok,
        out_shape=jax.ShapeDtypeStruct((nj, rows, LANES), BF16),
        scratch_shapes=[pltpu.VMEM((rows // S5_CHUNK, sdim), F32), pltpu.VMEM((rows, LANES), F32),
                        pltpu.VMEM((kdim, kdim), BF16), pltpu.VMEM((kdim, sdim), BF16)],
        compiler_params=_params(("parallel",)),
        name="s5_core",
    )(u3, w_io, w_is, w_so, tab, d_skip.reshape(1, nj * LANES))


def _band_windows(width, blk, tn):
    starts, ends = [], []
    for n in range(width // tn):
        h_lo = (n * tn) // blk
        h_hi = (n * tn + tn - 1) // blk
        starts.append((h_lo * blk) // LANES)
        ends.append(-(-((h_hi + 1) * blk) // LANES))
    kw = max(e - s for s, e in zip(starts, ends))
    total = width // LANES
    starts = [min(s, total - kw) for s in starts]
    return starts, kw * LANES


def _banded(w, starts, kw, tn):
    blk = w.shape[-1]
    tiles = []
    for n, s in enumerate(starts):
        pieces = []
        for h in range((n * tn) // blk, (n * tn + tn - 1) // blk + 1):
            j0 = max(0, n * tn - h * blk)
            j1 = min(blk, (n + 1) * tn - h * blk)
            r0 = h * blk - s * LANES
            pieces.append(jnp.pad(w[:, h, :, j0:j1], ((0, 0), (r0, kw - r0 - blk), (0, 0))))
        tiles.append(jnp.concatenate(pieces, axis=2))
    return jnp.stack(tiles, axis=1).astype(BF16)


def _make_lru_kernel(e, tm, tn, starts, kw, tiles_per_seq):
    ntile = e // tn
    nrow = tm // SUBLANES

    def kern(gx_ref, w_ref, vec_ref, y_ref, hcar, xb16):
        @pl.when(pl.program_id(0) % tiles_per_seq == 0)
        def _():
            hcar[...] = jnp.zeros((1, e), F32)

        xb16[...] = gx_ref[:, e:2 * e].astype(BF16)

        row = lax.broadcasted_iota(jnp.int32, (SUBLANES, tn), 0)
        masks = [row % HALF_TILE >= d for d in (1, 2)]
        in_upper = row < HALF_TILE
        neg = -vec_ref[2:3, :]
        softplus = jnp.maximum(neg, 0.0) + jnp.log1p(jnp.exp(-jnp.abs(neg)))
        rate = (-0.5 * LRU_C * LOG2_E) * softplus
        hb_rg = 0.5 * vec_ref[0:1, :]
        hb_ig = 0.5 * vec_ref[1:2, :]

        for n in range(ntile):
            cols = slice(n * tn, (n + 1) * tn)
            win = xb16[:, starts[n] * LANES:starts[n] * LANES + kw]
            tr = jnp.tanh(_dot(win, w_ref[0, n]) + hb_rg[:, cols])
            ig = 0.5 * jnp.tanh(_dot(win, w_ref[1, n]) + hb_ig[:, cols]) + 0.5
            a = jnp.exp2(rate[:, cols] * tr + rate[:, cols])
            om = 1.0 - a * a
            mult = om * lax.rsqrt(jnp.maximum(om, TINY))
            b = mult * (ig * gx_ref[:, e + n * tn:e + (n + 1) * tn])

            c = hcar[:, cols]
            hs = []
            for m in range(nrow):
                rows = slice(m * SUBLANES, (m + 1) * SUBLANES)
                av, bv = a[rows], b[rows]
                for d, keep in zip((1, 2), masks):
                    bv = bv + jnp.where(keep, av * pltpu.roll(bv, d, axis=0), 0.0)
                    av = jnp.where(keep, av * pltpu.roll(av, d, axis=0), av)
                upper = bv + av * c
                lower = bv + av * upper[HALF_TILE - 1:HALF_TILE]
                h = jnp.where(in_upper, upper, lower)
                c = h[SUBLANES - 1:SUBLANES]
                hs.append(h)
            hcar[:, cols] = c
            y_ref[:, cols] = (jnp.concatenate(hs, axis=0) * gx_ref[:, cols]).astype(BF16)

    return kern


def _lru_core(gx, w_rg, b_rg, w_ig, b_ig, lam, *, seq, tm):
    rows, e2 = gx.shape
    e = e2 // 2
    blk = w_rg.shape[1]
    tn = MXU_N
    assert e % tn == 0 and seq % tm == 0 and tm % SUBLANES == 0
    starts, kw = _band_windows(e, blk, tn)
    gates = _banded(0.5 * jnp.stack([w_rg, w_ig]), starts, kw, tn)
    vecs = jnp.stack([b_rg, b_ig, lam])
    ntile = e // tn
    return pl.pallas_call(
        _make_lru_kernel(e, tm, tn, starts, kw, seq // tm),
        grid=(rows // tm,),
        in_specs=[
            pl.BlockSpec((tm, e2), lambda m: (m, 0)),
            pl.BlockSpec((2, ntile, kw, tn), lambda m: (0, 0, 0, 0)),
            pl.BlockSpec((3, e), lambda m: (0, 0)),
        ],
        out_specs=pl.BlockSpec((tm, e), lambda m: (m, 0)),
        out_shape=jax.ShapeDtypeStruct((rows, e), BF16),
        scratch_shapes=[pltpu.VMEM((1, e), F32), pltpu.VMEM((tm, e), BF16)],
        compiler_params=_params(("arbitrary",)),
        name="lru_core",
    )(gx, gates, vecs)


def _tile(n, pref):
    t = min(n, pref)
    while n % t:
        t //= 2
    return t


def kernel(x, c, norm_g, w_ada, b_ada, s5_w_in, s5_lam_re, s5_lam_im, s5_log_dt, s5_b_re, s5_b_im, s5_c_re, s5_c_im, s5_d, s5_w_glu, lru_w_in, lru_conv_w, lru_conv_b, lru_w_rg, lru_b_rg, lru_w_ig, lru_b_ig, lru_lam, lru_w_out, ffn_w_gu, ffn_w_down, final_g):
    bsz, seq, d = x.shape
    depth = w_ada.shape[0]
    rows = bsz * seq
    x2 = x.reshape(rows, d)
    mods = _modulation(c, w_ada, b_ada)
    mods = mods.reshape(depth, mods.shape[1], 6, 1, d)
    gains = norm_g.reshape(depth, 2, 1, d)
    tm_in = _tile(seq, 512)
    tm_res = _tile(seq, 256)
    tm_up = _tile(rows, 1024)
    hidden = ffn_w_down.shape[1]

    h = None
    for i in range(depth):
        sh1, sc1, g1, sh2, sc2, g2 = [(mods, i, q) for q in range(6)]
        ffn_norm = ((gains, i, 1), sc2, sh2)
        j = i // 2
        if i % 2 == 0:
            u3 = _ln_matmul_tiled(x2, (gains, i, 0), sc1, sh1, s5_w_in, j, seq=seq, tm=tm_in,
                                  name="s5_in")
            ops = _s5_operators(s5_lam_re[j], s5_lam_im[j], s5_log_dt[j], s5_b_re[j], s5_b_im[j],
                                s5_c_re[j], s5_c_im[j])
            y3 = _s5_core(u3, ops, s5_d[j], nb=bsz, seq=seq)
            x2, h = _matmul_residual(y3, s5_w_glu, j, x2, g1, seq=seq, tm=tm_res, glu=True,
                                     tiled_a=True, next_norm=ffn_norm, name="s5_out")
        else:
            gx = _lru_in(h, lru_w_in, j, lru_conv_w[j], lru_conv_b[j], seq=seq, tm=tm_res,
                         name="lru_in")
            y = _lru_core(gx, lru_w_rg[j], lru_b_rg[j], lru_w_ig[j], lru_b_ig[j], lru_lam[j],
                          seq=seq, tm=_tile(seq, 128))
            x2, h = _matmul_residual(y, lru_w_out, j, x2, g1, seq=seq, tm=tm_res,
                                     next_norm=ffn_norm, name="lru_out")
        act = _swiglu_up(h, ffn_w_gu, i, tm=tm_up, tn=_tile(hidden, 512), name="ffn_up")
        if i + 1 == depth:
            out = _matmul_residual(act, ffn_w_down, i, x2, g2, seq=seq, tm=tm_res, final_gain=final_g,
                                   name="ffn_down")
            return out.reshape(bsz, seq, d)
        nxt = None
        if (i + 1) % 2 == 1:
            nxt = ((gains, i + 1, 0), (mods, i + 1, 1), (mods, i + 1, 0))
        x2, h = _matmul_residual(act, ffn_w_down, i, x2, g2, seq=seq, tm=tm_res, next_norm=nxt,
                                 name="ffn_down")
```

```python
import jax
import jax.numpy as jnp
from jax import lax
from jax.experimental import pallas as pl
from jax.experimental.pallas import tpu as pltpu

F32 = jnp.float32
BF16 = jnp.bfloat16

EPS = 1e-6
LRU_C = 8.0
LOG2_E = 1.4426950408889634
TINY = 1e-30
GELU_C1 = 0.7978845608028654
GELU_C3 = 0.7978845608028654 * 0.044715
LANES = 128
SUBLANES = 8
HALF_TILE = SUBLANES // 2
MXU_N = 256
S5_CHUNK = 8
S5_TILE_GROUPS = 8
VMEM_LIMIT = 56 * 1024 * 1024
W_CHUNK_BYTES = 8 * 1024 * 1024


def _params(semantics, vmem=VMEM_LIMIT):
    return pltpu.CompilerParams(dimension_semantics=semantics, vmem_limit_bytes=vmem)


def _dot(a, b):
    return jnp.dot(a, b, preferred_element_type=F32)


def _rms_mod(x, gain, scale, shift):
    ms = jnp.mean(x * x, axis=-1, keepdims=True)
    y = x * lax.rsqrt(ms + EPS) * gain
    return y * (1.0 + scale) + shift


def _gelu(x):
    inner = x * (GELU_C1 + GELU_C3 * (x * x))
    hx = 0.5 * x
    return hx + hx * jnp.tanh(inner)


def _cmul(ar, ai, br, bi):
    return ar * br - ai * bi, ar * bi + ai * br


def _mod_kernel(c_ref, w_ref, b_ref, o_ref):
    c = c_ref[...]
    cond = c * jax.nn.sigmoid(c)
    o_ref[...] = _dot(cond.astype(BF16), w_ref[...].astype(BF16)) + b_ref[...]


def _mod_tile(n):
    return 1024 if n % 1024 == 0 else n


def _mod_operands(c, w_ada, b_ada):
    depth, d, n = w_ada.shape
    b = c.shape[0]
    rows = -(-b // SUBLANES) * SUBLANES
    cp = jnp.zeros((rows, d), F32).at[:b].set(c)
    return cp, w_ada, b_ada.reshape(depth, 1, n)


class _Resident:
    def __init__(self, w_all, layer, rows, tm, seq):
        _, self.k, self.n = w_all.shape
        assert rows % tm == 0 and seq % tm == 0
        ck = self.k
        while ck * self.n * 4 > W_CHUNK_BYTES and ck % 32 == 0:
            ck //= 2
        self.ck, self.nc = ck, self.k // ck
        self.layer, self.tm, self.seq = layer, tm, seq
        self.grid = (self.nc + rows // tm,)
        self.scratch = pltpu.VMEM((self.k, self.n), BF16)

    def tile(self, m):
        return jnp.maximum(m - self.nc, 0)

    def weight(self):
        return pl.BlockSpec((None, self.ck, self.n),
                            lambda m: (self.layer, jnp.minimum(m, self.nc - 1), 0))

    def rows(self, width):
        return pl.BlockSpec((self.tm, width), lambda m: (self.tile(m), 0))

    def whole(self, shape):
        return pl.BlockSpec(shape, lambda m: (0,) * len(shape))

    def mod_vec(self, sel):
        table, layer, which = sel
        return pl.BlockSpec(
            (None, 1, None, 1, table.shape[-1]),
            lambda m: (layer, (self.tile(m) * self.tm) // self.seq, which, 0, 0))

    def gain_vec(self, sel):
        table, i, j = sel
        return pl.BlockSpec((None, None, 1, table.shape[-1]), lambda m: (i, j, 0, 0))

    def kernel(self, body, w_pos):
        nc, ck = self.nc, self.ck

        def kern(*refs):
            w_s = refs[-1]
            refs = list(refs[:-1])
            m = pl.program_id(0)

            @pl.when(m < nc)
            def _():
                w_s[pl.ds(pl.multiple_of(m * ck, ck), ck), :] = refs[w_pos][...].astype(BF16)

            @pl.when(m >= nc)
            def _():
                body(*refs[:w_pos], w_s, *refs[w_pos + 1:])

        return kern


def _ln_mm_kernel(x_ref, g_ref, sc_ref, sh_ref, w_ref, o_ref):
    h = _rms_mod(x_ref[...], g_ref[...], sc_ref[0], sh_ref[0]).astype(BF16)
    acc = _dot(h, w_ref[...])
    for q in range(o_ref.shape[0]):
        o_ref[q] = acc[:, q * LANES:(q + 1) * LANES]


def _ln_matmul_tiled(x2d, gain, scale, shift, w_all, layer, *, seq, tm, name):
    rows, d = x2d.shape
    n_out = w_all.shape[-1]
    assert n_out % LANES == 0
    plan = _Resident(w_all, layer, rows, tm, seq)
    return pl.pallas_call(
        plan.kernel(_ln_mm_kernel, 4),
        grid=plan.grid,
        in_specs=[plan.rows(d), plan.gain_vec(gain), plan.mod_vec(scale), plan.mod_vec(shift),
                  plan.weight()],
        out_specs=pl.BlockSpec((n_out // LANES, tm, LANES), lambda m: (0, plan.tile(m), 0)),
        out_shape=jax.ShapeDtypeStruct((n_out // LANES, rows, LANES), F32),
        scratch_shapes=[plan.scratch],
        compiler_params=_params(("arbitrary",)),
        name=name,
    )(x2d, gain[0], scale[0], shift[0], w_all)


def _make_lru_in_kernel(e, tm, tiles_per_seq, first_step):
    halo = SUBLANES

    def kern(a_ref, w_ref, cw_ref, cb_ref, o_ref, xpad):
        @pl.when((pl.program_id(0) - first_step) % tiles_per_seq == 0)
        def _():
            xpad[0:halo, :] = jnp.zeros((halo, e), F32)

        acc = _dot(a_ref[...], w_ref[...])
        o_ref[:, 0:e] = _gelu(acc[:, 0:e])
        xpad[halo:halo + tm, :] = acc[:, e:2 * e]
        nk = cw_ref.shape[0]
        xc = cb_ref[...] + cw_ref[nk - 1:nk, :] * xpad[halo:halo + tm, :]
        for k in range(nk - 1):
            off = halo - (nk - 1) + k
            xc = xc + cw_ref[k:k + 1, :] * xpad[off:off + tm, :]
        xpad[0:halo, :] = xpad[tm:tm + halo, :]
        o_ref[:, e:2 * e] = xc

    return kern


def _lru_in(a, w_all, layer, conv_w, conv_b, *, seq, tm, name):
    rows, k = a.shape
    e = w_all.shape[-1] // 2
    nk = conv_w.shape[0]
    assert nk - 1 <= SUBLANES
    plan = _Resident(w_all, layer, rows, tm, seq)
    return pl.pallas_call(
        plan.kernel(_make_lru_in_kernel(e, tm, seq // tm, plan.nc), 1),
        grid=plan.grid,
        in_specs=[plan.rows(k), plan.weight(), plan.whole((nk, e)), plan.whole((1, e))],
        out_specs=plan.rows(2 * e),
        out_shape=jax.ShapeDtypeStruct((rows, 2 * e), F32),
        scratch_shapes=[pltpu.VMEM((tm + SUBLANES, e), F32), plan.scratch],
        compiler_params=_params(("arbitrary",)),
        name=name,
    )(a, w_all, conv_w.reshape(nk, e), conv_b.reshape(1, e))


def _swiglu_kernel(h_ref, wg_ref, wu_ref, o_ref, wg_s, wu_s):
    @pl.when(pl.program_id(1) == 0)
    def _():
        wg_s[...] = wg_ref[...].astype(BF16)
        wu_s[...] = wu_ref[...].astype(BF16)

    h = h_ref[...]
    g = _dot(h, wg_s[...])
    o_ref[...] = (g * jax.nn.sigmoid(g) * _dot(h, wu_s[...])).astype(BF16)


def _swiglu_up(h, w_all, layer, *, tm, tn, name):
    rows, d = h.shape
    n_out = w_all.shape[-1] // 2
    assert rows % tm == 0 and n_out % tn == 0
    nt = n_out // tn
    return pl.pallas_call(
        _swiglu_kernel,
        grid=(nt, rows // tm),
        in_specs=[
            pl.BlockSpec((tm, d), lambda n, m: (m, 0)),
            pl.BlockSpec((None, d, tn), lambda n, m: (layer, 0, n)),
            pl.BlockSpec((None, d, tn), lambda n, m: (layer, 0, n + nt)),
        ],
        out_specs=pl.BlockSpec((tm, tn), lambda n, m: (m, n)),
        out_shape=jax.ShapeDtypeStruct((rows, n_out), BF16),
        scratch_shapes=[pltpu.VMEM((d, tn), BF16), pltpu.VMEM((d, tn), BF16)],
        compiler_params=_params(("parallel", "arbitrary")),
        name=name,
    )(h, w_all, w_all)


def _make_mm_res_kernel(glu, tiled_a, n_out, mode):
    def kern(*refs):
        a_ref, w_ref, res_ref, gate_ref = refs[:4]
        if tiled_a:
            a = jnp.concatenate([a_ref[q] for q in range(a_ref.shape[0])], axis=1)
        else:
            a = a_ref[...]
        acc = _dot(a, w_ref[...])
        if glu:
            acc = acc[:, :n_out] * jax.nn.sigmoid(acc[:, n_out:])
        x = res_ref[...] + gate_ref[0] * acc
        if mode == "final":
            g_ref, o_ref = refs[4:]
            ms = jnp.mean(x * x, axis=-1, keepdims=True)
            o_ref[...] = x * lax.rsqrt(ms + EPS) * g_ref[...]
        elif mode == "x+h":
            g_ref, sc_ref, sh_ref, x_out, h_out = refs[4:]
            x_out[...] = x
            h_out[...] = _rms_mod(x, g_ref[...], sc_ref[0], sh_ref[0]).astype(BF16)
        else:
            (x_out,) = refs[4:]
            x_out[...] = x

    return kern


def _matmul_residual(a, w_all, layer, res2d, gate, *, seq, tm, glu=False, tiled_a=False,
                     next_norm=None, final_gain=None, name):
    rows, n_out = res2d.shape
    k = w_all.shape[1]
    plan = _Resident(w_all, layer, rows, tm, seq)
    if tiled_a:
        a_spec = pl.BlockSpec((k // LANES, tm, LANES), lambda m: (0, plan.tile(m), 0))
    else:
        a_spec = plan.rows(k)
    row_spec = plan.rows(n_out)
    in_specs = [a_spec, plan.weight(), row_spec, plan.mod_vec(gate)]
    args = [a, w_all, res2d, gate[0]]
    x_shape = jax.ShapeDtypeStruct((rows, n_out), F32)
    if final_gain is not None:
        mode = "final"
        in_specs.append(plan.whole((1, n_out)))
        args.append(final_gain.reshape(1, n_out))
        out_specs, out_shape = row_spec, x_shape
    elif next_norm is not None:
        mode = "x+h"
        gain, scale, shift = next_norm
        in_specs += [plan.gain_vec(gain), plan.mod_vec(scale), plan.mod_vec(shift)]
        args += [gain[0], scale[0], shift[0]]
        out_specs = [row_spec, row_spec]
        out_shape = [x_shape, jax.ShapeDtypeStruct((rows, n_out), BF16)]
    else:
        mode = "x"
        out_specs, out_shape = [row_spec], [x_shape]
    out = pl.pallas_call(
        plan.kernel(_make_mm_res_kernel(glu, tiled_a, n_out, mode), 1),
        grid=plan.grid,
        in_specs=in_specs,
        out_specs=out_specs,
        out_shape=out_shape,
        scratch_shapes=[plan.scratch],
        compiler_params=_params(("arbitrary",)),
        name=name,
    )(*args)
    if mode == "final":
        return out
    return (out[0], out[1]) if mode == "x+h" else (out[0], None)


def _make_s5_prep_kernel(tg, p):
    tc = S5_CHUNK
    half = tg * p

    def kern(lr_ref, li_ref, ldt_ref, br_ref, bi_ref, cr_ref, ci_ref,
             wio_ref, wis_ref, wso_ref, tab_ref):
        lr = lr_ref[...]
        li = li_ref[...]
        dt = jnp.exp(ldt_ref[...])
        mag = jnp.exp(lr * dt)
        ab_re = mag * jnp.cos(li * dt)
        ab_im = mag * jnp.sin(li * dt)
        nr, ni = ab_re - 1.0, ab_im
        den = lr * lr + li * li
        f_re = (nr * lr + ni * li) / den
        f_im = (ni * lr - nr * li) / den
        br, bi = br_ref[...], bi_ref[...]
        gc = br.shape[0]
        bb_re = f_re * br - f_im * bi
        bb_im = f_re * bi + f_im * br
        cr, ci = cr_ref[...], ci_ref[...]
        gid = lax.broadcasted_iota(jnp.int32, (1, half), 1) // p

        def expand(v):
            return jnp.concatenate([jnp.where(gid == g, v, 0.0) for g in range(tg)], axis=0)

        pows = [(jnp.ones_like(ab_re), jnp.zeros_like(ab_im))]
        for _ in range(tc):
            pows.append(_cmul(pows[-1][0], pows[-1][1], ab_re, ab_im))

        for t in range(tc):
            pr, pi = pows[tc - 1 - t]
            wr, wi = _cmul(pr, pi, bb_re, bb_im)
            rows = slice(t * gc, (t + 1) * gc)
            wis_ref[rows, 0:half] = wr.astype(BF16)
            wis_ref[rows, half:2 * half] = wi.astype(BF16)

        def split(v):
            hi = v.astype(BF16)
            return hi, (v - hi.astype(F32)).astype(BF16)

        def nt_dot(a, b):
            return lax.dot_general(a, b, (((1,), (1,)), ((), ())), preferred_element_type=F32)

        b_hi, b_lo = split(jnp.concatenate([expand(bb_re), expand(bb_im)], axis=1))
        q_all = []
        for tau in range(tc + 1):
            pr, pi = pows[tau]
            qr, qi = _cmul(cr, ci, pr, pi)
            er, ei = expand(qr), expand(-qi)
            if tau >= 1:
                cols = slice((tau - 1) * LANES, tau * LANES)
                wso_ref[0:half, cols] = er.T.astype(BF16)
                wso_ref[half:2 * half, cols] = ei.T.astype(BF16)
            if tau < tc:
                q_all.append(jnp.concatenate([er, ei], axis=1))
        q_hi, q_lo = split(jnp.concatenate(q_all, axis=0))
        kt = nt_dot(q_hi, b_hi) + (nt_dot(q_lo, b_hi) + nt_dot(q_hi, b_lo))
        for tau in range(tc):
            wio_ref[tau] = kt[tau * LANES:(tau + 1) * LANES, :].T.astype(BF16)

        lam = [(jnp.ones_like(ab_re), jnp.zeros_like(ab_im))]
        for _ in range(SUBLANES):
            lam.append(_cmul(lam[-1][0], lam[-1][1], pows[tc][0], pows[tc][1]))
        zrow = jnp.zeros_like(ab_re)
        groups = [[lam[r] for r in range(SUBLANES)]]
        for d in (1, 2, 4):
            groups.append([lam[d] if r >= d else (zrow, zrow) for r in range(SUBLANES)])
        groups.append([lam[SUBLANES]] * SUBLANES)
        for gi, grp in enumerate(groups):
            rows = slice(gi * SUBLANES, (gi + 1) * SUBLANES)
            tab_ref[rows, 0:half] = jnp.concatenate([v[0] for v in grp], axis=0)
            tab_ref[rows, half:2 * half] = jnp.concatenate([v[1] for v in grp], axis=0)

    return kern


def _s5_operators(lam_re, lam_im, log_dt, b_re, b_im, c_re, c_im, modulation=None):
    g, p = lam_re.shape
    gc = b_re.shape[-1]
    gp = g * p
    tg = S5_TILE_GROUPS
    assert gc * tg == LANES and g % tg == 0
    nj = g // tg
    half = tg * p
    kdim = S5_CHUNK * LANES
    lr = lam_re.reshape(1, gp)
    li = lam_im.reshape(1, gp)
    ldt = jnp.repeat(log_dt, p).reshape(1, gp)
    brt = b_re.transpose(2, 0, 1).reshape(gc, gp)
    bit = b_im.transpose(2, 0, 1).reshape(gc, gp)
    crt = c_re.transpose(1, 0, 2).reshape(gc, gp)
    cit = c_im.transpose(1, 0, 2).reshape(gc, gp)
    prep = _make_s5_prep_kernel(tg, p)
    tile = lambda s: jnp.minimum(s, nj - 1)
    row = pl.BlockSpec((1, half), lambda s: (0, tile(s)))
    mat = pl.BlockSpec((gc, half), lambda s: (0, tile(s)))
    in_specs = [row, row, row, mat, mat, mat, mat]
    args = [lr, li, ldt, brt, bit, crt, cit]
    out_specs = [
        pl.BlockSpec((None, S5_CHUNK, LANES, LANES), lambda s: (tile(s), 0, 0, 0)),
        pl.BlockSpec((None, S5_CHUNK * gc, 2 * half), lambda s: (tile(s), 0, 0)),
        pl.BlockSpec((None, 2 * half, kdim), lambda s: (tile(s), 0, 0)),
        pl.BlockSpec((None, 5 * SUBLANES, 2 * half), lambda s: (tile(s), 0, 0)),
    ]
    out_shape = [
        jax.ShapeDtypeStruct((nj, S5_CHUNK, LANES, LANES), BF16),
        jax.ShapeDtypeStruct((nj, S5_CHUNK * gc, 2 * half), BF16),
        jax.ShapeDtypeStruct((nj, 2 * half, kdim), BF16),
        jax.ShapeDtypeStruct((nj, 5 * SUBLANES, 2 * half), F32),
    ]
    if modulation is None:
        return pl.pallas_call(
            prep, grid=(nj,), in_specs=in_specs, out_specs=out_specs, out_shape=out_shape,
            compiler_params=_params(("parallel",)), name="s5_prep",
        )(*args), None

    cp, w_ada, b_ada3 = modulation
    depth, d, n = w_ada.shape
    rows = cp.shape[0]
    tn = _mod_tile(n)
    nt = n // tn
    nmod = depth * nt
    mtile = lambda s: jnp.minimum(s, nmod - 1)

    def kern(c_ref, w_ref, b_ref, *rest):
        s = pl.program_id(0)

        @pl.when(s < nmod)
        def _():
            _mod_kernel(c_ref, w_ref, b_ref, rest[7])

        @pl.when(s < nj)
        def _():
            prep(*rest[:7], *rest[8:])

    out = pl.pallas_call(
        kern,
        grid=(max(nj, nmod),),
        in_specs=[
            pl.BlockSpec((rows, d), lambda s: (0, 0)),
            pl.BlockSpec((None, d, tn), lambda s: (mtile(s) // nt, 0, mtile(s) % nt)),
            pl.BlockSpec((None, 1, tn), lambda s: (mtile(s) // nt, 0, mtile(s) % nt)),
        ] + in_specs,
        out_specs=[pl.BlockSpec((None, rows, tn), lambda s: (mtile(s) // nt, 0, mtile(s) % nt))]
        + out_specs,
        out_shape=[jax.ShapeDtypeStruct((depth, rows, n), F32)] + out_shape,
        compiler_params=_params(("arbitrary",)),
        name="mod_s5_prep",
    )(cp, w_ada, b_ada3, *args)
    return out[1:], out[0]


def _make_s5_kernel(nb, chunks_per_batch, half, gc):
    tc = S5_CHUNK
    nchunk = nb * chunks_per_batch
    tiles_per_batch = chunks_per_batch // SUBLANES
    ntile = nchunk // SUBLANES

    def kern(u_ref, wio_ref, wis_ref, wso_ref, tab_ref, d_ref, y_ref, s_scr, y_scr, wio_s, wis_s):
        tg = LANES // gc
        gid = (lax.broadcasted_iota(jnp.int32, (1, 2 * half), 1) % half) // (half // tg)
        zero_rows = jnp.zeros((gc, 2 * half), BF16)
        for t in range(tc):
            blk = wis_ref[t * gc:(t + 1) * gc, :]
            for g in range(tg):
                r0 = (t * tg + g) * gc
                wis_s[r0:r0 + gc, :] = jnp.where(gid == g, blk, zero_rows)
        zero_blk = jnp.zeros((LANES, LANES), BF16)
        for t in range(tc):
            for t2 in range(tc):
                wio_s[t * LANES:(t + 1) * LANES, t2 * LANES:(t2 + 1) * LANES] = (
                    wio_ref[t2 - t] if t2 >= t else zero_blk)

        lhs = jnp.concatenate(
            [u_ref[pl.ds(t, nchunk, stride=tc), :].astype(BF16) for t in range(tc)], axis=1)
        s_scr[...] = _dot(lhs, wis_s[...])

        tab = tab_ref[...]
        pw_r, pw_i = tab[0:8, :half], tab[0:8, half:]
        steps = [(d, tab[8 * i:8 * i + 8, :half], tab[8 * i:8 * i + 8, half:])
                 for i, d in ((1, 1), (2, 2), (3, 4))]
        l8_r, l8_i = tab[32:33, :half], tab[32:33, half:]
        first_row = lax.broadcasted_iota(jnp.int32, (SUBLANES, half), 0) == 0

        for m in range(ntile):
            rows = slice(m * SUBLANES, (m + 1) * SUBLANES)
            s = s_scr[rows, :]
            er, ei = s[:, :half], s[:, half:]
            for d, mr, mi in steps:
                rr = pltpu.roll(er, d, axis=0)
                ri = pltpu.roll(ei, d, axis=0)
                er, ei = er + (mr * rr - mi * ri), ei + (mr * ri + mi * rr)
            hr = jnp.where(first_row, 0.0, pltpu.roll(er, 1, axis=0))
            hi = jnp.where(first_row, 0.0, pltpu.roll(ei, 1, axis=0))
            if m % tiles_per_batch == 0:
                cr, ci = er[7:8], ei[7:8]
            else:
                hr = hr + (pw_r * cr - pw_i * ci)
                hi = hi + (pw_r * ci + pw_i * cr)
                cr, ci = er[7:8] + (l8_r * cr - l8_i * ci), ei[7:8] + (l8_r * ci + l8_i * cr)
            s_scr[rows, :] = jnp.concatenate([hr, hi], axis=1)

        y_io = jnp.concatenate(
            [_dot(lhs[:, :(p + 1) * MXU_N], wio_s[0:(p + 1) * MXU_N, p * MXU_N:(p + 1) * MXU_N])
             for p in range(tc * LANES // MXU_N)], axis=1)
        y = y_io + _dot(s_scr[...].astype(BF16), wso_ref[...])
        for t in range(tc):
            y_scr[pl.ds(t, nchunk, stride=tc), :] = y[:, t * LANES:(t + 1) * LANES]
        y_ref[...] = _gelu(y_scr[...] + d_ref[...] * u_ref[...]).astype(BF16)

    return kern


def _s5_core(u3, ops, d_skip, *, nb, seq):
    w_io, w_is, w_so, tab = ops
    nj, rows, _ = u3.shape
    kdim = S5_CHUNK * LANES
    sdim = w_is.shape[-1]
    chunks_per_batch = seq // S5_CHUNK
    gc = w_is.shape[1] // S5_CHUNK
    assert rows == nb * seq and chunks_per_batch % SUBLANES == 0
    tok = pl.BlockSpec((None, rows, LANES), lambda j: (j, 0, 0))
    return pl.pallas_call(
        _make_s5_kernel(nb, chunks_per_batch, sdim // 2, gc),
        grid=(nj,),
        in_specs=[
            tok,
            pl.BlockSpec((None, S5_CHUNK, LANES, LANES), lambda j: (j, 0, 0, 0)),
            pl.BlockSpec((None, S5_CHUNK * gc, sdim), lambda j: (j, 0, 0)),
            pl.BlockSpec((None, sdim, kdim), lambda j: (j, 0, 0)),
            pl.BlockSpec((None, 5 * SUBLANES, sdim), lambda j: (j, 0, 0)),
            pl.BlockSpec((1, LANES), lambda j: (0, j)),
        ],
        out_specs=tok,
        out_shape=jax.ShapeDtypeStruct((nj, rows, LANES), BF16),
        scratch_shapes=[pltpu.VMEM((rows // S5_CHUNK, sdim), F32), pltpu.VMEM((rows, LANES), F32),
                        pltpu.VMEM((kdim, kdim), BF16), pltpu.VMEM((kdim, sdim), BF16)],
        compiler_params=_params(("parallel",)),
        name="s5_core",
    )(u3, w_io, w_is, w_so, tab, d_skip.reshape(1, nj * LANES))


def _band_windows(width, blk, tn):
    starts, ends = [], []
    for n in range(width // tn):
        h_lo = (n * tn) // blk
        h_hi = (n * tn + tn - 1) // blk
        starts.append((h_lo * blk) // LANES)
        ends.append(-(-((h_hi + 1) * blk) // LANES))
    kw = max(e - s for s, e in zip(starts, ends))
    total = width // LANES
    starts = [min(s, total - kw) for s in starts]
    return starts, kw * LANES


def _banded(w, starts, kw, tn):
    blk = w.shape[-1]
    tiles = []
    for n, s in enumerate(starts):
        pieces = []
        for h in range((n * tn) // blk, (n * tn + tn - 1) // blk + 1):
            j0 = max(0, n * tn - h * blk)
            j1 = min(blk, (n + 1) * tn - h * blk)
            r0 = h * blk - s * LANES
            pieces.append(jnp.pad(w[:, h, :, j0:j1], ((0, 0), (r0, kw - r0 - blk), (0, 0))))
        tiles.append(jnp.concatenate(pieces, axis=2))
    return jnp.stack(tiles, axis=1).astype(BF16)


def _make_lru_kernel(e, tm, tn, starts, kw, tiles_per_seq):
    ntile = e // tn
    nrow = tm // SUBLANES

    def kern(gx_ref, w_ref, vec_ref, y_ref, hcar, xb16):
        @pl.when(pl.program_id(0) % tiles_per_seq == 0)
        def _():
            hcar[...] = jnp.zeros((1, e), F32)

        xb16[...] = gx_ref[:, e:2 * e].astype(BF16)

        row = lax.broadcasted_iota(jnp.int32, (SUBLANES, tn), 0)
        masks = [row % HALF_TILE >= d for d in (1, 2)]
        in_upper = row < HALF_TILE
        neg = -vec_ref[2:3, :]
        softplus = jnp.maximum(neg, 0.0) + jnp.log1p(jnp.exp(-jnp.abs(neg)))
        rate = (-0.5 * LRU_C * LOG2_E) * softplus
        hb_rg = 0.5 * vec_ref[0:1, :]
        hb_ig = 0.5 * vec_ref[1:2, :]

        for n in range(ntile):
            cols = slice(n * tn, (n + 1) * tn)
            win = xb16[:, starts[n] * LANES:starts[n] * LANES + kw]
            tr = jnp.tanh(_dot(win, w_ref[0, n]) + hb_rg[:, cols])
            ig = 0.5 * jnp.tanh(_dot(win, w_ref[1, n]) + hb_ig[:, cols]) + 0.5
            a = jnp.exp2(rate[:, cols] * tr + rate[:, cols])
            om = 1.0 - a * a
            mult = om * lax.rsqrt(jnp.maximum(om, TINY))
            b = mult * (ig * gx_ref[:, e + n * tn:e + (n + 1) * tn])

            c = hcar[:, cols]
            hs = []
            for m in range(nrow):
                rows = slice(m * SUBLANES, (m + 1) * SUBLANES)
                av, bv = a[rows], b[rows]
                for d, keep in zip((1, 2), masks):
                    bv = bv + jnp.where(keep, av * pltpu.roll(bv, d, axis=0), 0.0)
                    av = jnp.where(keep, av * pltpu.roll(av, d, axis=0), av)
                upper = bv + av * c
                lower = bv + av * upper[HALF_TILE - 1:HALF_TILE]
                h = jnp.where(in_upper, upper, lower)
                c = h[SUBLANES - 1:SUBLANES]
                hs.append(h)
            hcar[:, cols] = c
            y_ref[:, cols] = (jnp.concatenate(hs, axis=0) * gx_ref[:, cols]).astype(BF16)

    return kern


def _lru_core(gx, w_rg, b_rg, w_ig, b_ig, lam, *, seq, tm):
    rows, e2 = gx.shape
    e = e2 // 2
    blk = w_rg.shape[1]
    tn = MXU_N
    assert e % tn == 0 and seq % tm == 0 and tm % SUBLANES == 0
    starts, kw = _band_windows(e, blk, tn)
    gates = _banded(0.5 * jnp.stack([w_rg, w_ig]), starts, kw, tn)
    vecs = jnp.stack([b_rg, b_ig, lam])
    ntile = e // tn
    return pl.pallas_call(
        _make_lru_kernel(e, tm, tn, starts, kw, seq // tm),
        grid=(rows // tm,),
        in_specs=[
            pl.BlockSpec((tm, e2), lambda m: (m, 0)),
            pl.BlockSpec((2, ntile, kw, tn), lambda m: (0, 0, 0, 0)),
            pl.BlockSpec((3, e), lambda m: (0, 0)),
        ],
        out_specs=pl.BlockSpec((tm, e), lambda m: (m, 0)),
        out_shape=jax.ShapeDtypeStruct((rows, e), BF16),
        scratch_shapes=[pltpu.VMEM((1, e), F32), pltpu.VMEM((tm, e), BF16)],
        compiler_params=_params(("arbitrary",)),
        name="lru_core",
    )(gx, gates, vecs)


def _tile(n, pref):
    t = min(n, pref)
    while n % t:
        t //= 2
    return t


def kernel(x, c, norm_g, w_ada, b_ada, s5_w_in, s5_lam_re, s5_lam_im, s5_log_dt, s5_b_re, s5_b_im, s5_c_re, s5_c_im, s5_d, s5_w_glu, lru_w_in, lru_conv_w, lru_conv_b, lru_w_rg, lru_b_rg, lru_w_ig, lru_b_ig, lru_lam, lru_w_out, ffn_w_gu, ffn_w_down, final_g):
    bsz, seq, d = x.shape
    depth = w_ada.shape[0]
    rows = bsz * seq
    x2 = x.reshape(rows, d)
    s5_ops, mods = _s5_operators(s5_lam_re[0], s5_lam_im[0], s5_log_dt[0], s5_b_re[0], s5_b_im[0],
                                 s5_c_re[0], s5_c_im[0],
                                 modulation=_mod_operands(c, w_ada, b_ada))
    mods = mods.reshape(depth, mods.shape[1], 6, 1, d)
    gains = norm_g.reshape(depth, 2, 1, d)
    tm_in = _tile(seq, 512)
    tm_res = _tile(seq, 256)
    tm_up = _tile(rows, 1024)
    hidden = ffn_w_down.shape[1]

    h = None
    for i in range(depth):
        sh1, sc1, g1, sh2, sc2, g2 = [(mods, i, q) for q in range(6)]
        ffn_norm = ((gains, i, 1), sc2, sh2)
        j = i // 2
        if i % 2 == 0:
            u3 = _ln_matmul_tiled(x2, (gains, i, 0), sc1, sh1, s5_w_in, j, seq=seq, tm=tm_in,
                                  name="s5_in")
            if j > 0:
                s5_ops, _ = _s5_operators(s5_lam_re[j], s5_lam_im[j], s5_log_dt[j], s5_b_re[j],
                                          s5_b_im[j], s5_c_re[j], s5_c_im[j])
            y3 = _s5_core(u3, s5_ops, s5_d[j], nb=bsz, seq=seq)
            x2, h = _matmul_residual(y3, s5_w_glu, j, x2, g1, seq=seq, tm=tm_res, glu=True,
                                     tiled_a=True, next_norm=ffn_norm, name="s5_out")
        else:
            gx = _lru_in(h, lru_w_in, j, lru_conv_w[j], lru_conv_b[j], seq=seq, tm=tm_res,
                         name="lru_in")
            y = _lru_core(gx, lru_w_rg[j], lru_b_rg[j], lru_w_ig[j], lru_b_ig[j], lru_lam[j],
                          seq=seq, tm=_tile(seq, 128))
            x2, h = _matmul_residual(y, lru_w_out, j, x2, g1, seq=seq, tm=tm_res,
                                     next_norm=ffn_norm, name="lru_out")
        act = _swiglu_up(h, ffn_w_gu, i, tm=tm_up, tn=_tile(hidden, 512), name="ffn_up")
        if i + 1 == depth:
            out = _matmul_residual(act, ffn_w_down, i, x2, g2, seq=seq, tm=tm_res, final_gain=final_g,
                                   name="ffn_down")
            return out.reshape(bsz, seq, d)
        nxt = None
        if (i + 1) % 2 == 1:
            nxt = ((gains, i + 1, 0), (mods, i + 1, 1), (mods, i + 1, 0))
        x2, h = _matmul_residual(act, ffn_w_down, i, x2, g2, seq=seq, tm=tm_res, next_norm=nxt,
                                 name="ffn_down")
```

```python
import jax
import jax.numpy as jnp
from jax import lax
from jax.experimental import pallas as pl
from jax.experimental.pallas import tpu as pltpu

F32 = jnp.float32
BF16 = jnp.bfloat16

EPS = 1e-6
LRU_C = 8.0
LOG2_E = 1.4426950408889634
TINY = 1e-30
GELU_C1 = 0.7978845608028654
GELU_C3 = 0.7978845608028654 * 0.044715
LANES = 128
SUBLANES = 8
HALF_TILE = SUBLANES // 2
MXU_N = 256
S5_CHUNK = 8
S5_TILE_GROUPS = 8
VMEM_LIMIT = 56 * 1024 * 1024
W_CHUNK_BYTES = 8 * 1024 * 1024


def _params(semantics, vmem=VMEM_LIMIT):
    return pltpu.CompilerParams(dimension_semantics=semantics, vmem_limit_bytes=vmem)


def _dot(a, b):
    return jnp.dot(a, b, preferred_element_type=F32)


def _rms_mod(x, gain, scale, shift):
    ms = jnp.mean(x * x, axis=-1, keepdims=True)
    y = x * lax.rsqrt(ms + EPS) * gain
    return y * (1.0 + scale) + shift


def _gelu(x):
    inner = x * (GELU_C1 + GELU_C3 * (x * x))
    hx = 0.5 * x
    return hx + hx * jnp.tanh(inner)


def _cmul(ar, ai, br, bi):
    return ar * br - ai * bi, ar * bi + ai * br


def _mod_kernel(c_ref, w_ref, b_ref, o_ref):
    c = c_ref[...]
    cond = c * jax.nn.sigmoid(c)
    o_ref[...] = _dot(cond.astype(BF16), w_ref[...].astype(BF16)) + b_ref[...]


def _mod_tile(n):
    return 2048 if n % 2048 == 0 else n


def _mod_operands(c, w_ada, b_ada):
    depth, d, n = w_ada.shape
    b = c.shape[0]
    rows = -(-b // SUBLANES) * SUBLANES
    cp = jnp.zeros((rows, d), F32).at[:b].set(c)
    return cp, w_ada, b_ada.reshape(depth, 1, n)


class _Resident:
    def __init__(self, w_all, layer, rows, tm, seq):
        _, self.k, self.n = w_all.shape
        assert rows % tm == 0 and seq % tm == 0
        ck = self.k
        while ck * self.n * 4 > W_CHUNK_BYTES and ck % 32 == 0:
            ck //= 2
        self.ck, self.nc = ck, self.k // ck
        self.layer, self.tm, self.seq = layer, tm, seq
        self.grid = (self.nc + rows // tm,)
        self.scratch = pltpu.VMEM((self.k, self.n), BF16)

    def tile(self, m):
        return jnp.maximum(m - self.nc, 0)

    def weight(self):
        return pl.BlockSpec((None, self.ck, self.n),
                            lambda m: (self.layer, jnp.minimum(m, self.nc - 1), 0))

    def rows(self, width):
        return pl.BlockSpec((self.tm, width), lambda m: (self.tile(m), 0))

    def whole(self, shape):
        return pl.BlockSpec(shape, lambda m: (0,) * len(shape))

    def mod_vec(self, sel):
        table, layer, which = sel
        return pl.BlockSpec(
            (None, 1, None, 1, table.shape[-1]),
            lambda m: (layer, (self.tile(m) * self.tm) // self.seq, which, 0, 0))

    def gain_vec(self, sel):
        table, i, j = sel
        return pl.BlockSpec((None, None, 1, table.shape[-1]), lambda m: (i, j, 0, 0))

    def kernel(self, body, w_pos):
        nc, ck = self.nc, self.ck

        def kern(*refs):
            w_s = refs[-1]
            refs = list(refs[:-1])
            m = pl.program_id(0)

            @pl.when(m < nc)
            def _():
                w_s[pl.ds(pl.multiple_of(m * ck, ck), ck), :] = refs[w_pos][...].astype(BF16)

            @pl.when(m >= nc)
            def _():
                body(*refs[:w_pos], w_s, *refs[w_pos + 1:])

        return kern


def _ln_mm_kernel(x_ref, g_ref, sc_ref, sh_ref, w_ref, o_ref):
    h = _rms_mod(x_ref[...], g_ref[...], sc_ref[0], sh_ref[0]).astype(BF16)
    acc = _dot(h, w_ref[...])
    for q in range(o_ref.shape[0]):
        o_ref[q] = acc[:, q * LANES:(q + 1) * LANES]


def _ln_matmul_tiled(x2d, gain, scale, shift, w_all, layer, *, seq, tm, name):
    rows, d = x2d.shape
    n_out = w_all.shape[-1]
    assert n_out % LANES == 0
    plan = _Resident(w_all, layer, rows, tm, seq)
    return pl.pallas_call(
        plan.kernel(_ln_mm_kernel, 4),
        grid=plan.grid,
        in_specs=[plan.rows(d), plan.gain_vec(gain), plan.mod_vec(scale), plan.mod_vec(shift),
                  plan.weight()],
        out_specs=pl.BlockSpec((n_out // LANES, tm, LANES), lambda m: (0, plan.tile(m), 0)),
        out_shape=jax.ShapeDtypeStruct((n_out // LANES, rows, LANES), F32),
        scratch_shapes=[plan.scratch],
        compiler_params=_params(("arbitrary",)),
        name=name,
    )(x2d, gain[0], scale[0], shift[0], w_all)


def _make_lru_in_kernel(e, tm, tiles_per_seq, first_step):
    halo = SUBLANES

    def kern(a_ref, w_ref, cw_ref, cb_ref, o_ref, xpad):
        @pl.when((pl.program_id(0) - first_step) % tiles_per_seq == 0)
        def _():
            xpad[0:halo, :] = jnp.zeros((halo, e), F32)

        acc = _dot(a_ref[...], w_ref[...])
        o_ref[:, 0:e] = _gelu(acc[:, 0:e])
        xpad[halo:halo + tm, :] = acc[:, e:2 * e]
        nk = cw_ref.shape[0]
        xc = cb_ref[...] + cw_ref[nk - 1:nk, :] * xpad[halo:halo + tm, :]
        for k in range(nk - 1):
            off = halo - (nk - 1) + k
            xc = xc + cw_ref[k:k + 1, :] * xpad[off:off + tm, :]
        xpad[0:halo, :] = xpad[tm:tm + halo, :]
        o_ref[:, e:2 * e] = xc

    return kern


def _lru_in(a, w_all, layer, conv_w, conv_b, *, seq, tm, name):
    rows, k = a.shape
    e = w_all.shape[-1] // 2
    nk = conv_w.shape[0]
    assert nk - 1 <= SUBLANES
    plan = _Resident(w_all, layer, rows, tm, seq)
    return pl.pallas_call(
        plan.kernel(_make_lru_in_kernel(e, tm, seq // tm, plan.nc), 1),
        grid=plan.grid,
        in_specs=[plan.rows(k), plan.weight(), plan.whole((nk, e)), plan.whole((1, e))],
        out_specs=plan.rows(2 * e),
        out_shape=jax.ShapeDtypeStruct((rows, 2 * e), F32),
        scratch_shapes=[pltpu.VMEM((tm + SUBLANES, e), F32), plan.scratch],
        compiler_params=_params(("arbitrary",)),
        name=name,
    )(a, w_all, conv_w.reshape(nk, e), conv_b.reshape(1, e))


def _swiglu_kernel(h_ref, wg_ref, wu_ref, o_ref, wg_s, wu_s):
    @pl.when(pl.program_id(1) == 0)
    def _():
        wg_s[...] = wg_ref[...].astype(BF16)
        wu_s[...] = wu_ref[...].astype(BF16)

    h = h_ref[...]
    g = _dot(h, wg_s[...])
    o_ref[...] = (g * jax.nn.sigmoid(g) * _dot(h, wu_s[...])).astype(BF16)


def _swiglu_up(h, w_all, layer, *, tm, tn, name):
    rows, d = h.shape
    n_out = w_all.shape[-1] // 2
    assert rows % tm == 0 and n_out % tn == 0
    nt = n_out // tn
    return pl.pallas_call(
        _swiglu_kernel,
        grid=(nt, rows // tm),
        in_specs=[
            pl.BlockSpec((tm, d), lambda n, m: (m, 0)),
            pl.BlockSpec((None, d, tn), lambda n, m: (layer, 0, n)),
            pl.BlockSpec((None, d, tn), lambda n, m: (layer, 0, n + nt)),
        ],
        out_specs=pl.BlockSpec((tm, tn), lambda n, m: (m, n)),
        out_shape=jax.ShapeDtypeStruct((rows, n_out), BF16),
        scratch_shapes=[pltpu.VMEM((d, tn), BF16), pltpu.VMEM((d, tn), BF16)],
        compiler_params=_params(("parallel", "arbitrary")),
        name=name,
    )(h, w_all, w_all)


def _make_mm_res_kernel(glu, tiled_a, n_out, mode):
    def kern(*refs):
        a_ref, w_ref, res_ref, gate_ref = refs[:4]
        if tiled_a:
            a = jnp.concatenate([a_ref[q] for q in range(a_ref.shape[0])], axis=1)
        else:
            a = a_ref[...]
        acc = _dot(a, w_ref[...])
        if glu:
            acc = acc[:, :n_out] * jax.nn.sigmoid(acc[:, n_out:])
        x = res_ref[...] + gate_ref[0] * acc
        if mode == "final":
            g_ref, o_ref = refs[4:]
            ms = jnp.mean(x * x, axis=-1, keepdims=True)
            o_ref[...] = x * lax.rsqrt(ms + EPS) * g_ref[...]
        elif mode == "x+h":
            g_ref, sc_ref, sh_ref, x_out, h_out = refs[4:]
            x_out[...] = x
            h_out[...] = _rms_mod(x, g_ref[...], sc_ref[0], sh_ref[0]).astype(BF16)
        else:
            (x_out,) = refs[4:]
            x_out[...] = x

    return kern


def _matmul_residual(a, w_all, layer, res2d, gate, *, seq, tm, glu=False, tiled_a=False,
                     next_norm=None, final_gain=None, name):
    rows, n_out = res2d.shape
    k = w_all.shape[1]
    plan = _Resident(w_all, layer, rows, tm, seq)
    if tiled_a:
        a_spec = pl.BlockSpec((k // LANES, tm, LANES), lambda m: (0, plan.tile(m), 0))
    else:
        a_spec = plan.rows(k)
    row_spec = plan.rows(n_out)
    in_specs = [a_spec, plan.weight(), row_spec, plan.mod_vec(gate)]
    args = [a, w_all, res2d, gate[0]]
    x_shape = jax.ShapeDtypeStruct((rows, n_out), F32)
    if final_gain is not None:
        mode = "final"
        in_specs.append(plan.whole((1, n_out)))
        args.append(final_gain.reshape(1, n_out))
        out_specs, out_shape = row_spec, x_shape
    elif next_norm is not None:
        mode = "x+h"
        gain, scale, shift = next_norm
        in_specs += [plan.gain_vec(gain), plan.mod_vec(scale), plan.mod_vec(shift)]
        args += [gain[0], scale[0], shift[0]]
        out_specs = [row_spec, row_spec]
        out_shape = [x_shape, jax.ShapeDtypeStruct((rows, n_out), BF16)]
    else:
        mode = "x"
        out_specs, out_shape = [row_spec], [x_shape]
    out = pl.pallas_call(
        plan.kernel(_make_mm_res_kernel(glu, tiled_a, n_out, mode), 1),
        grid=plan.grid,
        in_specs=in_specs,
        out_specs=out_specs,
        out_shape=out_shape,
        scratch_shapes=[plan.scratch],
        compiler_params=_params(("arbitrary",)),
        name=name,
    )(*args)
    if mode == "final":
        return out
    return (out[0], out[1]) if mode == "x+h" else (out[0], None)


def _make_s5_prep_kernel(tg, p):
    tc = S5_CHUNK
    half = tg * p

    def kern(lr_ref, li_ref, ldt_ref, br_ref, bi_ref, cr_ref, ci_ref,
             wio_ref, wis_ref, wso_ref, tab_ref):
        lr = lr_ref[...]
        li = li_ref[...]
        dt = jnp.exp(ldt_ref[...])
        mag = jnp.exp(lr * dt)
        ab_re = mag * jnp.cos(li * dt)
        ab_im = mag * jnp.sin(li * dt)
        nr, ni = ab_re - 1.0, ab_im
        den = lr * lr + li * li
        f_re = (nr * lr + ni * li) / den
        f_im = (ni * lr - nr * li) / den
        br, bi = br_ref[...], bi_ref[...]
        gc = br.shape[0]
        bb_re = f_re * br - f_im * bi
        bb_im = f_re * bi + f_im * br
        cr, ci = cr_ref[...], ci_ref[...]
        gid = lax.broadcasted_iota(jnp.int32, (1, half), 1) // p

        def expand(v):
            return jnp.concatenate([jnp.where(gid == g, v, 0.0) for g in range(tg)], axis=0)

        pows = [(jnp.ones_like(ab_re), jnp.zeros_like(ab_im))]
        for _ in range(tc):
            pows.append(_cmul(pows[-1][0], pows[-1][1], ab_re, ab_im))

        for t in range(tc):
            pr, pi = pows[tc - 1 - t]
            wr, wi = _cmul(pr, pi, bb_re, bb_im)
            rows = slice(t * gc, (t + 1) * gc)
            wis_ref[rows, 0:half] = wr.astype(BF16)
            wis_ref[rows, half:2 * half] = wi.astype(BF16)

        def split(v):
            hi = v.astype(BF16)
            return hi, (v - hi.astype(F32)).astype(BF16)

        def nt_dot(a, b):
            return lax.dot_general(a, b, (((1,), (1,)), ((), ())), preferred_element_type=F32)

        b_hi, b_lo = split(jnp.concatenate([expand(bb_re), expand(bb_im)], axis=1))
        q_all = []
        for tau in range(tc + 1):
            pr, pi = pows[tau]
            qr, qi = _cmul(cr, ci, pr, pi)
            er, ei = expand(qr), expand(-qi)
            if tau >= 1:
                cols = slice((tau - 1) * LANES, tau * LANES)
                wso_ref[0:half, cols] = er.T.astype(BF16)
                wso_ref[half:2 * half, cols] = ei.T.astype(BF16)
            if tau < tc:
                q_all.append(jnp.concatenate([er, ei], axis=1))
        q_hi, q_lo = split(jnp.concatenate(q_all, axis=0))
        kt = nt_dot(q_hi, b_hi) + (nt_dot(q_lo, b_hi) + nt_dot(q_hi, b_lo))
        for tau in range(tc):
            wio_ref[tau] = kt[tau * LANES:(tau + 1) * LANES, :].T.astype(BF16)

        lam = [(jnp.ones_like(ab_re), jnp.zeros_like(ab_im))]
        for _ in range(SUBLANES):
            lam.append(_cmul(lam[-1][0], lam[-1][1], pows[tc][0], pows[tc][1]))
        zrow = jnp.zeros_like(ab_re)
        groups = [[lam[r] for r in range(SUBLANES)]]
        for d in (1, 2, 4):
            groups.append([lam[d] if r >= d else (zrow, zrow) for r in range(SUBLANES)])
        groups.append([lam[SUBLANES]] * SUBLANES)
        for gi, grp in enumerate(groups):
            rows = slice(gi * SUBLANES, (gi + 1) * SUBLANES)
            tab_ref[rows, 0:half] = jnp.concatenate([v[0] for v in grp], axis=0)
            tab_ref[rows, half:2 * half] = jnp.concatenate([v[1] for v in grp], axis=0)

    return kern


def _s5_operators(lam_re, lam_im, log_dt, b_re, b_im, c_re, c_im, modulation=None):
    g, p = lam_re.shape
    gc = b_re.shape[-1]
    gp = g * p
    tg = S5_TILE_GROUPS
    assert gc * tg == LANES and g % tg == 0
    nj = g // tg
    half = tg * p
    kdim = S5_CHUNK * LANES
    lr = lam_re.reshape(1, gp)
    li = lam_im.reshape(1, gp)
    ldt = jnp.repeat(log_dt, p).reshape(1, gp)
    brt = b_re.transpose(2, 0, 1).reshape(gc, gp)
    bit = b_im.transpose(2, 0, 1).reshape(gc, gp)
    crt = c_re.transpose(1, 0, 2).reshape(gc, gp)
    cit = c_im.transpose(1, 0, 2).reshape(gc, gp)
    prep = _make_s5_prep_kernel(tg, p)
    tile = lambda s: jnp.minimum(s, nj - 1)
    row = pl.BlockSpec((1, half), lambda s: (0, tile(s)))
    mat = pl.BlockSpec((gc, half), lambda s: (0, tile(s)))
    in_specs = [row, row, row, mat, mat, mat, mat]
    args = [lr, li, ldt, brt, bit, crt, cit]
    out_specs = [
        pl.BlockSpec((None, S5_CHUNK, LANES, LANES), lambda s: (tile(s), 0, 0, 0)),
        pl.BlockSpec((None, S5_CHUNK * gc, 2 * half), lambda s: (tile(s), 0, 0)),
        pl.BlockSpec((None, 2 * half, kdim), lambda s: (tile(s), 0, 0)),
        pl.BlockSpec((None, 5 * SUBLANES, 2 * half), lambda s: (tile(s), 0, 0)),
    ]
    out_shape = [
        jax.ShapeDtypeStruct((nj, S5_CHUNK, LANES, LANES), BF16),
        jax.ShapeDtypeStruct((nj, S5_CHUNK * gc, 2 * half), BF16),
        jax.ShapeDtypeStruct((nj, 2 * half, kdim), BF16),
        jax.ShapeDtypeStruct((nj, 5 * SUBLANES, 2 * half), F32),
    ]
    if modulation is None:
        return pl.pallas_call(
            prep, grid=(nj,), in_specs=in_specs, out_specs=out_specs, out_shape=out_shape,
            compiler_params=_params(("parallel",)), name="s5_prep",
        )(*args), None

    cp, w_ada, b_ada3 = modulation
    depth, d, n = w_ada.shape
    rows = cp.shape[0]
    tn = _mod_tile(n)
    nt = n // tn
    nmod = depth * nt
    mtile = lambda s: jnp.minimum(s, nmod - 1)

    def kern(c_ref, w_ref, b_ref, *rest):
        s = pl.program_id(0)

        @pl.when(s < nmod)
        def _():
            _mod_kernel(c_ref, w_ref, b_ref, rest[7])

        @pl.when(s < nj)
        def _():
            prep(*rest[:7], *rest[8:])

    out = pl.pallas_call(
        kern,
        grid=(max(nj, nmod),),
        in_specs=[
            pl.BlockSpec((rows, d), lambda s: (0, 0)),
            pl.BlockSpec((None, d, tn), lambda s: (mtile(s) // nt, 0, mtile(s) % nt)),
            pl.BlockSpec((None, 1, tn), lambda s: (mtile(s) // nt, 0, mtile(s) % nt)),
        ] + in_specs,
        out_specs=[pl.BlockSpec((None, rows, tn), lambda s: (mtile(s) // nt, 0, mtile(s) % nt))]
        + out_specs,
        out_shape=[jax.ShapeDtypeStruct((depth, rows, n), F32)] + out_shape,
        compiler_params=_params(("arbitrary",)),
        name="mod_s5_prep",
    )(cp, w_ada, b_ada3, *args)
    return out[1:], out[0]


def _make_s5_kernel(nb, chunks_per_batch, half, gc):
    tc = S5_CHUNK
    nchunk = nb * chunks_per_batch
    tiles_per_batch = chunks_per_batch // SUBLANES
    ntile = nchunk // SUBLANES

    def kern(u_ref, wio_ref, wis_ref, wso_ref, tab_ref, d_ref, y_ref, s_scr, y_scr, wio_s, wis_s):
        tg = LANES // gc
        gid = (lax.broadcasted_iota(jnp.int32, (1, 2 * half), 1) % half) // (half // tg)
        zero_rows = jnp.zeros((gc, 2 * half), BF16)
        for t in range(tc):
            blk = wis_ref[t * gc:(t + 1) * gc, :]
            for g in range(tg):
                r0 = (t * tg + g) * gc
                wis_s[r0:r0 + gc, :] = jnp.where(gid == g, blk, zero_rows)
        zero_blk = jnp.zeros((LANES, LANES), BF16)
        for t in range(tc):
            for t2 in range(tc):
                wio_s[t * LANES:(t + 1) * LANES, t2 * LANES:(t2 + 1) * LANES] = (
                    wio_ref[t2 - t] if t2 >= t else zero_blk)

        lhs = jnp.concatenate(
            [u_ref[pl.ds(t, nchunk, stride=tc), :].astype(BF16) for t in range(tc)], axis=1)
        s_scr[...] = _dot(lhs, wis_s[...])

        tab = tab_ref[...]
        pw_r, pw_i = tab[0:8, :half], tab[0:8, half:]
        steps = [(d, tab[8 * i:8 * i + 8, :half], tab[8 * i:8 * i + 8, half:])
                 for i, d in ((1, 1), (2, 2), (3, 4))]
        l8_r, l8_i = tab[32:33, :half], tab[32:33, half:]
        first_row = lax.broadcasted_iota(jnp.int32, (SUBLANES, half), 0) == 0

        for m in range(ntile):
            rows = slice(m * SUBLANES, (m + 1) * SUBLANES)
            s = s_scr[rows, :]
            er, ei = s[:, :half], s[:, half:]
            for d, mr, mi in steps:
                rr = pltpu.roll(er, d, axis=0)
                ri = pltpu.roll(ei, d, axis=0)
                er, ei = er + (mr * rr - mi * ri), ei + (mr * ri + mi * rr)
            hr = jnp.where(first_row, 0.0, pltpu.roll(er, 1, axis=0))
            hi = jnp.where(first_row, 0.0, pltpu.roll(ei, 1, axis=0))
            if m % tiles_per_batch == 0:
                cr, ci = er[7:8], ei[7:8]
            else:
                hr = hr + (pw_r * cr - pw_i * ci)
                hi = hi + (pw_r * ci + pw_i * cr)
                cr, ci = er[7:8] + (l8_r * cr - l8_i * ci), ei[7:8] + (l8_r * ci + l8_i * cr)
            s_scr[rows, :] = jnp.concatenate([hr, hi], axis=1)

        y_io = jnp.concatenate(
            [_dot(lhs[:, :(p + 1) * MXU_N], wio_s[0:(p + 1) * MXU_N, p * MXU_N:(p + 1) * MXU_N])
             for p in range(tc * LANES // MXU_N)], axis=1)
        y = y_io + _dot(s_scr[...].astype(BF16), wso_ref[...])
        for t in range(tc):
            y_scr[pl.ds(t, nchunk, stride=tc), :] = y[:, t * LANES:(t + 1) * LANES]
        y_ref[...] = _gelu(y_scr[...] + d_ref[...] * u_ref[...]).astype(BF16)

    return kern


def _s5_core(u3, ops, d_skip, *, nb, seq):
    w_io, w_is, w_so, tab = ops
    nj, rows, _ = u3.shape
    kdim = S5_CHUNK * LANES
    sdim = w_is.shape[-1]
    chunks_per_batch = seq // S5_CHUNK
    gc = w_is.shape[1] // S5_CHUNK
    assert rows == nb * seq and chunks_per_batch % SUBLANES == 0
    tok = pl.BlockSpec((None, rows, LANES), lambda j: (j, 0, 0))
    return pl.pallas_call(
        _make_s5_kernel(nb, chunks_per_batch, sdim // 2, gc),
        grid=(nj,),
        in_specs=[
            tok,
            pl.BlockSpec((None, S5_CHUNK, LANES, LANES), lambda j: (j, 0, 0, 0)),
            pl.BlockSpec((None, S5_CHUNK * gc, sdim), lambda j: (j, 0, 0)),
            pl.BlockSpec((None, sdim, kdim), lambda j: (j, 0, 0)),
            pl.BlockSpec((None, 5 * SUBLANES, sdim), lambda j: (j, 0, 0)),
            pl.BlockSpec((1, LANES), lambda j: (0, j)),
        ],
        out_specs=tok,
        out_shape=jax.ShapeDtypeStruct((nj, rows, LANES), BF16),
        scratch_shapes=[pltpu.VMEM((rows // S5_CHUNK, sdim), F32), pltpu.VMEM((rows, LANES), F32),
                        pltpu.VMEM((kdim, kdim), BF16), pltpu.VMEM((kdim, sdim), BF16)],
        compiler_params=_params(("parallel",)),
        name="s5_core",
    )(u3, w_io, w_is, w_so, tab, d_skip.reshape(1, nj * LANES))


def _band_windows(width, blk, tn):
    starts, ends = [], []
    for n in range(width // tn):
        h_lo = (n * tn) // blk
        h_hi = (n * tn + tn - 1) // blk
        starts.append((h_lo * blk) // LANES)
        ends.append(-(-((h_hi + 1) * blk) // LANES))
    kw = max(e - s for s, e in zip(starts, ends))
    total = width // LANES
    starts = [min(s, total - kw) for s in starts]
    return starts, kw * LANES


def _banded(w, starts, kw, tn):
    blk = w.shape[-1]
    tiles = []
    for n, s in enumerate(starts):
        pieces = []
        for h in range((n * tn) // blk, (n * tn + tn - 1) // blk + 1):
            j0 = max(0, n * tn - h * blk)
            j1 = min(blk, (n + 1) * tn - h * blk)
            r0 = h * blk - s * LANES
            pieces.append(jnp.pad(w[:, h, :, j0:j1], ((0, 0), (r0, kw - r0 - blk), (0, 0))))
        tiles.append(jnp.concatenate(pieces, axis=2))
    return jnp.stack(tiles, axis=1).astype(BF16)


def _make_lru_kernel(e, tm, tn, starts, kw, tiles_per_seq):
    ntile = e // tn
    nrow = tm // SUBLANES

    def kern(gx_ref, w_ref, vec_ref, y_ref, hcar, xb16):
        @pl.when(pl.program_id(0) % tiles_per_seq == 0)
        def _():
            hcar[...] = jnp.zeros((1, e), F32)

        xb16[...] = gx_ref[:, e:2 * e].astype(BF16)

        row = lax.broadcasted_iota(jnp.int32, (SUBLANES, tn), 0)
        masks = [row % HALF_TILE >= d for d in (1, 2)]
        in_upper = row < HALF_TILE
        neg = -vec_ref[2:3, :]
        softplus = jnp.maximum(neg, 0.0) + jnp.log1p(jnp.exp(-jnp.abs(neg)))
        rate = (-0.5 * LRU_C * LOG2_E) * softplus
        hb_rg = 0.5 * vec_ref[0:1, :]
        hb_ig = 0.5 * vec_ref[1:2, :]

        for n in range(ntile):
            cols = slice(n * tn, (n + 1) * tn)
            win = xb16[:, starts[n] * LANES:starts[n] * LANES + kw]
            tr = jnp.tanh(_dot(win, w_ref[0, n]) + hb_rg[:, cols])
            ig = 0.5 * jnp.tanh(_dot(win, w_ref[1, n]) + hb_ig[:, cols]) + 0.5
            a = jnp.exp2(rate[:, cols] * tr + rate[:, cols])
            om = 1.0 - a * a
            mult = om * lax.rsqrt(jnp.maximum(om, TINY))
            b = mult * (ig * gx_ref[:, e + n * tn:e + (n + 1) * tn])

            c = hcar[:, cols]
            hs = []
            for m in range(nrow):
                rows = slice(m * SUBLANES, (m + 1) * SUBLANES)
                av, bv = a[rows], b[rows]
                for d, keep in zip((1, 2), masks):
                    bv = bv + jnp.where(keep, av * pltpu.roll(bv, d, axis=0), 0.0)
                    av = jnp.where(keep, av * pltpu.roll(av, d, axis=0), av)
                upper = bv + av * c
                lower = bv + av * upper[HALF_TILE - 1:HALF_TILE]
                h = jnp.where(in_upper, upper, lower)
                c = h[SUBLANES - 1:SUBLANES]
                hs.append(h)
            hcar[:, cols] = c
            y_ref[:, cols] = (jnp.concatenate(hs, axis=0) * gx_ref[:, cols]).astype(BF16)

    return kern


def _lru_core(gx, w_rg, b_rg, w_ig, b_ig, lam, *, seq, tm):
    rows, e2 = gx.shape
    e = e2 // 2
    blk = w_rg.shape[1]
    tn = MXU_N
    assert e % tn == 0 and seq % tm == 0 and tm % SUBLANES == 0
    starts, kw = _band_windows(e, blk, tn)
    gates = _banded(0.5 * jnp.stack([w_rg, w_ig]), starts, kw, tn)
    vecs = jnp.stack([b_rg, b_ig, lam])
    ntile = e // tn
    return pl.pallas_call(
        _make_lru_kernel(e, tm, tn, starts, kw, seq // tm),
        grid=(rows // tm,),
        in_specs=[
            pl.BlockSpec((tm, e2), lambda m: (m, 0)),
            pl.BlockSpec((2, ntile, kw, tn), lambda m: (0, 0, 0, 0)),
            pl.BlockSpec((3, e), lambda m: (0, 0)),
        ],
        out_specs=pl.BlockSpec((tm, e), lambda m: (m, 0)),
        out_shape=jax.ShapeDtypeStruct((rows, e), BF16),
        scratch_shapes=[pltpu.VMEM((1, e), F32), pltpu.VMEM((tm, e), BF16)],
        compiler_params=_params(("arbitrary",)),
        name="lru_core",
    )(gx, gates, vecs)


def _tile(n, pref):
    t = min(n, pref)
    while n % t:
        t //= 2
    return t


def kernel(x, c, norm_g, w_ada, b_ada, s5_w_in, s5_lam_re, s5_lam_im, s5_log_dt, s5_b_re, s5_b_im, s5_c_re, s5_c_im, s5_d, s5_w_glu, lru_w_in, lru_conv_w, lru_conv_b, lru_w_rg, lru_b_rg, lru_w_ig, lru_b_ig, lru_lam, lru_w_out, ffn_w_gu, ffn_w_down, final_g):
    bsz, seq, d = x.shape
    depth = w_ada.shape[0]
    rows = bsz * seq
    x2 = x.reshape(rows, d)
    s5_ops, mods = _s5_operators(s5_lam_re[0], s5_lam_im[0], s5_log_dt[0], s5_b_re[0], s5_b_im[0],
                                 s5_c_re[0], s5_c_im[0],
                                 modulation=_mod_operands(c, w_ada, b_ada))
    mods = mods.reshape(depth, mods.shape[1], 6, 1, d)
    gains = norm_g.reshape(depth, 2, 1, d)
    tm_in = _tile(seq, 512)
    tm_res = _tile(seq, 256)
    tm_up = _tile(rows, 1024)
    hidden = ffn_w_down.shape[1]

    h = None
    for i in range(depth):
        sh1, sc1, g1, sh2, sc2, g2 = [(mods, i, q) for q in range(6)]
        ffn_norm = ((gains, i, 1), sc2, sh2)
        j = i // 2
        if i % 2 == 0:
            u3 = _ln_matmul_tiled(x2, (gains, i, 0), sc1, sh1, s5_w_in, j, seq=seq, tm=tm_in,
                                  name="s5_in")
            if j > 0:
                s5_ops, _ = _s5_operators(s5_lam_re[j], s5_lam_im[j], s5_log_dt[j], s5_b_re[j],
                                          s5_b_im[j], s5_c_re[j], s5_c_im[j])
            y3 = _s5_core(u3, s5_ops, s5_d[j], nb=bsz, seq=seq)
            x2, h = _matmul_residual(y3, s5_w_glu, j, x2, g1, seq=seq, tm=tm_res, glu=True,
                                     tiled_a=True, next_norm=ffn_norm, name="s5_out")
        else:
            gx = _lru_in(h, lru_w_in, j, lru_conv_w[j], lru_conv_b[j], seq=seq, tm=tm_res,
                         name="lru_in")
            y = _lru_core(gx, lru_w_rg[j], lru_b_rg[j], lru_w_ig[j], lru_b_ig[j], lru_lam[j],
                          seq=seq, tm=_tile(seq, 128))
            x2, h = _matmul_residual(y, lru_w_out, j, x2, g1, seq=seq, tm=tm_res,
                                     next_norm=ffn_norm, name="lru_out")
        act = _swiglu_up(h, ffn_w_gu, i, tm=tm_up, tn=_tile(hidden, 512), name="ffn_up")
        if i + 1 == depth:
            out = _matmul_residual(act, ffn_w_down, i, x2, g2, seq=seq, tm=tm_res, final_gain=final_g,
                                   name="ffn_down")
            return out.reshape(bsz, seq, d)
        nxt = None
        if (i + 1) % 2 == 1:
            nxt = ((gains, i + 1, 0), (mods, i + 1, 1), (mods, i + 1, 0))
        x2, h = _matmul_residual(act, ffn_w_down, i, x2, g2, seq=seq, tm=tm_res, next_norm=nxt,
                                 name="ffn_down")
```

```python
import jax
import jax.numpy as jnp
from jax import lax
from jax.experimental import pallas as pl
from jax.experimental.pallas import tpu as pltpu

F32 = jnp.float32
BF16 = jnp.bfloat16

EPS = 1e-6
LRU_C = 8.0
LOG2_E = 1.4426950408889634
TINY = 1e-30
GELU_C1 = 0.7978845608028654
GELU_C3 = 0.7978845608028654 * 0.044715
LANES = 128
SUBLANES = 8
HALF_TILE = SUBLANES // 2
MXU_N = 256
S5_CHUNK = 8
S5_TILE_GROUPS = 8
VMEM_LIMIT = 56 * 1024 * 1024
W_CHUNK_BYTES = 8 * 1024 * 1024


def _params(semantics, vmem=VMEM_LIMIT):
    return pltpu.CompilerParams(dimension_semantics=semantics, vmem_limit_bytes=vmem)


def _dot(a, b):
    return jnp.dot(a, b, preferred_element_type=F32)


def _rms_mod(x, gain, scale, shift):
    ms = jnp.mean(x * x, axis=-1, keepdims=True)
    y = x * lax.rsqrt(ms + EPS) * gain
    return y * (1.0 + scale) + shift


def _gelu(x):
    inner = x * (GELU_C1 + GELU_C3 * (x * x))
    hx = 0.5 * x
    return hx + hx * jnp.tanh(inner)


def _cmul(ar, ai, br, bi):
    return ar * br - ai * bi, ar * bi + ai * br


def _mod_kernel(c_ref, w_ref, b_ref, o_ref):
    c = c_ref[...]
    cond = c * jax.nn.sigmoid(c)
    o_ref[...] = _dot(cond.astype(BF16), w_ref[...].astype(BF16)) + b_ref[...]


def _mod_tile(n):
    return 1024 if n % 1024 == 0 else n


def _mod_operands(c, w_ada, b_ada):
    depth, d, n = w_ada.shape
    b = c.shape[0]
    rows = -(-b // SUBLANES) * SUBLANES
    cp = jnp.zeros((rows, d), F32).at[:b].set(c)
    return cp, w_ada, b_ada.reshape(depth, 1, n)


class _Resident:
    def __init__(self, w_all, layer, rows, tm, seq):
        self.k, self.n = w_all.shape[-2:]
        assert rows % tm == 0 and seq % tm == 0
        self.rounded = w_all.ndim == 2
        ck = self.k
        while not self.rounded and ck * self.n * 4 > W_CHUNK_BYTES and ck % 32 == 0:
            ck //= 2
        self.ck, self.nc = ck, (0 if self.rounded else self.k // ck)
        self.layer, self.tm, self.seq = layer, tm, seq
        self.grid = (self.nc + rows // tm,)
        self.scratch = [] if self.rounded else [pltpu.VMEM((self.k, self.n), BF16)]

    def tile(self, m):
        return jnp.maximum(m - self.nc, 0)

    def weight(self):
        if self.rounded:
            return pl.BlockSpec((self.k, self.n), lambda m: (0, 0), pipeline_mode=pl.Buffered(1))
        return pl.BlockSpec((None, self.ck, self.n),
                            lambda m: (self.layer, jnp.minimum(m, self.nc - 1), 0))

    def rows(self, width):
        return pl.BlockSpec((self.tm, width), lambda m: (self.tile(m), 0))

    def whole(self, shape):
        return pl.BlockSpec(shape, lambda m: (0,) * len(shape))

    def mod_vec(self, sel):
        table, layer, which = sel
        return pl.BlockSpec(
            (None, 1, None, 1, table.shape[-1]),
            lambda m: (layer, (self.tile(m) * self.tm) // self.seq, which, 0, 0))

    def gain_vec(self, sel):
        table, i, j = sel
        return pl.BlockSpec((None, None, 1, table.shape[-1]), lambda m: (i, j, 0, 0))

    def kernel(self, body, w_pos):
        nc, ck = self.nc, self.ck
        if self.rounded:
            return body

        def kern(*refs):
            w_s = refs[-1]
            refs = list(refs[:-1])
            m = pl.program_id(0)

            @pl.when(m < nc)
            def _():
                w_s[pl.ds(pl.multiple_of(m * ck, ck), ck), :] = refs[w_pos][...].astype(BF16)

            @pl.when(m >= nc)
            def _():
                body(*refs[:w_pos], w_s, *refs[w_pos + 1:])

        return kern


def _ln_mm_kernel(x_ref, g_ref, sc_ref, sh_ref, w_ref, o_ref):
    h = _rms_mod(x_ref[...], g_ref[...], sc_ref[0], sh_ref[0]).astype(BF16)
    acc = _dot(h, w_ref[...])
    for q in range(o_ref.shape[0]):
        o_ref[q] = acc[:, q * LANES:(q + 1) * LANES]


def _ln_matmul_tiled(x2d, gain, scale, shift, w_all, layer, *, seq, tm, name):
    rows, d = x2d.shape
    n_out = w_all.shape[-1]
    assert n_out % LANES == 0
    plan = _Resident(w_all, layer, rows, tm, seq)
    return pl.pallas_call(
        plan.kernel(_ln_mm_kernel, 4),
        grid=plan.grid,
        in_specs=[plan.rows(d), plan.gain_vec(gain), plan.mod_vec(scale), plan.mod_vec(shift),
                  plan.weight()],
        out_specs=pl.BlockSpec((n_out // LANES, tm, LANES), lambda m: (0, plan.tile(m), 0)),
        out_shape=jax.ShapeDtypeStruct((n_out // LANES, rows, LANES), F32),
        scratch_shapes=plan.scratch,
        compiler_params=_params(("arbitrary",)),
        name=name,
    )(x2d, gain[0], scale[0], shift[0], w_all)


def _make_lru_in_kernel(e, tm, tiles_per_seq, first_step):
    halo = SUBLANES

    def kern(a_ref, w_ref, cw_ref, cb_ref, o_ref, xpad):
        @pl.when((pl.program_id(0) - first_step) % tiles_per_seq == 0)
        def _():
            xpad[0:halo, :] = jnp.zeros((halo, e), F32)

        acc = _dot(a_ref[...], w_ref[...])
        o_ref[:, 0:e] = _gelu(acc[:, 0:e])
        xpad[halo:halo + tm, :] = acc[:, e:2 * e]
        nk = cw_ref.shape[0]
        xc = cb_ref[...] + cw_ref[nk - 1:nk, :] * xpad[halo:halo + tm, :]
        for k in range(nk - 1):
            off = halo - (nk - 1) + k
            xc = xc + cw_ref[k:k + 1, :] * xpad[off:off + tm, :]
        xpad[0:halo, :] = xpad[tm:tm + halo, :]
        o_ref[:, e:2 * e] = xc

    return kern


def _lru_in(a, w_all, layer, conv_w, conv_b, *, seq, tm, name):
    rows, k = a.shape
    e = w_all.shape[-1] // 2
    nk = conv_w.shape[0]
    assert nk - 1 <= SUBLANES
    plan = _Resident(w_all, layer, rows, tm, seq)
    return pl.pallas_call(
        plan.kernel(_make_lru_in_kernel(e, tm, seq // tm, plan.nc), 1),
        grid=plan.grid,
        in_specs=[plan.rows(k), plan.weight(), plan.whole((nk, e)), plan.whole((1, e))],
        out_specs=plan.rows(2 * e),
        out_shape=jax.ShapeDtypeStruct((rows, 2 * e), F32),
        scratch_shapes=[pltpu.VMEM((tm + SUBLANES, e), F32)] + plan.scratch,
        compiler_params=_params(("arbitrary",)),
        name=name,
    )(a, w_all, conv_w.reshape(nk, e), conv_b.reshape(1, e))


def _make_swiglu_kernel(side):
    def kern(*refs):
        if side:
            h_ref, wg_ref, wu_ref, wn_ref, o_ref, wn_out, wg_s, wu_s = refs
            wn_out[...] = wn_ref[...].astype(BF16)
        else:
            h_ref, wg_ref, wu_ref, o_ref, wg_s, wu_s = refs

        @pl.when(pl.program_id(1) == 0)
        def _():
            wg_s[...] = wg_ref[...].astype(BF16)
            wu_s[...] = wu_ref[...].astype(BF16)

        h = h_ref[...]
        g = _dot(h, wg_s[...])
        o_ref[...] = (g * jax.nn.sigmoid(g) * _dot(h, wu_s[...])).astype(BF16)

    return kern


def _swiglu_up(h, w_all, layer, w_next, *, tm, tn, name):
    rows, d = h.shape
    n_out = w_all.shape[-1] // 2
    assert rows % tm == 0 and n_out % tn == 0
    nt, mt = n_out // tn, rows // tm
    _, k2, n2 = w_next.shape
    slab = k2 // (nt * mt)
    side = slab * nt * mt == k2 and slab % (2 * SUBLANES) == 0
    in_specs = [
        pl.BlockSpec((tm, d), lambda n, m: (m, 0)),
        pl.BlockSpec((None, d, tn), lambda n, m: (layer, 0, n)),
        pl.BlockSpec((None, d, tn), lambda n, m: (layer, 0, n + nt)),
    ]
    args = [h, w_all, w_all]
    out_specs = [pl.BlockSpec((tm, tn), lambda n, m: (m, n))]
    out_shape = [jax.ShapeDtypeStruct((rows, n_out), BF16)]
    if side:
        in_specs.append(pl.BlockSpec((None, slab, n2), lambda n, m: (layer, n * mt + m, 0)))
        args.append(w_next)
        out_specs.append(pl.BlockSpec((slab, n2), lambda n, m: (n * mt + m, 0)))
        out_shape.append(jax.ShapeDtypeStruct((k2, n2), BF16))
    out = pl.pallas_call(
        _make_swiglu_kernel(side),
        grid=(nt, mt),
        in_specs=in_specs,
        out_specs=out_specs,
        out_shape=out_shape,
        scratch_shapes=[pltpu.VMEM((d, tn), BF16), pltpu.VMEM((d, tn), BF16)],
        compiler_params=_params(("parallel", "arbitrary")),
        name=name,
    )(*args)
    return out[0], (out[1] if side else None)


def _make_mm_res_kernel(glu, tiled_a, n_out, mode):
    def kern(*refs):
        a_ref, w_ref, res_ref, gate_ref = refs[:4]
        if tiled_a:
            a = jnp.concatenate([a_ref[q] for q in range(a_ref.shape[0])], axis=1)
        else:
            a = a_ref[...]
        acc = _dot(a, w_ref[...])
        if glu:
            acc = acc[:, :n_out] * jax.nn.sigmoid(acc[:, n_out:])
        x = res_ref[...] + gate_ref[0] * acc
        if mode == "final":
            g_ref, o_ref = refs[4:]
            ms = jnp.mean(x * x, axis=-1, keepdims=True)
            o_ref[...] = x * lax.rsqrt(ms + EPS) * g_ref[...]
        elif mode == "x+h":
            g_ref, sc_ref, sh_ref, x_out, h_out = refs[4:]
            x_out[...] = x
            h_out[...] = _rms_mod(x, g_ref[...], sc_ref[0], sh_ref[0]).astype(BF16)
        else:
            (x_out,) = refs[4:]
            x_out[...] = x

    return kern


def _matmul_residual(a, w_all, layer, res2d, gate, *, seq, tm, glu=False, tiled_a=False,
                     next_norm=None, final_gain=None, name):
    rows, n_out = res2d.shape
    plan = _Resident(w_all, layer, rows, tm, seq)
    k = plan.k
    if tiled_a:
        a_spec = pl.BlockSpec((k // LANES, tm, LANES), lambda m: (0, plan.tile(m), 0))
    else:
        a_spec = plan.rows(k)
    row_spec = plan.rows(n_out)
    in_specs = [a_spec, plan.weight(), row_spec, plan.mod_vec(gate)]
    args = [a, w_all, res2d, gate[0]]
    x_shape = jax.ShapeDtypeStruct((rows, n_out), F32)
    if final_gain is not None:
        mode = "final"
        in_specs.append(plan.whole((1, n_out)))
        args.append(final_gain.reshape(1, n_out))
        out_specs, out_shape = row_spec, x_shape
    elif next_norm is not None:
        mode = "x+h"
        gain, scale, shift = next_norm
        in_specs += [plan.gain_vec(gain), plan.mod_vec(scale), plan.mod_vec(shift)]
        args += [gain[0], scale[0], shift[0]]
        out_specs = [row_spec, row_spec]
        out_shape = [x_shape, jax.ShapeDtypeStruct((rows, n_out), BF16)]
    else:
        mode = "x"
        out_specs, out_shape = [row_spec], [x_shape]
    out = pl.pallas_call(
        plan.kernel(_make_mm_res_kernel(glu, tiled_a, n_out, mode), 1),
        grid=plan.grid,
        in_specs=in_specs,
        out_specs=out_specs,
        out_shape=out_shape,
        scratch_shapes=plan.scratch,
        compiler_params=_params(("arbitrary",)),
        name=name,
    )(*args)
    if mode == "final":
        return out
    return (out[0], out[1]) if mode == "x+h" else (out[0], None)


def _make_s5_prep_kernel(tg, p):
    tc = S5_CHUNK
    half = tg * p

    def kern(lr_ref, li_ref, ldt_ref, br_ref, bi_ref, cr_ref, ci_ref,
             wio_ref, wis_ref, wso_ref, tab_ref):
        lr = lr_ref[...]
        li = li_ref[...]
        dt = jnp.exp(ldt_ref[...])
        mag = jnp.exp(lr * dt)
        ab_re = mag * jnp.cos(li * dt)
        ab_im = mag * jnp.sin(li * dt)
        nr, ni = ab_re - 1.0, ab_im
        den = lr * lr + li * li
        f_re = (nr * lr + ni * li) / den
        f_im = (ni * lr - nr * li) / den
        br, bi = br_ref[...], bi_ref[...]
        gc = br.shape[0]
        bb_re = f_re * br - f_im * bi
        bb_im = f_re * bi + f_im * br
        cr, ci = cr_ref[...], ci_ref[...]
        gid = lax.broadcasted_iota(jnp.int32, (1, half), 1) // p

        def expand(v):
            return jnp.concatenate([jnp.where(gid == g, v, 0.0) for g in range(tg)], axis=0)

        pows = [(jnp.ones_like(ab_re), jnp.zeros_like(ab_im))]
        for _ in range(tc):
            pows.append(_cmul(pows[-1][0], pows[-1][1], ab_re, ab_im))

        for t in range(tc):
            pr, pi = pows[tc - 1 - t]
            wr, wi = _cmul(pr, pi, bb_re, bb_im)
            rows = slice(t * gc, (t + 1) * gc)
            wis_ref[rows, 0:half] = wr.astype(BF16)
            wis_ref[rows, half:2 * half] = wi.astype(BF16)

        def split(v):
            hi = v.astype(BF16)
            return hi, (v - hi.astype(F32)).astype(BF16)

        def nt_dot(a, b):
            return lax.dot_general(a, b, (((1,), (1,)), ((), ())), preferred_element_type=F32)

        b_hi, b_lo = split(jnp.concatenate([expand(bb_re), expand(bb_im)], axis=1))
        q_all = []
        for tau in range(tc + 1):
            pr, pi = pows[tau]
            qr, qi = _cmul(cr, ci, pr, pi)
            er, ei = expand(qr), expand(-qi)
            if tau >= 1:
                cols = slice((tau - 1) * LANES, tau * LANES)
                wso_ref[0:half, cols] = er.T.astype(BF16)
                wso_ref[half:2 * half, cols] = ei.T.astype(BF16)
            if tau < tc:
                q_all.append(jnp.concatenate([er, ei], axis=1))
        q_hi, q_lo = split(jnp.concatenate(q_all, axis=0))
        kt = nt_dot(q_hi, b_hi) + (nt_dot(q_lo, b_hi) + nt_dot(q_hi, b_lo))
        for tau in range(tc):
            wio_ref[tau] = kt[tau * LANES:(tau + 1) * LANES, :].T.astype(BF16)

        lam = [(jnp.ones_like(ab_re), jnp.zeros_like(ab_im))]
        for _ in range(SUBLANES):
            lam.append(_cmul(lam[-1][0], lam[-1][1], pows[tc][0], pows[tc][1]))
        zrow = jnp.zeros_like(ab_re)
        groups = [[lam[r] for r in range(SUBLANES)]]
        for d in (1, 2, 4):
            groups.append([lam[d] if r >= d else (zrow, zrow) for r in range(SUBLANES)])
        groups.append([lam[SUBLANES]] * SUBLANES)
        for gi, grp in enumerate(groups):
            rows = slice(gi * SUBLANES, (gi + 1) * SUBLANES)
            tab_ref[rows, 0:half] = jnp.concatenate([v[0] for v in grp], axis=0)
            tab_ref[rows, half:2 * half] = jnp.concatenate([v[1] for v in grp], axis=0)

    return kern


def _s5_operators(lam_re, lam_im, log_dt, b_re, b_im, c_re, c_im, modulation=None):
    g, p = lam_re.shape
    gc = b_re.shape[-1]
    gp = g * p
    tg = S5_TILE_GROUPS
    assert gc * tg == LANES and g % tg == 0
    nj = g // tg
    half = tg * p
    kdim = S5_CHUNK * LANES
    lr = lam_re.reshape(1, gp)
    li = lam_im.reshape(1, gp)
    ldt = jnp.repeat(log_dt, p).reshape(1, gp)
    brt = b_re.transpose(2, 0, 1).reshape(gc, gp)
    bit = b_im.transpose(2, 0, 1).reshape(gc, gp)
    crt = c_re.transpose(1, 0, 2).reshape(gc, gp)
    cit = c_im.transpose(1, 0, 2).reshape(gc, gp)
    prep = _make_s5_prep_kernel(tg, p)
    tile = lambda s: jnp.minimum(s, nj - 1)
    row = pl.BlockSpec((1, half), lambda s: (0, tile(s)))
    mat = pl.BlockSpec((gc, half), lambda s: (0, tile(s)))
    in_specs = [row, row, row, mat, mat, mat, mat]
    args = [lr, li, ldt, brt, bit, crt, cit]
    out_specs = [
        pl.BlockSpec((None, S5_CHUNK, LANES, LANES), lambda s: (tile(s), 0, 0, 0)),
        pl.BlockSpec((None, S5_CHUNK * gc, 2 * half), lambda s: (tile(s), 0, 0)),
        pl.BlockSpec((None, 2 * half, kdim), lambda s: (tile(s), 0, 0)),
        pl.BlockSpec((None, 5 * SUBLANES, 2 * half), lambda s: (tile(s), 0, 0)),
    ]
    out_shape = [
        jax.ShapeDtypeStruct((nj, S5_CHUNK, LANES, LANES), BF16),
        jax.ShapeDtypeStruct((nj, S5_CHUNK * gc, 2 * half), BF16),
        jax.ShapeDtypeStruct((nj, 2 * half, kdim), BF16),
        jax.ShapeDtypeStruct((nj, 5 * SUBLANES, 2 * half), F32),
    ]
    if modulation is None:
        return pl.pallas_call(
            prep, grid=(nj,), in_specs=in_specs, out_specs=out_specs, out_shape=out_shape,
            compiler_params=_params(("parallel",)), name="s5_prep",
        )(*args), None

    cp, w_ada, b_ada3 = modulation
    depth, d, n = w_ada.shape
    rows = cp.shape[0]
    tn = _mod_tile(n)
    nt = n // tn
    nmod = depth * nt
    mtile = lambda s: jnp.minimum(s, nmod - 1)

    def kern(c_ref, w_ref, b_ref, *rest):
        s = pl.program_id(0)

        @pl.when(s < nmod)
        def _():
            _mod_kernel(c_ref, w_ref, b_ref, rest[7])

        @pl.when(s < nj)
        def _():
            prep(*rest[:7], *rest[8:])

    out = pl.pallas_call(
        kern,
        grid=(max(nj, nmod),),
        in_specs=[
            pl.BlockSpec((rows, d), lambda s: (0, 0)),
            pl.BlockSpec((None, d, tn), lambda s: (mtile(s) // nt, 0, mtile(s) % nt)),
            pl.BlockSpec((None, 1, tn), lambda s: (mtile(s) // nt, 0, mtile(s) % nt)),
        ] + in_specs,
        out_specs=[pl.BlockSpec((None, rows, tn), lambda s: (mtile(s) // nt, 0, mtile(s) % nt))]
        + out_specs,
        out_shape=[jax.ShapeDtypeStruct((depth, rows, n), F32)] + out_shape,
        compiler_params=_params(("arbitrary",)),
        name="mod_s5_prep",
    )(cp, w_ada, b_ada3, *args)
    return out[1:], out[0]


def _make_s5_kernel(nb, chunks_per_batch, half, gc):
    tc = S5_CHUNK
    nchunk = nb * chunks_per_batch
    tiles_per_batch = chunks_per_batch // SUBLANES
    ntile = nchunk // SUBLANES

    def kern(u_ref, wio_ref, wis_ref, wso_ref, tab_ref, d_ref, y_ref, s_scr, y_scr, wio_s, wis_s):
        tg = LANES // gc
        gid = (lax.broadcasted_iota(jnp.int32, (1, 2 * half), 1) % half) // (half // tg)
        zero_rows = jnp.zeros((gc, 2 * half), BF16)
        for t in range(tc):
            blk = wis_ref[t * gc:(t + 1) * gc, :]
            for g in range(tg):
                r0 = (t * tg + g) * gc
                wis_s[r0:r0 + gc, :] = jnp.where(gid == g, blk, zero_rows)
        zero_blk = jnp.zeros((LANES, LANES), BF16)
        for t in range(tc):
            for t2 in range(tc):
                wio_s[t * LANES:(t + 1) * LANES, t2 * LANES:(t2 + 1) * LANES] = (
                    wio_ref[t2 - t] if t2 >= t else zero_blk)

        lhs = jnp.concatenate(
            [u_ref[pl.ds(t, nchunk, stride=tc), :].astype(BF16) for t in range(tc)], axis=1)
        s_scr[...] = _dot(lhs, wis_s[...])

        tab = tab_ref[...]
        pw_r, pw_i = tab[0:8, :half], tab[0:8, half:]
        steps = [(d, tab[8 * i:8 * i + 8, :half], tab[8 * i:8 * i + 8, half:])
                 for i, d in ((1, 1), (2, 2), (3, 4))]
        l8_r, l8_i = tab[32:33, :half], tab[32:33, half:]
        first_row = lax.broadcasted_iota(jnp.int32, (SUBLANES, half), 0) == 0

        for m in range(ntile):
            rows = slice(m * SUBLANES, (m + 1) * SUBLANES)
            s = s_scr[rows, :]
            er, ei = s[:, :half], s[:, half:]
            for d, mr, mi in steps:
                rr = pltpu.roll(er, d, axis=0)
                ri = pltpu.roll(ei, d, axis=0)
                er, ei = er + (mr * rr - mi * ri), ei + (mr * ri + mi * rr)
            hr = jnp.where(first_row, 0.0, pltpu.roll(er, 1, axis=0))
            hi = jnp.where(first_row, 0.0, pltpu.roll(ei, 1, axis=0))
            if m % tiles_per_batch == 0:
                cr, ci = er[7:8], ei[7:8]
            else:
                hr = hr + (pw_r * cr - pw_i * ci)
                hi = hi + (pw_r * ci + pw_i * cr)
                cr, ci = er[7:8] + (l8_r * cr - l8_i * ci), ei[7:8] + (l8_r * ci + l8_i * cr)
            s_scr[rows, :] = jnp.concatenate([hr, hi], axis=1)

        y_io = jnp.concatenate(
            [_dot(lhs[:, :(p + 1) * MXU_N], wio_s[0:(p + 1) * MXU_N, p * MXU_N:(p + 1) * MXU_N])
             for p in range(tc * LANES // MXU_N)], axis=1)
        y = y_io + _dot(s_scr[...].astype(BF16), wso_ref[...])
        for t in range(tc):
            y_scr[pl.ds(t, nchunk, stride=tc), :] = y[:, t * LANES:(t + 1) * LANES]
        y_ref[...] = _gelu(y_scr[...] + d_ref[...] * u_ref[...]).astype(BF16)

    return kern


def _s5_core(u3, ops, d_skip, *, nb, seq):
    w_io, w_is, w_so, tab = ops
    nj, rows, _ = u3.shape
    kdim = S5_CHUNK * LANES
    sdim = w_is.shape[-1]
    chunks_per_batch = seq // S5_CHUNK
    gc = w_is.shape[1] // S5_CHUNK
    assert rows == nb * seq and chunks_per_batch % SUBLANES == 0
    tok = pl.BlockSpec((None, rows, LANES), lambda j: (j, 0, 0))
    return pl.pallas_call(
        _make_s5_kernel(nb, chunks_per_batch, sdim // 2, gc),
        grid=(nj,),
        in_specs=[
            tok,
            pl.BlockSpec((None, S5_CHUNK, LANES, LANES), lambda j: (j, 0, 0, 0)),
            pl.BlockSpec((None, S5_CHUNK * gc, sdim), lambda j: (j, 0, 0)),
            pl.BlockSpec((None, sdim, kdim), lambda j: (j, 0, 0)),
            pl.BlockSpec((None, 5 * SUBLANES, sdim), lambda j: (j, 0, 0)),
            pl.BlockSpec((1, LANES), lambda j: (0, j)),
        ],
        out_specs=tok,
        out_shape=jax.ShapeDtypeStruct((nj, rows, LANES), BF16),
        scratch_shapes=[pltpu.VMEM((rows // S5_CHUNK, sdim), F32), pltpu.VMEM((rows, LANES), F32),
                        pltpu.VMEM((kdim, kdim), BF16), pltpu.VMEM((kdim, sdim), BF16)],
        compiler_params=_params(("parallel",)),
        name="s5_core",
    )(u3, w_io, w_is, w_so, tab, d_skip.reshape(1, nj * LANES))


def _band_windows(width, blk, tn):
    starts, ends = [], []
    for n in range(width // tn):
        h_lo = (n * tn) // blk
        h_hi = (n * tn + tn - 1) // blk
        starts.append((h_lo * blk) // LANES)
        ends.append(-(-((h_hi + 1) * blk) // LANES))
    kw = max(e - s for s, e in zip(starts, ends))
    total = width // LANES
    starts = [min(s, total - kw) for s in starts]
    return starts, kw * LANES


def _banded(w, starts, kw, tn):
    blk = w.shape[-1]
    tiles = []
    for n, s in enumerate(starts):
        pieces = []
        for h in range((n * tn) // blk, (n * tn + tn - 1) // blk + 1):
            j0 = max(0, n * tn - h * blk)
            j1 = min(blk, (n + 1) * tn - h * blk)
            r0 = h * blk - s * LANES
            pieces.append(jnp.pad(w[:, h, :, j0:j1], ((0, 0), (r0, kw - r0 - blk), (0, 0))))
        tiles.append(jnp.concatenate(pieces, axis=2))
    return jnp.stack(tiles, axis=1).astype(BF16)


def _make_lru_kernel(e, tm, tn, starts, kw, tiles_per_seq):
    ntile = e // tn
    nrow = tm // SUBLANES

    def kern(gx_ref, w_ref, vec_ref, y_ref, hcar, xb16):
        @pl.when(pl.program_id(0) % tiles_per_seq == 0)
        def _():
            hcar[...] = jnp.zeros((1, e), F32)

        xb16[...] = gx_ref[:, e:2 * e].astype(BF16)

        row = lax.broadcasted_iota(jnp.int32, (SUBLANES, tn), 0)
        masks = [row % HALF_TILE >= d for d in (1, 2)]
        in_upper = row < HALF_TILE
        neg = -vec_ref[2:3, :]
        softplus = jnp.maximum(neg, 0.0) + jnp.log1p(jnp.exp(-jnp.abs(neg)))
        rate = (-0.5 * LRU_C * LOG2_E) * softplus
        hb_rg = 0.5 * vec_ref[0:1, :]
        hb_ig = 0.5 * vec_ref[1:2, :]

        for n in range(ntile):
            cols = slice(n * tn, (n + 1) * tn)
            win = xb16[:, starts[n] * LANES:starts[n] * LANES + kw]
            tr = jnp.tanh(_dot(win, w_ref[0, n]) + hb_rg[:, cols])
            ig = 0.5 * jnp.tanh(_dot(win, w_ref[1, n]) + hb_ig[:, cols]) + 0.5
            a = jnp.exp2(rate[:, cols] * tr + rate[:, cols])
            om = 1.0 - a * a
            mult = om * lax.rsqrt(jnp.maximum(om, TINY))
            b = mult * (ig * gx_ref[:, e + n * tn:e + (n + 1) * tn])

            c = hcar[:, cols]
            hs = []
            for m in range(nrow):
                rows = slice(m * SUBLANES, (m + 1) * SUBLANES)
                av, bv = a[rows], b[rows]
                for d, keep in zip((1, 2), masks):
                    bv = bv + jnp.where(keep, av * pltpu.roll(bv, d, axis=0), 0.0)
                    av = jnp.where(keep, av * pltpu.roll(av, d, axis=0), av)
                upper = bv + av * c
                lower = bv + av * upper[HALF_TILE - 1:HALF_TILE]
                h = jnp.where(in_upper, upper, lower)
                c = h[SUBLANES - 1:SUBLANES]
                hs.append(h)
            hcar[:, cols] = c
            y_ref[:, cols] = (jnp.concatenate(hs, axis=0) * gx_ref[:, cols]).astype(BF16)

    return kern


def _lru_core(gx, w_rg, b_rg, w_ig, b_ig, lam, *, seq, tm):
    rows, e2 = gx.shape
    e = e2 // 2
    blk = w_rg.shape[1]
    tn = MXU_N
    assert e % tn == 0 and seq % tm == 0 and tm % SUBLANES == 0
    starts, kw = _band_windows(e, blk, tn)
    gates = _banded(0.5 * jnp.stack([w_rg, w_ig]), starts, kw, tn)
    vecs = jnp.stack([b_rg, b_ig, lam])
    ntile = e // tn
    return pl.pallas_call(
        _make_lru_kernel(e, tm, tn, starts, kw, seq // tm),
        grid=(rows // tm,),
        in_specs=[
            pl.BlockSpec((tm, e2), lambda m: (m, 0)),
            pl.BlockSpec((2, ntile, kw, tn), lambda m: (0, 0, 0, 0)),
            pl.BlockSpec((3, e), lambda m: (0, 0)),
        ],
        out_specs=pl.BlockSpec((tm, e), lambda m: (m, 0)),
        out_shape=jax.ShapeDtypeStruct((rows, e), BF16),
        scratch_shapes=[pltpu.VMEM((1, e), F32), pltpu.VMEM((tm, e), BF16)],
        compiler_params=_params(("arbitrary",)),
        name="lru_core",
    )(gx, gates, vecs)


def _tile(n, pref):
    t = min(n, pref)
    while n % t:
        t //= 2
    return t


def kernel(x, c, norm_g, w_ada, b_ada, s5_w_in, s5_lam_re, s5_lam_im, s5_log_dt, s5_b_re, s5_b_im, s5_c_re, s5_c_im, s5_d, s5_w_glu, lru_w_in, lru_conv_w, lru_conv_b, lru_w_rg, lru_b_rg, lru_w_ig, lru_b_ig, lru_lam, lru_w_out, ffn_w_gu, ffn_w_down, final_g):
    bsz, seq, d = x.shape
    depth = w_ada.shape[0]
    rows = bsz * seq
    x2 = x.reshape(rows, d)
    s5_ops, mods = _s5_operators(s5_lam_re[0], s5_lam_im[0], s5_log_dt[0], s5_b_re[0], s5_b_im[0],
                                 s5_c_re[0], s5_c_im[0],
                                 modulation=_mod_operands(c, w_ada, b_ada))
    mods = mods.reshape(depth, mods.shape[1], 6, 1, d)
    gains = norm_g.reshape(depth, 2, 1, d)
    tm_in = _tile(seq, 512)
    tm_res = _tile(seq, 256)
    tm_up = _tile(rows, 1024)
    hidden = ffn_w_down.shape[1]

    h = None
    for i in range(depth):
        sh1, sc1, g1, sh2, sc2, g2 = [(mods, i, q) for q in range(6)]
        ffn_norm = ((gains, i, 1), sc2, sh2)
        j = i // 2
        if i % 2 == 0:
            u3 = _ln_matmul_tiled(x2, (gains, i, 0), sc1, sh1, s5_w_in, j, seq=seq, tm=tm_in,
                                  name="s5_in")
            if j > 0:
                s5_ops, _ = _s5_operators(s5_lam_re[j], s5_lam_im[j], s5_log_dt[j], s5_b_re[j],
                                          s5_b_im[j], s5_c_re[j], s5_c_im[j])
            y3 = _s5_core(u3, s5_ops, s5_d[j], nb=bsz, seq=seq)
            x2, h = _matmul_residual(y3, s5_w_glu, j, x2, g1, seq=seq, tm=tm_res, glu=True,
                                     tiled_a=True, next_norm=ffn_norm, name="s5_out")
        else:
            gx = _lru_in(h, lru_w_in, j, lru_conv_w[j], lru_conv_b[j], seq=seq, tm=tm_res,
                         name="lru_in")
            y = _lru_core(gx, lru_w_rg[j], lru_b_rg[j], lru_w_ig[j], lru_b_ig[j], lru_lam[j],
                          seq=seq, tm=_tile(seq, 128))
            x2, h = _matmul_residual(y, lru_w_out, j, x2, g1, seq=seq, tm=tm_res,
                                     next_norm=ffn_norm, name="lru_out")
        act, w_down = _swiglu_up(h, ffn_w_gu, i, ffn_w_down, tm=tm_up, tn=_tile(hidden, 512),
                                 name="ffn_up")
        if w_down is None:
            w_down = ffn_w_down
        if i + 1 == depth:
            out = _matmul_residual(act, w_down, i, x2, g2, seq=seq, tm=tm_res, final_gain=final_g,
                                   name="ffn_down")
            return out.reshape(bsz, seq, d)
        nxt = None
        if (i + 1) % 2 == 1:
            nxt = ((gains, i + 1, 0), (mods, i + 1, 1), (mods, i + 1, 0))
        x2, h = _matmul_residual(act, w_down, i, x2, g2, seq=seq, tm=tm_res, next_norm=nxt,
                                 name="ffn_down")
```

```python
import jax
import jax.numpy as jnp
from jax import lax
from jax.experimental import pallas as pl
from jax.experimental.pallas import tpu as pltpu

F32 = jnp.float32
BF16 = jnp.bfloat16

EPS = 1e-6
LRU_C = 8.0
LOG2_E = 1.4426950408889634
TINY = 1e-30
GELU_C1 = 0.7978845608028654
GELU_C3 = 0.7978845608028654 * 0.044715
LANES = 128
SUBLANES = 8
HALF_TILE = SUBLANES // 2
MXU_N = 256
S5_CHUNK = 8
S5_TILE_GROUPS = 8
VMEM_LIMIT = 56 * 1024 * 1024
W_CHUNK_BYTES = 8 * 1024 * 1024


def _params(semantics, vmem=VMEM_LIMIT):
    return pltpu.CompilerParams(dimension_semantics=semantics, vmem_limit_bytes=vmem)


def _dot(a, b):
    return jnp.dot(a, b, preferred_element_type=F32)


def _rms_mod(x, gain, scale, shift):
    ms = jnp.mean(x * x, axis=-1, keepdims=True)
    y = x * lax.rsqrt(ms + EPS) * gain
    return y * (1.0 + scale) + shift


def _gelu(x):
    inner = x * (GELU_C1 + GELU_C3 * (x * x))
    hx = 0.5 * x
    return hx + hx * jnp.tanh(inner)


def _cmul(ar, ai, br, bi):
    return ar * br - ai * bi, ar * bi + ai * br


def _mod_kernel(c_ref, w_ref, b_ref, o_ref):
    c = c_ref[...]
    cond = c * jax.nn.sigmoid(c)
    o_ref[...] = _dot(cond.astype(BF16), w_ref[...].astype(BF16)) + b_ref[...]


def _mod_tile(n):
    return 1024 if n % 1024 == 0 else n


def _mod_operands(c, w_ada, b_ada):
    depth, d, n = w_ada.shape
    b = c.shape[0]
    rows = -(-b // SUBLANES) * SUBLANES
    cp = jnp.zeros((rows, d), F32).at[:b].set(c)
    return cp, w_ada, b_ada.reshape(depth, 1, n)


class _Resident:
    def __init__(self, w_all, layer, rows, tm, seq):
        _, self.k, self.n = w_all.shape
        assert rows % tm == 0 and seq % tm == 0
        ck = self.k
        while ck * self.n * 4 > W_CHUNK_BYTES and ck % 32 == 0:
            ck //= 2
        self.ck, self.nc = ck, self.k // ck
        self.layer, self.tm, self.seq = layer, tm, seq
        self.grid = (self.nc + rows // tm,)
        self.scratch = pltpu.VMEM((self.k, self.n), BF16)

    def tile(self, m):
        return jnp.maximum(m - self.nc, 0)

    def weight(self):
        return pl.BlockSpec((None, self.ck, self.n),
                            lambda m: (self.layer, jnp.minimum(m, self.nc - 1), 0))

    def rows(self, width):
        return pl.BlockSpec((self.tm, width), lambda m: (self.tile(m), 0))

    def whole(self, shape):
        return pl.BlockSpec(shape, lambda m: (0,) * len(shape))

    def mod_vec(self, sel):
        table, layer, which = sel
        return pl.BlockSpec(
            (None, 1, None, 1, table.shape[-1]),
            lambda m: (layer, (self.tile(m) * self.tm) // self.seq, which, 0, 0))

    def gain_vec(self, sel):
        table, i, j = sel
        return pl.BlockSpec((None, None, 1, table.shape[-1]), lambda m: (i, j, 0, 0))

    def kernel(self, body, w_pos):
        nc, ck = self.nc, self.ck

        def kern(*refs):
            w_s = refs[-1]
            refs = list(refs[:-1])
            m = pl.program_id(0)

            @pl.when(m < nc)
            def _():
                w_s[pl.ds(pl.multiple_of(m * ck, ck), ck), :] = refs[w_pos][...].astype(BF16)

            @pl.when(m >= nc)
            def _():
                body(*refs[:w_pos], w_s, *refs[w_pos + 1:])

        return kern


def _ln_mm_kernel(x_ref, g_ref, sc_ref, sh_ref, w_ref, o_ref):
    h = _rms_mod(x_ref[...], g_ref[...], sc_ref[0], sh_ref[0]).astype(BF16)
    acc = _dot(h, w_ref[...])
    for q in range(o_ref.shape[0]):
        o_ref[q] = acc[:, q * LANES:(q + 1) * LANES]


def _ln_matmul_tiled(x2d, gain, scale, shift, w_all, layer, *, seq, tm, name):
    rows, d = x2d.shape
    n_out = w_all.shape[-1]
    assert n_out % LANES == 0
    plan = _Resident(w_all, layer, rows, tm, seq)
    return pl.pallas_call(
        plan.kernel(_ln_mm_kernel, 4),
        grid=plan.grid,
        in_specs=[plan.rows(d), plan.gain_vec(gain), plan.mod_vec(scale), plan.mod_vec(shift),
                  plan.weight()],
        out_specs=pl.BlockSpec((n_out // LANES, tm, LANES), lambda m: (0, plan.tile(m), 0)),
        out_shape=jax.ShapeDtypeStruct((n_out // LANES, rows, LANES), F32),
        scratch_shapes=[plan.scratch],
        compiler_params=_params(("arbitrary",)),
        name=name,
    )(x2d, gain[0], scale[0], shift[0], w_all)


def _make_lru_in_kernel(e, tm, tiles_per_seq, first_step):
    halo = SUBLANES

    def kern(a_ref, w_ref, cw_ref, cb_ref, o_ref, xpad):
        @pl.when((pl.program_id(0) - first_step) % tiles_per_seq == 0)
        def _():
            xpad[0:halo, :] = jnp.zeros((halo, e), F32)

        acc = _dot(a_ref[...], w_ref[...])
        o_ref[:, 0:e] = _gelu(acc[:, 0:e])
        xpad[halo:halo + tm, :] = acc[:, e:2 * e]
        nk = cw_ref.shape[0]
        xc = cb_ref[...] + cw_ref[nk - 1:nk, :] * xpad[halo:halo + tm, :]
        for k in range(nk - 1):
            off = halo - (nk - 1) + k
            xc = xc + cw_ref[k:k + 1, :] * xpad[off:off + tm, :]
        xpad[0:halo, :] = xpad[tm:tm + halo, :]
        o_ref[:, e:2 * e] = xc

    return kern


def _lru_in(a, w_all, layer, conv_w, conv_b, *, seq, tm, name):
    rows, k = a.shape
    e = w_all.shape[-1] // 2
    nk = conv_w.shape[0]
    assert nk - 1 <= SUBLANES
    plan = _Resident(w_all, layer, rows, tm, seq)
    return pl.pallas_call(
        plan.kernel(_make_lru_in_kernel(e, tm, seq // tm, plan.nc), 1),
        grid=plan.grid,
        in_specs=[plan.rows(k), plan.weight(), plan.whole((nk, e)), plan.whole((1, e))],
        out_specs=plan.rows(2 * e),
        out_shape=jax.ShapeDtypeStruct((rows, 2 * e), F32),
        scratch_shapes=[pltpu.VMEM((tm + SUBLANES, e), F32), plan.scratch],
        compiler_params=_params(("arbitrary",)),
        name=name,
    )(a, w_all, conv_w.reshape(nk, e), conv_b.reshape(1, e))


def _swiglu_kernel(h_ref, wg_ref, wu_ref, o_ref, wg_s, wu_s):
    @pl.when(pl.program_id(1) == 0)
    def _():
        wg_s[...] = wg_ref[...].astype(BF16)
        wu_s[...] = wu_ref[...].astype(BF16)

    h = h_ref[...]
    g = _dot(h, wg_s[...])
    o_ref[...] = (g * jax.nn.sigmoid(g) * _dot(h, wu_s[...])).astype(BF16)


def _swiglu_up(h, w_all, layer, *, tm, tn, name):
    rows, d = h.shape
    n_out = w_all.shape[-1] // 2
    assert rows % tm == 0 and n_out % tn == 0
    nt = n_out // tn
    return pl.pallas_call(
        _swiglu_kernel,
        grid=(nt, rows // tm),
        in_specs=[
            pl.BlockSpec((tm, d), lambda n, m: (m, 0)),
            pl.BlockSpec((None, d, tn), lambda n, m: (layer, 0, n)),
            pl.BlockSpec((None, d, tn), lambda n, m: (layer, 0, n + nt)),
        ],
        out_specs=pl.BlockSpec((tm, tn), lambda n, m: (m, n)),
        out_shape=jax.ShapeDtypeStruct((rows, n_out), BF16),
        scratch_shapes=[pltpu.VMEM((d, tn), BF16), pltpu.VMEM((d, tn), BF16)],
        compiler_params=_params(("parallel", "arbitrary")),
        name=name,
    )(h, w_all, w_all)


def _make_mm_res_kernel(glu, tiled_a, n_out, mode):
    def kern(*refs):
        a_ref, w_ref, res_ref, gate_ref = refs[:4]
        if tiled_a:
            a = jnp.concatenate([a_ref[q] for q in range(a_ref.shape[0])], axis=1)
        else:
            a = a_ref[...]
        acc = _dot(a, w_ref[...])
        if glu:
            acc = acc[:, :n_out] * jax.nn.sigmoid(acc[:, n_out:])
        x = res_ref[...] + gate_ref[0] * acc
        if mode == "final":
            g_ref, o_ref = refs[4:]
            ms = jnp.mean(x * x, axis=-1, keepdims=True)
            o_ref[...] = x * lax.rsqrt(ms + EPS) * g_ref[...]
        elif mode == "x+h":
            g_ref, sc_ref, sh_ref, x_out, h_out = refs[4:]
            x_out[...] = x
            h_out[...] = _rms_mod(x, g_ref[...], sc_ref[0], sh_ref[0]).astype(BF16)
        else:
            (x_out,) = refs[4:]
            x_out[...] = x

    return kern


def _matmul_residual(a, w_all, layer, res2d, gate, *, seq, tm, glu=False, tiled_a=False,
                     next_norm=None, final_gain=None, name):
    rows, n_out = res2d.shape
    k = w_all.shape[1]
    plan = _Resident(w_all, layer, rows, tm, seq)
    if tiled_a:
        a_spec = pl.BlockSpec((k // LANES, tm, LANES), lambda m: (0, plan.tile(m), 0))
    else:
        a_spec = plan.rows(k)
    row_spec = plan.rows(n_out)
    in_specs = [a_spec, plan.weight(), row_spec, plan.mod_vec(gate)]
    args = [a, w_all, res2d, gate[0]]
    x_shape = jax.ShapeDtypeStruct((rows, n_out), F32)
    if final_gain is not None:
        mode = "final"
        in_specs.append(plan.whole((1, n_out)))
        args.append(final_gain.reshape(1, n_out))
        out_specs, out_shape = row_spec, x_shape
    elif next_norm is not None:
        mode = "x+h"
        gain, scale, shift = next_norm
        in_specs += [plan.gain_vec(gain), plan.mod_vec(scale), plan.mod_vec(shift)]
        args += [gain[0], scale[0], shift[0]]
        out_specs = [row_spec, row_spec]
        out_shape = [x_shape, jax.ShapeDtypeStruct((rows, n_out), BF16)]
    else:
        mode = "x"
        out_specs, out_shape = [row_spec], [x_shape]
    out = pl.pallas_call(
        plan.kernel(_make_mm_res_kernel(glu, tiled_a, n_out, mode), 1),
        grid=plan.grid,
        in_specs=in_specs,
        out_specs=out_specs,
        out_shape=out_shape,
        scratch_shapes=[plan.scratch],
        compiler_params=_params(("arbitrary",)),
        name=name,
    )(*args)
    if mode == "final":
        return out
    return (out[0], out[1]) if mode == "x+h" else (out[0], None)


def _make_s5_prep_kernel(tg, p):
    tc = S5_CHUNK
    half = tg * p

    def kern(lr_ref, li_ref, ldt_ref, br_ref, bi_ref, cr_ref, ci_ref,
             wio_ref, wis_ref, wso_ref, tab_ref):
        lr = lr_ref[...]
        li = li_ref[...]
        dt = jnp.exp(ldt_ref[...])
        mag = jnp.exp(lr * dt)
        ab_re = mag * jnp.cos(li * dt)
        ab_im = mag * jnp.sin(li * dt)
        nr, ni = ab_re - 1.0, ab_im
        den = lr * lr + li * li
        f_re = (nr * lr + ni * li) / den
        f_im = (ni * lr - nr * li) / den
        br, bi = br_ref[...], bi_ref[...]
        gc = br.shape[0]
        bb_re = f_re * br - f_im * bi
        bb_im = f_re * bi + f_im * br
        cr, ci = cr_ref[...], ci_ref[...]
        gid = lax.broadcasted_iota(jnp.int32, (1, half), 1) // p

        def expand(v):
            return jnp.concatenate([jnp.where(gid == g, v, 0.0) for g in range(tg)], axis=0)

        pows = [(jnp.ones_like(ab_re), jnp.zeros_like(ab_im))]
        for _ in range(tc):
            pows.append(_cmul(pows[-1][0], pows[-1][1], ab_re, ab_im))

        for t in range(tc):
            pr, pi = pows[tc - 1 - t]
            wr, wi = _cmul(pr, pi, bb_re, bb_im)
            rows = slice(t * gc, (t + 1) * gc)
            wis_ref[rows, 0:half] = wr.astype(BF16)
            wis_ref[rows, half:2 * half] = wi.astype(BF16)

        def split(v):
            hi = v.astype(BF16)
            return hi, (v - hi.astype(F32)).astype(BF16)

        def nt_dot(a, b):
            return lax.dot_general(a, b, (((1,), (1,)), ((), ())), preferred_element_type=F32)

        b_hi, b_lo = split(jnp.concatenate([expand(bb_re), expand(bb_im)], axis=1))
        q_all = []
        for tau in range(tc + 1):
            pr, pi = pows[tau]
            qr, qi = _cmul(cr, ci, pr, pi)
            er, ei = expand(qr), expand(-qi)
            if tau >= 1:
                cols = slice((tau - 1) * LANES, tau * LANES)
                wso_ref[0:half, cols] = er.T.astype(BF16)
                wso_ref[half:2 * half, cols] = ei.T.astype(BF16)
            if tau < tc:
                q_all.append(jnp.concatenate([er, ei], axis=1))
        q_hi, q_lo = split(jnp.concatenate(q_all, axis=0))
        kt = nt_dot(q_hi, b_hi) + (nt_dot(q_lo, b_hi) + nt_dot(q_hi, b_lo))
        for tau in range(tc):
            wio_ref[tau] = kt[tau * LANES:(tau + 1) * LANES, :].T.astype(BF16)

        lam = [(jnp.ones_like(ab_re), jnp.zeros_like(ab_im))]
        for _ in range(SUBLANES):
            lam.append(_cmul(lam[-1][0], lam[-1][1], pows[tc][0], pows[tc][1]))
        zrow = jnp.zeros_like(ab_re)
        groups = [[lam[r] for r in range(SUBLANES)]]
        for d in (1, 2, 4):
            groups.append([lam[d] if r >= d else (zrow, zrow) for r in range(SUBLANES)])
        groups.append([lam[SUBLANES]] * SUBLANES)
        for gi, grp in enumerate(groups):
            rows = slice(gi * SUBLANES, (gi + 1) * SUBLANES)
            tab_ref[rows, 0:half] = jnp.concatenate([v[0] for v in grp], axis=0)
            tab_ref[rows, half:2 * half] = jnp.concatenate([v[1] for v in grp], axis=0)

    return kern


def _s5_operators(lam_re, lam_im, log_dt, b_re, b_im, c_re, c_im, modulation=None):
    g, p = lam_re.shape
    gc = b_re.shape[-1]
    gp = g * p
    tg = S5_TILE_GROUPS
    assert gc * tg == LANES and g % tg == 0
    nj = g // tg
    half = tg * p
    kdim = S5_CHUNK * LANES
    lr = lam_re.reshape(1, gp)
    li = lam_im.reshape(1, gp)
    ldt = jnp.repeat(log_dt, p).reshape(1, gp)
    brt = b_re.transpose(2, 0, 1).reshape(gc, gp)
    bit = b_im.transpose(2, 0, 1).reshape(gc, gp)
    crt = c_re.transpose(1, 0, 2).reshape(gc, gp)
    cit = c_im.transpose(1, 0, 2).reshape(gc, gp)
    prep = _make_s5_prep_kernel(tg, p)
    tile = lambda s: jnp.minimum(s, nj - 1)
    row = pl.BlockSpec((1, half), lambda s: (0, tile(s)))
    mat = pl.BlockSpec((gc, half), lambda s: (0, tile(s)))
    in_specs = [row, row, row, mat, mat, mat, mat]
    args = [lr, li, ldt, brt, bit, crt, cit]
    out_specs = [
        pl.BlockSpec((None, S5_CHUNK, LANES, LANES), lambda s: (tile(s), 0, 0, 0)),
        pl.BlockSpec((None, S5_CHUNK * gc, 2 * half), lambda s: (tile(s), 0, 0)),
        pl.BlockSpec((None, 2 * half, kdim), lambda s: (tile(s), 0, 0)),
        pl.BlockSpec((None, 5 * SUBLANES, 2 * half), lambda s: (tile(s), 0, 0)),
    ]
    out_shape = [
        jax.ShapeDtypeStruct((nj, S5_CHUNK, LANES, LANES), BF16),
        jax.ShapeDtypeStruct((nj, S5_CHUNK * gc, 2 * half), BF16),
        jax.ShapeDtypeStruct((nj, 2 * half, kdim), BF16),
        jax.ShapeDtypeStruct((nj, 5 * SUBLANES, 2 * half), F32),
    ]
    if modulation is None:
        return pl.pallas_call(
            prep, grid=(nj,), in_specs=in_specs, out_specs=out_specs, out_shape=out_shape,
            compiler_params=_params(("parallel",)), name="s5_prep",
        )(*args), None

    cp, w_ada, b_ada3 = modulation
    depth, d, n = w_ada.shape
    rows = cp.shape[0]
    tn = _mod_tile(n)
    nt = n // tn
    nmod = depth * nt
    mtile = lambda s: jnp.minimum(s, nmod - 1)

    def kern(c_ref, w_ref, b_ref, *rest):
        s = pl.program_id(0)

        @pl.when(s < nmod)
        def _():
            _mod_kernel(c_ref, w_ref, b_ref, rest[7])

        @pl.when(s < nj)
        def _():
            prep(*rest[:7], *rest[8:])

    out = pl.pallas_call(
        kern,
        grid=(max(nj, nmod),),
        in_specs=[
            pl.BlockSpec((rows, d), lambda s: (0, 0)),
            pl.BlockSpec((None, d, tn), lambda s: (mtile(s) // nt, 0, mtile(s) % nt)),
            pl.BlockSpec((None, 1, tn), lambda s: (mtile(s) // nt, 0, mtile(s) % nt)),
        ] + in_specs,
        out_specs=[pl.BlockSpec((None, rows, tn), lambda s: (mtile(s) // nt, 0, mtile(s) % nt))]
        + out_specs,
        out_shape=[jax.ShapeDtypeStruct((depth, rows, n), F32)] + out_shape,
        compiler_params=_params(("arbitrary",)),
        name="mod_s5_prep",
    )(cp, w_ada, b_ada3, *args)
    return out[1:], out[0]


def _make_s5_kernel(nb, chunks_per_batch, half, gc):
    tc = S5_CHUNK
    nchunk = nb * chunks_per_batch
    tiles_per_batch = chunks_per_batch // SUBLANES
    ntile = nchunk // SUBLANES

    def kern(u_ref, wio_ref, wis_ref, wso_ref, tab_ref, d_ref, y_ref, s_scr, y_scr, wio_s, wis_s):
        tg = LANES // gc
        gid = (lax.broadcasted_iota(jnp.int32, (1, 2 * half), 1) % half) // (half // tg)
        zero_rows = jnp.zeros((gc, 2 * half), BF16)
        for t in range(tc):
            blk = wis_ref[t * gc:(t + 1) * gc, :]
            for g in range(tg):
                r0 = (t * tg + g) * gc
                wis_s[r0:r0 + gc, :] = jnp.where(gid == g, blk, zero_rows)
        zero_blk = jnp.zeros((LANES, LANES), BF16)
        for t in range(tc):
            for t2 in range(tc):
                wio_s[t * LANES:(t + 1) * LANES, t2 * LANES:(t2 + 1) * LANES] = (
                    wio_ref[t2 - t] if t2 >= t else zero_blk)

        lhs = jnp.concatenate(
            [u_ref[pl.ds(t, nchunk, stride=tc), :].astype(BF16) for t in range(tc)], axis=1)
        s_scr[...] = _dot(lhs, wis_s[...])

        tab = tab_ref[...]
        pw_r, pw_i = tab[0:8, :half], tab[0:8, half:]
        steps = [(d, tab[8 * i:8 * i + 8, :half], tab[8 * i:8 * i + 8, half:])
                 for i, d in ((1, 1), (2, 2), (3, 4))]
        l8_r, l8_i = tab[32:33, :half], tab[32:33, half:]
        first_row = lax.broadcasted_iota(jnp.int32, (SUBLANES, half), 0) == 0

        for m in range(ntile):
            rows = slice(m * SUBLANES, (m + 1) * SUBLANES)
            s = s_scr[rows, :]
            er, ei = s[:, :half], s[:, half:]
            for d, mr, mi in steps:
                rr = pltpu.roll(er, d, axis=0)
                ri = pltpu.roll(ei, d, axis=0)
                er, ei = er + (mr * rr - mi * ri), ei + (mr * ri + mi * rr)
            hr = jnp.where(first_row, 0.0, pltpu.roll(er, 1, axis=0))
            hi = jnp.where(first_row, 0.0, pltpu.roll(ei, 1, axis=0))
            if m % tiles_per_batch == 0:
                cr, ci = er[7:8], ei[7:8]
            else:
                hr = hr + (pw_r * cr - pw_i * ci)
                hi = hi + (pw_r * ci + pw_i * cr)
                cr, ci = er[7:8] + (l8_r * cr - l8_i * ci), ei[7:8] + (l8_r * ci + l8_i * cr)
            s_scr[rows, :] = jnp.concatenate([hr, hi], axis=1)

        y_io = jnp.concatenate(
            [_dot(lhs[:, :(p + 1) * MXU_N], wio_s[0:(p + 1) * MXU_N, p * MXU_N:(p + 1) * MXU_N])
             for p in range(tc * LANES // MXU_N)], axis=1)
        y = y_io + _dot(s_scr[...].astype(BF16), wso_ref[...])
        for t in range(tc):
            y_scr[pl.ds(t, nchunk, stride=tc), :] = y[:, t * LANES:(t + 1) * LANES]
        y_ref[...] = _gelu(y_scr[...] + d_ref[...] * u_ref[...]).astype(BF16)

    return kern


def _s5_core(u3, ops, d_skip, *, nb, seq):
    w_io, w_is, w_so, tab = ops
    nj, rows, _ = u3.shape
    kdim = S5_CHUNK * LANES
    sdim = w_is.shape[-1]
    chunks_per_batch = seq // S5_CHUNK
    gc = w_is.shape[1] // S5_CHUNK
    assert rows == nb * seq and chunks_per_batch % SUBLANES == 0
    tok = pl.BlockSpec((None, rows, LANES), lambda j: (j, 0, 0))
    return pl.pallas_call(
        _make_s5_kernel(nb, chunks_per_batch, sdim // 2, gc),
        grid=(nj,),
        in_specs=[
            tok,
            pl.BlockSpec((None, S5_CHUNK, LANES, LANES), lambda j: (j, 0, 0, 0)),
            pl.BlockSpec((None, S5_CHUNK * gc, sdim), lambda j: (j, 0, 0)),
            pl.BlockSpec((None, sdim, kdim), lambda j: (j, 0, 0)),
            pl.BlockSpec((None, 5 * SUBLANES, sdim), lambda j: (j, 0, 0)),
            pl.BlockSpec((1, LANES), lambda j: (0, j)),
        ],
        out_specs=tok,
        out_shape=jax.ShapeDtypeStruct((nj, rows, LANES), BF16),
        scratch_shapes=[pltpu.VMEM((rows // S5_CHUNK, sdim), F32), pltpu.VMEM((rows, LANES), F32),
                        pltpu.VMEM((kdim, kdim), BF16), pltpu.VMEM((kdim, sdim), BF16)],
        compiler_params=_params(("parallel",)),
        name="s5_core",
    )(u3, w_io, w_is, w_so, tab, d_skip.reshape(1, nj * LANES))


def _band_windows(width, blk, tn):
    starts, ends = [], []
    for n in range(width // tn):
        h_lo = (n * tn) // blk
        h_hi = (n * tn + tn - 1) // blk
        starts.append((h_lo * blk) // LANES)
        ends.append(-(-((h_hi + 1) * blk) // LANES))
    kw = max(e - s for s, e in zip(starts, ends))
    total = width // LANES
    starts = [min(s, total - kw) for s in starts]
    return starts, kw * LANES


def _banded(w, starts, kw, tn):
    blk = w.shape[-1]
    pieces = []
    for n, s in enumerate(starts):
        for h in range((n * tn) // blk, (n * tn + tn - 1) // blk + 1):
            j0 = max(0, n * tn - h * blk)
            j1 = min(blk, (n + 1) * tn - h * blk)
            r0 = h * blk - s * LANES
            pieces.append(jnp.pad(w[:, h, :, j0:j1], ((0, 0), (r0, kw - r0 - blk), (0, 0))))
    return jnp.concatenate(pieces, axis=2).astype(BF16)


def _make_lru_kernel(e, tm, tn, starts, kw, tiles_per_seq):
    ntile = e // tn
    nrow = tm // SUBLANES

    def kern(gx_ref, w_ref, vec_ref, y_ref, hcar, xb16):
        @pl.when(pl.program_id(0) % tiles_per_seq == 0)
        def _():
            hcar[...] = jnp.zeros((1, e), F32)

        xb16[...] = gx_ref[:, e:2 * e].astype(BF16)

        row = lax.broadcasted_iota(jnp.int32, (SUBLANES, tn), 0)
        masks = [row % HALF_TILE >= d for d in (1, 2)]
        in_upper = row < HALF_TILE
        neg = -vec_ref[2:3, :]
        softplus = jnp.maximum(neg, 0.0) + jnp.log1p(jnp.exp(-jnp.abs(neg)))
        rate = (-0.5 * LRU_C * LOG2_E) * softplus
        hb_rg = 0.5 * vec_ref[0:1, :]
        hb_ig = 0.5 * vec_ref[1:2, :]

        for n in range(ntile):
            cols = slice(n * tn, (n + 1) * tn)
            win = xb16[:, starts[n] * LANES:starts[n] * LANES + kw]
            tr = jnp.tanh(_dot(win, w_ref[0, :, cols]) + hb_rg[:, cols])
            ig = 0.5 * jnp.tanh(_dot(win, w_ref[1, :, cols]) + hb_ig[:, cols]) + 0.5
            a = jnp.exp2(rate[:, cols] * tr + rate[:, cols])
            om = 1.0 - a * a
            mult = om * lax.rsqrt(jnp.maximum(om, TINY))
            b = mult * (ig * gx_ref[:, e + n * tn:e + (n + 1) * tn])

            c = hcar[:, cols]
            hs = []
            for m in range(nrow):
                rows = slice(m * SUBLANES, (m + 1) * SUBLANES)
                av, bv = a[rows], b[rows]
                for d, keep in zip((1, 2), masks):
                    bv = bv + jnp.where(keep, av * pltpu.roll(bv, d, axis=0), 0.0)
                    av = jnp.where(keep, av * pltpu.roll(av, d, axis=0), av)
                upper = bv + av * c
                lower = bv + av * upper[HALF_TILE - 1:HALF_TILE]
                h = jnp.where(in_upper, upper, lower)
                c = h[SUBLANES - 1:SUBLANES]
                hs.append(h)
            hcar[:, cols] = c
            y_ref[:, cols] = (jnp.concatenate(hs, axis=0) * gx_ref[:, cols]).astype(BF16)

    return kern


def _lru_core(gx, w_rg, b_rg, w_ig, b_ig, lam, *, seq, tm):
    rows, e2 = gx.shape
    e = e2 // 2
    blk = w_rg.shape[1]
    tn = MXU_N
    assert e % tn == 0 and seq % tm == 0 and tm % SUBLANES == 0
    starts, kw = _band_windows(e, blk, tn)
    gates = _banded(0.5 * jnp.stack([w_rg, w_ig]), starts, kw, tn)
    vecs = jnp.stack([b_rg, b_ig, lam])
    return pl.pallas_call(
        _make_lru_kernel(e, tm, tn, starts, kw, seq // tm),
        grid=(rows // tm,),
        in_specs=[
            pl.BlockSpec((tm, e2), lambda m: (m, 0)),
            pl.BlockSpec((2, kw, e), lambda m: (0, 0, 0)),
            pl.BlockSpec((3, e), lambda m: (0, 0)),
        ],
        out_specs=pl.BlockSpec((tm, e), lambda m: (m, 0)),
        out_shape=jax.ShapeDtypeStruct((rows, e), BF16),
        scratch_shapes=[pltpu.VMEM((1, e), F32), pltpu.VMEM((tm, e), BF16)],
        compiler_params=_params(("arbitrary",)),
        name="lru_core",
    )(gx, gates, vecs)


def _tile(n, pref):
    t = min(n, pref)
    while n % t:
        t //= 2
    return t


def kernel(x, c, norm_g, w_ada, b_ada, s5_w_in, s5_lam_re, s5_lam_im, s5_log_dt, s5_b_re, s5_b_im, s5_c_re, s5_c_im, s5_d, s5_w_glu, lru_w_in, lru_conv_w, lru_conv_b, lru_w_rg, lru_b_rg, lru_w_ig, lru_b_ig, lru_lam, lru_w_out, ffn_w_gu, ffn_w_down, final_g):
    bsz, seq, d = x.shape
    depth = w_ada.shape[0]
    rows = bsz * seq
    x2 = x.reshape(rows, d)
    s5_ops, mods = _s5_operators(s5_lam_re[0], s5_lam_im[0], s5_log_dt[0], s5_b_re[0], s5_b_im[0],
                                 s5_c_re[0], s5_c_im[0],
                                 modulation=_mod_operands(c, w_ada, b_ada))
    mods = mods.reshape(depth, mods.shape[1], 6, 1, d)
    gains = norm_g.reshape(depth, 2, 1, d)
    tm_in = _tile(seq, 512)
    tm_res = _tile(seq, 256)
    tm_up = _tile(rows, 1024)
    hidden = ffn_w_down.shape[1]

    h = None
    for i in range(depth):
        sh1, sc1, g1, sh2, sc2, g2 = [(mods, i, q) for q in range(6)]
        ffn_norm = ((gains, i, 1), sc2, sh2)
        j = i // 2
        if i % 2 == 0:
            u3 = _ln_matmul_tiled(x2, (gains, i, 0), sc1, sh1, s5_w_in, j, seq=seq, tm=tm_in,
                                  name="s5_in")
            if j > 0:
                s5_ops, _ = _s5_operators(s5_lam_re[j], s5_lam_im[j], s5_log_dt[j], s5_b_re[j],
                                          s5_b_im[j], s5_c_re[j], s5_c_im[j])
            y3 = _s5_core(u3, s5_ops, s5_d[j], nb=bsz, seq=seq)
            x2, h = _matmul_residual(y3, s5_w_glu, j, x2, g1, seq=seq, tm=tm_res, glu=True,
                                     tiled_a=True, next_norm=ffn_norm, name="s5_out")
        else:
            gx = _lru_in(h, lru_w_in, j, lru_conv_w[j], lru_conv_b[j], seq=seq, tm=tm_res,
                         name="lru_in")
            y = _lru_core(gx, lru_w_rg[j], lru_b_rg[j], lru_w_ig[j], lru_b_ig[j], lru_lam[j],
                          seq=seq, tm=_tile(seq, 128))
            x2, h = _matmul_residual(y, lru_w_out, j, x2, g1, seq=seq, tm=tm_res,
                                     next_norm=ffn_norm, name="lru_out")
        act = _swiglu_up(h, ffn_w_gu, i, tm=tm_up, tn=_tile(hidden, 512), name="ffn_up")
        if i + 1 == depth:
            out = _matmul_residual(act, ffn_w_down, i, x2, g2, seq=seq, tm=tm_res, final_gain=final_g,
                                   name="ffn_down")
            return out.reshape(bsz, seq, d)
        nxt = None
        if (i + 1) % 2 == 1:
            nxt = ((gains, i + 1, 0), (mods, i + 1, 1), (mods, i + 1, 0))
        x2, h = _matmul_residual(act, ffn_w_down, i, x2, g2, seq=seq, tm=tm_res, next_norm=nxt,
                                 name="ffn_down")
```

```python
import jax
import jax.numpy as jnp
from jax import lax
from jax.experimental import pallas as pl
from jax.experimental.pallas import tpu as pltpu

F32 = jnp.float32
BF16 = jnp.bfloat16

EPS = 1e-6
LRU_C = 8.0
LOG2_E = 1.4426950408889634
TINY = 1e-30
GELU_C1 = 0.7978845608028654
GELU_C3 = 0.7978845608028654 * 0.044715
LANES = 128
SUBLANES = 8
HALF_TILE = SUBLANES // 2
MXU_N = 256
S5_CHUNK = 8
S5_TILE_GROUPS = 8
VMEM_LIMIT = 56 * 1024 * 1024
W_CHUNK_BYTES = 8 * 1024 * 1024


def _params(semantics, vmem=VMEM_LIMIT):
    return pltpu.CompilerParams(dimension_semantics=semantics, vmem_limit_bytes=vmem)


def _dot(a, b):
    return jnp.dot(a, b, preferred_element_type=F32)


def _rms_mod(x, gain, scale, shift):
    ms = jnp.mean(x * x, axis=-1, keepdims=True)
    y = x * lax.rsqrt(ms + EPS) * gain
    return y * (1.0 + scale) + shift


def _gelu(x):
    inner = x * (GELU_C1 + GELU_C3 * (x * x))
    hx = 0.5 * x
    return hx + hx * jnp.tanh(inner)


def _cmul(ar, ai, br, bi):
    return ar * br - ai * bi, ar * bi + ai * br


def _mod_kernel(c_ref, w_ref, b_ref, o_ref):
    c = c_ref[...]
    cond = c * jax.nn.sigmoid(c)
    o_ref[...] = _dot(cond.astype(BF16), w_ref[...].astype(BF16)) + b_ref[...]


def _mod_tile(n):
    return 1024 if n % 1024 == 0 else n


def _mod_operands(c, w_ada, b_ada):
    depth, d, n = w_ada.shape
    b = c.shape[0]
    rows = -(-b // SUBLANES) * SUBLANES
    cp = jnp.zeros((rows, d), F32).at[:b].set(c)
    return cp, w_ada, b_ada.reshape(depth, 1, n)


class _Resident:
    def __init__(self, w_all, layer, rows, tm, seq):
        _, self.k, self.n = w_all.shape
        assert rows % tm == 0 and seq % tm == 0
        ck = self.k
        while ck * self.n * 4 > W_CHUNK_BYTES and ck % 32 == 0:
            ck //= 2
        self.ck, self.nc = ck, self.k // ck
        self.layer, self.tm, self.seq = layer, tm, seq
        self.grid = (self.nc + rows // tm,)
        self.scratch = pltpu.VMEM((self.k, self.n), BF16)

    def tile(self, m):
        return jnp.maximum(m - self.nc, 0)

    def weight(self):
        return pl.BlockSpec((None, self.ck, self.n),
                            lambda m: (self.layer, jnp.minimum(m, self.nc - 1), 0))

    def rows(self, width):
        return pl.BlockSpec((self.tm, width), lambda m: (self.tile(m), 0))

    def whole(self, shape):
        return pl.BlockSpec(shape, lambda m: (0,) * len(shape))

    def mod_vec(self, sel):
        table, layer, which = sel
        return pl.BlockSpec(
            (None, 1, None, 1, table.shape[-1]),
            lambda m: (layer, (self.tile(m) * self.tm) // self.seq, which, 0, 0))

    def gain_vec(self, sel):
        table, i, j = sel
        return pl.BlockSpec((None, None, 1, table.shape[-1]), lambda m: (i, j, 0, 0))

    def kernel(self, body, w_pos):
        nc, ck = self.nc, self.ck

        def kern(*refs):
            w_s = refs[-1]
            refs = list(refs[:-1])
            m = pl.program_id(0)

            @pl.when(m < nc)
            def _():
                w_s[pl.ds(pl.multiple_of(m * ck, ck), ck), :] = refs[w_pos][...].astype(BF16)

            @pl.when(m >= nc)
            def _():
                body(*refs[:w_pos], w_s, *refs[w_pos + 1:])

        return kern


def _ln_mm_kernel(x_ref, g_ref, sc_ref, sh_ref, w_ref, o_ref):
    h = _rms_mod(x_ref[...], g_ref[...], sc_ref[0], sh_ref[0]).astype(BF16)
    acc = _dot(h, w_ref[...])
    for q in range(o_ref.shape[0]):
        o_ref[q] = acc[:, q * LANES:(q + 1) * LANES]


def _ln_matmul_tiled(x2d, gain, scale, shift, w_all, layer, *, seq, tm, name):
    rows, d = x2d.shape
    n_out = w_all.shape[-1]
    assert n_out % LANES == 0
    plan = _Resident(w_all, layer, rows, tm, seq)
    return pl.pallas_call(
        plan.kernel(_ln_mm_kernel, 4),
        grid=plan.grid,
        in_specs=[plan.rows(d), plan.gain_vec(gain), plan.mod_vec(scale), plan.mod_vec(shift),
                  plan.weight()],
        out_specs=pl.BlockSpec((n_out // LANES, tm, LANES), lambda m: (0, plan.tile(m), 0)),
        out_shape=jax.ShapeDtypeStruct((n_out // LANES, rows, LANES), F32),
        scratch_shapes=[plan.scratch],
        compiler_params=_params(("arbitrary",)),
        name=name,
    )(x2d, gain[0], scale[0], shift[0], w_all)


def _make_lru_in_kernel(e, tm, tiles_per_seq, first_step):
    halo = SUBLANES

    def kern(a_ref, w_ref, cw_ref, cb_ref, o_ref, xpad):
        @pl.when((pl.program_id(0) - first_step) % tiles_per_seq == 0)
        def _():
            xpad[0:halo, :] = jnp.zeros((halo, e), F32)

        acc = _dot(a_ref[...], w_ref[...])
        o_ref[:, 0:e] = _gelu(acc[:, 0:e])
        xpad[halo:halo + tm, :] = acc[:, e:2 * e]
        nk = cw_ref.shape[0]
        xc = cb_ref[...] + cw_ref[nk - 1:nk, :] * xpad[halo:halo + tm, :]
        for k in range(nk - 1):
            off = halo - (nk - 1) + k
            xc = xc + cw_ref[k:k + 1, :] * xpad[off:off + tm, :]
        xpad[0:halo, :] = xpad[tm:tm + halo, :]
        o_ref[:, e:2 * e] = xc

    return kern


def _lru_in(a, w_all, layer, conv_w, conv_b, *, seq, tm, name):
    rows, k = a.shape
    e = w_all.shape[-1] // 2
    nk = conv_w.shape[0]
    assert nk - 1 <= SUBLANES
    plan = _Resident(w_all, layer, rows, tm, seq)
    return pl.pallas_call(
        plan.kernel(_make_lru_in_kernel(e, tm, seq // tm, plan.nc), 1),
        grid=plan.grid,
        in_specs=[plan.rows(k), plan.weight(), plan.whole((nk, e)), plan.whole((1, e))],
        out_specs=plan.rows(2 * e),
        out_shape=jax.ShapeDtypeStruct((rows, 2 * e), F32),
        scratch_shapes=[pltpu.VMEM((tm + SUBLANES, e), F32), plan.scratch],
        compiler_params=_params(("arbitrary",)),
        name=name,
    )(a, w_all, conv_w.reshape(nk, e), conv_b.reshape(1, e))


def _swiglu_kernel(h_ref, wg_ref, wu_ref, o_ref, wg_s, wu_s):
    @pl.when(pl.program_id(1) == 0)
    def _():
        wg_s[...] = wg_ref[...].astype(BF16)
        wu_s[...] = wu_ref[...].astype(BF16)

    h = h_ref[...]
    g = _dot(h, wg_s[...])
    o_ref[...] = (g * jax.nn.sigmoid(g) * _dot(h, wu_s[...])).astype(BF16)


def _swiglu_up(h, w_all, layer, *, tm, tn, name):
    rows, d = h.shape
    n_out = w_all.shape[-1] // 2
    assert rows % tm == 0 and n_out % tn == 0
    nt = n_out // tn
    return pl.pallas_call(
        _swiglu_kernel,
        grid=(nt, rows // tm),
        in_specs=[
            pl.BlockSpec((tm, d), lambda n, m: (m, 0)),
            pl.BlockSpec((None, d, tn), lambda n, m: (layer, 0, n)),
            pl.BlockSpec((None, d, tn), lambda n, m: (layer, 0, n + nt)),
        ],
        out_specs=pl.BlockSpec((tm, tn), lambda n, m: (m, n)),
        out_shape=jax.ShapeDtypeStruct((rows, n_out), BF16),
        scratch_shapes=[pltpu.VMEM((d, tn), BF16), pltpu.VMEM((d, tn), BF16)],
        compiler_params=_params(("parallel", "arbitrary")),
        name=name,
    )(h, w_all, w_all)


def _make_mm_res_kernel(glu, tiled_a, n_out, mode):
    def kern(*refs):
        a_ref, w_ref, res_ref, gate_ref = refs[:4]
        if tiled_a:
            a = jnp.concatenate([a_ref[q] for q in range(a_ref.shape[0])], axis=1)
        else:
            a = a_ref[...]
        acc = _dot(a, w_ref[...])
        if glu:
            acc = acc[:, :n_out] * jax.nn.sigmoid(acc[:, n_out:])
        x = res_ref[...] + gate_ref[0] * acc
        if mode == "final":
            g_ref, o_ref = refs[4:]
            ms = jnp.mean(x * x, axis=-1, keepdims=True)
            o_ref[...] = x * lax.rsqrt(ms + EPS) * g_ref[...]
        elif mode == "x+h":
            g_ref, sc_ref, sh_ref, x_out, h_out = refs[4:]
            x_out[...] = x
            h_out[...] = _rms_mod(x, g_ref[...], sc_ref[0], sh_ref[0]).astype(BF16)
        else:
            (x_out,) = refs[4:]
            x_out[...] = x

    return kern


def _matmul_residual(a, w_all, layer, res2d, gate, *, seq, tm, glu=False, tiled_a=False,
                     next_norm=None, final_gain=None, name):
    rows, n_out = res2d.shape
    k = w_all.shape[1]
    plan = _Resident(w_all, layer, rows, tm, seq)
    if tiled_a:
        a_spec = pl.BlockSpec((k // LANES, tm, LANES), lambda m: (0, plan.tile(m), 0))
    else:
        a_spec = plan.rows(k)
    row_spec = plan.rows(n_out)
    in_specs = [a_spec, plan.weight(), row_spec, plan.mod_vec(gate)]
    args = [a, w_all, res2d, gate[0]]
    x_shape = jax.ShapeDtypeStruct((rows, n_out), F32)
    if final_gain is not None:
        mode = "final"
        in_specs.append(plan.whole((1, n_out)))
        args.append(final_gain.reshape(1, n_out))
        out_specs, out_shape = row_spec, x_shape
    elif next_norm is not None:
        mode = "x+h"
        gain, scale, shift = next_norm
        in_specs += [plan.gain_vec(gain), plan.mod_vec(scale), plan.mod_vec(shift)]
        args += [gain[0], scale[0], shift[0]]
        out_specs = [row_spec, row_spec]
        out_shape = [x_shape, jax.ShapeDtypeStruct((rows, n_out), BF16)]
    else:
        mode = "x"
        out_specs, out_shape = [row_spec], [x_shape]
    out = pl.pallas_call(
        plan.kernel(_make_mm_res_kernel(glu, tiled_a, n_out, mode), 1),
        grid=plan.grid,
        in_specs=in_specs,
        out_specs=out_specs,
        out_shape=out_shape,
        scratch_shapes=[plan.scratch],
        compiler_params=_params(("arbitrary",)),
        name=name,
    )(*args)
    if mode == "final":
        return out
    return (out[0], out[1]) if mode == "x+h" else (out[0], None)


def _make_s5_prep_kernel(tg, p):
    tc = S5_CHUNK
    half = tg * p

    def kern(lr_ref, li_ref, ldt_ref, br_ref, bi_ref, cr_ref, ci_ref,
             wio_ref, wis_ref, wso_ref, tab_ref):
        lr = lr_ref[...]
        li = li_ref[...]
        dt = jnp.exp(ldt_ref[...])
        mag = jnp.exp(lr * dt)
        ab_re = mag * jnp.cos(li * dt)
        ab_im = mag * jnp.sin(li * dt)
        nr, ni = ab_re - 1.0, ab_im
        den = lr * lr + li * li
        f_re = (nr * lr + ni * li) / den
        f_im = (ni * lr - nr * li) / den
        br, bi = br_ref[...], bi_ref[...]
        gc = br.shape[0]
        bb_re = f_re * br - f_im * bi
        bb_im = f_re * bi + f_im * br
        cr, ci = cr_ref[...], ci_ref[...]
        gid = lax.broadcasted_iota(jnp.int32, (1, half), 1) // p

        def expand(v):
            return jnp.concatenate([jnp.where(gid == g, v, 0.0) for g in range(tg)], axis=0)

        pows = [(jnp.ones_like(ab_re), jnp.zeros_like(ab_im))]
        for _ in range(tc):
            pows.append(_cmul(pows[-1][0], pows[-1][1], ab_re, ab_im))

        for t in range(tc):
            pr, pi = pows[tc - 1 - t]
            wr, wi = _cmul(pr, pi, bb_re, bb_im)
            rows = slice(t * gc, (t + 1) * gc)
            wis_ref[rows, 0:half] = wr.astype(BF16)
            wis_ref[rows, half:2 * half] = wi.astype(BF16)

        def split(v):
            hi = v.astype(BF16)
            return hi, (v - hi.astype(F32)).astype(BF16)

        def nt_dot(a, b):
            return lax.dot_general(a, b, (((1,), (1,)), ((), ())), preferred_element_type=F32)

        b_hi, b_lo = split(jnp.concatenate([expand(bb_re), expand(bb_im)], axis=1))
        q_all = []
        for tau in range(tc + 1):
            pr, pi = pows[tau]
            qr, qi = _cmul(cr, ci, pr, pi)
            er, ei = expand(qr), expand(-qi)
            if tau >= 1:
                cols = slice((tau - 1) * LANES, tau * LANES)
                wso_ref[0:half, cols] = er.T.astype(BF16)
                wso_ref[half:2 * half, cols] = ei.T.astype(BF16)
            if tau < tc:
                q_all.append(jnp.concatenate([er, ei], axis=1))
        q_hi, q_lo = split(jnp.concatenate(q_all, axis=0))
        kt = nt_dot(q_hi, b_hi) + (nt_dot(q_lo, b_hi) + nt_dot(q_hi, b_lo))
        for tau in range(tc):
            wio_ref[tau] = kt[tau * LANES:(tau + 1) * LANES, :].T.astype(BF16)

        lam = [(jnp.ones_like(ab_re), jnp.zeros_like(ab_im))]
        for _ in range(SUBLANES):
            lam.append(_cmul(lam[-1][0], lam[-1][1], pows[tc][0], pows[tc][1]))
        zrow = jnp.zeros_like(ab_re)
        groups = [[lam[r] for r in range(SUBLANES)]]
        for d in (1, 2, 4):
            groups.append([lam[d] if r >= d else (zrow, zrow) for r in range(SUBLANES)])
        groups.append([lam[SUBLANES]] * SUBLANES)
        for gi, grp in enumerate(groups):
            rows = slice(gi * SUBLANES, (gi + 1) * SUBLANES)
            tab_ref[rows, 0:half] = jnp.concatenate([v[0] for v in grp], axis=0)
            tab_ref[rows, half:2 * half] = jnp.concatenate([v[1] for v in grp], axis=0)

    return kern


def _s5_operators(lam_re, lam_im, log_dt, b_re, b_im, c_re, c_im, modulation=None):
    g, p = lam_re.shape
    gc = b_re.shape[-1]
    gp = g * p
    tg = S5_TILE_GROUPS
    assert gc * tg == LANES and g % tg == 0
    nj = g // tg
    half = tg * p
    kdim = S5_CHUNK * LANES
    lr = lam_re.reshape(1, gp)
    li = lam_im.reshape(1, gp)
    ldt = jnp.repeat(log_dt, p).reshape(1, gp)
    brt = b_re.transpose(2, 0, 1).reshape(gc, gp)
    bit = b_im.transpose(2, 0, 1).reshape(gc, gp)
    crt = c_re.transpose(1, 0, 2).reshape(gc, gp)
    cit = c_im.transpose(1, 0, 2).reshape(gc, gp)
    prep = _make_s5_prep_kernel(tg, p)
    tile = lambda s: jnp.minimum(s, nj - 1)
    row = pl.BlockSpec((1, half), lambda s: (0, tile(s)))
    mat = pl.BlockSpec((gc, half), lambda s: (0, tile(s)))
    in_specs = [row, row, row, mat, mat, mat, mat]
    args = [lr, li, ldt, brt, bit, crt, cit]
    out_specs = [
        pl.BlockSpec((None, S5_CHUNK, LANES, LANES), lambda s: (tile(s), 0, 0, 0)),
        pl.BlockSpec((None, S5_CHUNK * gc, 2 * half), lambda s: (tile(s), 0, 0)),
        pl.BlockSpec((None, 2 * half, kdim), lambda s: (tile(s), 0, 0)),
        pl.BlockSpec((None, 5 * SUBLANES, 2 * half), lambda s: (tile(s), 0, 0)),
    ]
    out_shape = [
        jax.ShapeDtypeStruct((nj, S5_CHUNK, LANES, LANES), BF16),
        jax.ShapeDtypeStruct((nj, S5_CHUNK * gc, 2 * half), BF16),
        jax.ShapeDtypeStruct((nj, 2 * half, kdim), BF16),
        jax.ShapeDtypeStruct((nj, 5 * SUBLANES, 2 * half), F32),
    ]
    if modulation is None:
        return pl.pallas_call(
            prep, grid=(nj,), in_specs=in_specs, out_specs=out_specs, out_shape=out_shape,
            compiler_params=_params(("parallel",)), name="s5_prep",
        )(*args), None

    cp, w_ada, b_ada3 = modulation
    depth, d, n = w_ada.shape
    rows = cp.shape[0]
    tn = _mod_tile(n)
    nt = n // tn
    nmod = depth * nt
    mtile = lambda s: jnp.minimum(s, nmod - 1)

    def kern(c_ref, w_ref, b_ref, *rest):
        s = pl.program_id(0)

        @pl.when(s < nmod)
        def _():
            _mod_kernel(c_ref, w_ref, b_ref, rest[7])

        @pl.when(s < nj)
        def _():
            prep(*rest[:7], *rest[8:])

    out = pl.pallas_call(
        kern,
        grid=(max(nj, nmod),),
        in_specs=[
            pl.BlockSpec((rows, d), lambda s: (0, 0)),
            pl.BlockSpec((None, d, tn), lambda s: (mtile(s) // nt, 0, mtile(s) % nt)),
            pl.BlockSpec((None, 1, tn), lambda s: (mtile(s) // nt, 0, mtile(s) % nt)),
        ] + in_specs,
        out_specs=[pl.BlockSpec((None, rows, tn), lambda s: (mtile(s) // nt, 0, mtile(s) % nt))]
        + out_specs,
        out_shape=[jax.ShapeDtypeStruct((depth, rows, n), F32)] + out_shape,
        compiler_params=_params(("arbitrary",)),
        name="mod_s5_prep",
    )(cp, w_ada, b_ada3, *args)
    return out[1:], out[0]


def _make_s5_kernel(nb, chunks_per_batch, half, gc):
    tc = S5_CHUNK
    nchunk = nb * chunks_per_batch
    tiles_per_batch = chunks_per_batch // SUBLANES
    ntile = nchunk // SUBLANES

    def kern(u_ref, wio_ref, wis_ref, wso_ref, tab_ref, d_ref, y_ref, s_scr, y_scr, wio_s, wis_s):
        tg = LANES // gc
        gid = (lax.broadcasted_iota(jnp.int32, (1, 2 * half), 1) % half) // (half // tg)
        zero_rows = jnp.zeros((gc, 2 * half), BF16)
        for t in range(tc):
            blk = wis_ref[t * gc:(t + 1) * gc, :]
            for g in range(tg):
                r0 = (t * tg + g) * gc
                wis_s[r0:r0 + gc, :] = jnp.where(gid == g, blk, zero_rows)
        zero_blk = jnp.zeros((LANES, LANES), BF16)
        for t in range(tc):
            for t2 in range(tc):
                wio_s[t * LANES:(t + 1) * LANES, t2 * LANES:(t2 + 1) * LANES] = (
                    wio_ref[t2 - t] if t2 >= t else zero_blk)

        lhs = jnp.concatenate(
            [u_ref[pl.ds(t, nchunk, stride=tc), :].astype(BF16) for t in range(tc)], axis=1)
        s_scr[...] = _dot(lhs, wis_s[...])

        tab = tab_ref[...]
        pw_r, pw_i = tab[0:8, :half], tab[0:8, half:]
        steps = [(d, tab[8 * i:8 * i + 8, :half], tab[8 * i:8 * i + 8, half:])
                 for i, d in ((1, 1), (2, 2), (3, 4))]
        l8_r, l8_i = tab[32:33, :half], tab[32:33, half:]
        first_row = lax.broadcasted_iota(jnp.int32, (SUBLANES, half), 0) == 0

        for m in range(ntile):
            rows = slice(m * SUBLANES, (m + 1) * SUBLANES)
            s = s_scr[rows, :]
            er, ei = s[:, :half], s[:, half:]
            for d, mr, mi in steps:
                rr = pltpu.roll(er, d, axis=0)
                ri = pltpu.roll(ei, d, axis=0)
                er, ei = er + (mr * rr - mi * ri), ei + (mr * ri + mi * rr)
            hr = jnp.where(first_row, 0.0, pltpu.roll(er, 1, axis=0))
            hi = jnp.where(first_row, 0.0, pltpu.roll(ei, 1, axis=0))
            if m % tiles_per_batch == 0:
                cr, ci = er[7:8], ei[7:8]
            else:
                hr = hr + (pw_r * cr - pw_i * ci)
                hi = hi + (pw_r * ci + pw_i * cr)
                cr, ci = er[7:8] + (l8_r * cr - l8_i * ci), ei[7:8] + (l8_r * ci + l8_i * cr)
            s_scr[rows, :] = jnp.concatenate([hr, hi], axis=1)

        y_io = jnp.concatenate(
            [_dot(lhs[:, :(p + 1) * MXU_N], wio_s[0:(p + 1) * MXU_N, p * MXU_N:(p + 1) * MXU_N])
             for p in range(tc * LANES // MXU_N)], axis=1)
        y = y_io + _dot(s_scr[...].astype(BF16), wso_ref[...])
        for t in range(tc):
            y_scr[pl.ds(t, nchunk, stride=tc), :] = y[:, t * LANES:(t + 1) * LANES]
        y_ref[...] = _gelu(y_scr[...] + d_ref[...] * u_ref[...]).astype(BF16)

    return kern


def _s5_core(u3, ops, d_skip, *, nb, seq):
    w_io, w_is, w_so, tab = ops
    nj, rows, _ = u3.shape
    kdim = S5_CHUNK * LANES
    sdim = w_is.shape[-1]
    chunks_per_batch = seq // S5_CHUNK
    gc = w_is.shape[1] // S5_CHUNK
    assert rows == nb * seq and chunks_per_batch % SUBLANES == 0
    tok = pl.BlockSpec((None, rows, LANES), lambda j: (j, 0, 0))
    return pl.pallas_call(
        _make_s5_kernel(nb, chunks_per_batch, sdim // 2, gc),
        grid=(nj,),
        in_specs=[
            tok,
            pl.BlockSpec((None, S5_CHUNK, LANES, LANES), lambda j: (j, 0, 0, 0)),
            pl.BlockSpec((None, S5_CHUNK * gc, sdim), lambda j: (j, 0, 0)),
            pl.BlockSpec((None, sdim, kdim), lambda j: (j, 0, 0)),
            pl.BlockSpec((None, 5 * SUBLANES, sdim), lambda j: (j, 0, 0)),
            pl.BlockSpec((1, LANES), lambda j: (0, j)),
        ],
        out_specs=tok,
        out_shape=jax.ShapeDtypeStruct((nj, rows, LANES), BF16),
        scratch_shapes=[pltpu.VMEM((rows // S5_CHUNK, sdim), F32), pltpu.VMEM((rows, LANES), F32),
                        pltpu.VMEM((kdim, kdim), BF16), pltpu.VMEM((kdim, sdim), BF16)],
        compiler_params=_params(("parallel",)),
        name="s5_core",
    )(u3, w_io, w_is, w_so, tab, d_skip.reshape(1, nj * LANES))


def _band_windows(width, blk, tn):
    starts, ends = [], []
    for n in range(width // tn):
        h_lo = (n * tn) // blk
        h_hi = (n * tn + tn - 1) // blk
        starts.append((h_lo * blk) // LANES)
        ends.append(-(-((h_hi + 1) * blk) // LANES))
    kw = max(e - s for s, e in zip(starts, ends))
    total = width // LANES
    starts = [min(s, total - kw) for s in starts]
    return starts, kw * LANES


def _banded(w, starts, kw, tn):
    blk = w.shape[-1]
    tiles = []
    for n, s in enumerate(starts):
        pieces = []
        for h in range((n * tn) // blk, (n * tn + tn - 1) // blk + 1):
            j0 = max(0, n * tn - h * blk)
            j1 = min(blk, (n + 1) * tn - h * blk)
            r0 = h * blk - s * LANES
            pieces.append(jnp.pad(w[:, h, :, j0:j1], ((0, 0), (r0, kw - r0 - blk), (0, 0))))
        tiles.append(jnp.concatenate(pieces, axis=2))
    return jnp.stack(tiles, axis=1).astype(BF16)


def _make_lru_kernel(e, tm, tn, starts, kw, tiles_per_seq):
    ntile = e // tn
    nrow = tm // SUBLANES

    def kern(gx_ref, w_ref, vec_ref, y_ref, hcar, xb16):
        @pl.when(pl.program_id(0) % tiles_per_seq == 0)
        def _():
            hcar[...] = jnp.zeros((1, e), F32)

        xb16[...] = gx_ref[:, e:2 * e].astype(BF16)

        row = lax.broadcasted_iota(jnp.int32, (SUBLANES, tn), 0)
        masks = [row % HALF_TILE >= d for d in (1, 2)]
        in_upper = row < HALF_TILE
        neg = -vec_ref[2:3, :]
        softplus = jnp.maximum(neg, 0.0) + jnp.log1p(jnp.exp(-jnp.abs(neg)))
        rate = (-LRU_C * LOG2_E) * softplus
        hb_rg = 0.5 * vec_ref[0:1, :]
        hb_ig = 0.5 * vec_ref[1:2, :]

        for n in range(ntile):
            cols = slice(n * tn, (n + 1) * tn)
            win = xb16[:, starts[n] * LANES:starts[n] * LANES + kw]
            r = 1.0 / (1.0 + jnp.exp2((-2.0 * LOG2_E) * (_dot(win, w_ref[0, n]) + hb_rg[:, cols])))
            ig = 1.0 / (1.0 + jnp.exp2((-2.0 * LOG2_E) * (_dot(win, w_ref[1, n]) + hb_ig[:, cols])))
            a = jnp.exp2(rate[:, cols] * r)
            om = 1.0 - a * a
            mult = om * lax.rsqrt(jnp.maximum(om, TINY))
            b = mult * (ig * gx_ref[:, e + n * tn:e + (n + 1) * tn])

            c = hcar[:, cols]
            hs = []
            for m in range(nrow):
                rows = slice(m * SUBLANES, (m + 1) * SUBLANES)
                av, bv = a[rows], b[rows]
                for d, keep in zip((1, 2), masks):
                    bv = bv + jnp.where(keep, av * pltpu.roll(bv, d, axis=0), 0.0)
                    av = jnp.where(keep, av * pltpu.roll(av, d, axis=0), av)
                upper = bv + av * c
                lower = bv + av * upper[HALF_TILE - 1:HALF_TILE]
                h = jnp.where(in_upper, upper, lower)
                c = h[SUBLANES - 1:SUBLANES]
                hs.append(h)
            hcar[:, cols] = c
            y_ref[:, cols] = (jnp.concatenate(hs, axis=0) * gx_ref[:, cols]).astype(BF16)

    return kern


def _lru_core(gx, w_rg, b_rg, w_ig, b_ig, lam, *, seq, tm):
    rows, e2 = gx.shape
    e = e2 // 2
    blk = w_rg.shape[1]
    tn = MXU_N
    assert e % tn == 0 and seq % tm == 0 and tm % SUBLANES == 0
    starts, kw = _band_windows(e, blk, tn)
    gates = _banded(0.5 * jnp.stack([w_rg, w_ig]), starts, kw, tn)
    vecs = jnp.stack([b_rg, b_ig, lam])
    ntile = e // tn
    return pl.pallas_call(
        _make_lru_kernel(e, tm, tn, starts, kw, seq // tm),
        grid=(rows // tm,),
        in_specs=[
            pl.BlockSpec((tm, e2), lambda m: (m, 0)),
            pl.BlockSpec((2, ntile, kw, tn), lambda m: (0, 0, 0, 0)),
            pl.BlockSpec((3, e), lambda m: (0, 0)),
        ],
        out_specs=pl.BlockSpec((tm, e), lambda m: (m, 0)),
        out_shape=jax.ShapeDtypeStruct((rows, e), BF16),
        scratch_shapes=[pltpu.VMEM((1, e), F32), pltpu.VMEM((tm, e), BF16)],
        compiler_params=_params(("arbitrary",)),
        name="lru_core",
    )(gx, gates, vecs)


def _tile(n, pref):
    t = min(n, pref)
    while n % t:
        t //= 2
    return t


def kernel(x, c, norm_g, w_ada, b_ada, s5_w_in, s5_lam_re, s5_lam_im, s5_log_dt, s5_b_re, s5_b_im, s5_c_re, s5_c_im, s5_d, s5_w_glu, lru_w_in, lru_conv_w, lru_conv_b, lru_w_rg, lru_b_rg, lru_w_ig, lru_b_ig, lru_lam, lru_w_out, ffn_w_gu, ffn_w_down, final_g):
    bsz, seq, d = x.shape
    depth = w_ada.shape[0]
    rows = bsz * seq
    x2 = x.reshape(rows, d)
    s5_ops, mods = _s5_operators(s5_lam_re[0], s5_lam_im[0], s5_log_dt[0], s5_b_re[0], s5_b_im[0],
                                 s5_c_re[0], s5_c_im[0],
                                 modulation=_mod_operands(c, w_ada, b_ada))
    mods = mods.reshape(depth, mods.shape[1], 6, 1, d)
    gains = norm_g.reshape(depth, 2, 1, d)
    tm_in = _tile(seq, 512)
    tm_res = _tile(seq, 256)
    tm_up = _tile(rows, 1024)
    hidden = ffn_w_down.shape[1]

    h = None
    for i in range(depth):
        sh1, sc1, g1, sh2, sc2, g2 = [(mods, i, q) for q in range(6)]
        ffn_norm = ((gains, i, 1), sc2, sh2)
        j = i // 2
        if i % 2 == 0:
            u3 = _ln_matmul_tiled(x2, (gains, i, 0), sc1, sh1, s5_w_in, j, seq=seq, tm=tm_in,
                                  name="s5_in")
            if j > 0:
                s5_ops, _ = _s5_operators(s5_lam_re[j], s5_lam_im[j], s5_log_dt[j], s5_b_re[j],
                                          s5_b_im[j], s5_c_re[j], s5_c_im[j])
            y3 = _s5_core(u3, s5_ops, s5_d[j], nb=bsz, seq=seq)
            x2, h = _matmul_residual(y3, s5_w_glu, j, x2, g1, seq=seq, tm=tm_res, glu=True,
                                     tiled_a=True, next_norm=ffn_norm, name="s5_out")
        else:
            gx = _lru_in(h, lru_w_in, j, lru_conv_w[j], lru_conv_b[j], seq=seq, tm=tm_res,
                         name="lru_in")
            y = _lru_core(gx, lru_w_rg[j], lru_b_rg[j], lru_w_ig[j], lru_b_ig[j], lru_lam[j],
                          seq=seq, tm=_tile(seq, 128))
            x2, h = _matmul_residual(y, lru_w_out, j, x2, g1, seq=seq, tm=tm_res,
                                     next_norm=ffn_norm, name="lru_out")
        act = _swiglu_up(h, ffn_w_gu, i, tm=tm_up, tn=_tile(hidden, 512), name="ffn_up")
        if i + 1 == depth:
            out = _matmul_residual(act, ffn_w_down, i, x2, g2, seq=seq, tm=tm_res, final_gain=final_g,
                                   name="ffn_down")
            return out.reshape(bsz, seq, d)
        nxt = None
        if (i + 1) % 2 == 1:
            nxt = ((gains, i + 1, 0), (mods, i + 1, 1), (mods, i + 1, 0))
        x2, h = _matmul_residual(act, ffn_w_down, i, x2, g2, seq=seq, tm=tm_res, next_norm=nxt,
                                 name="ffn_down")
```

```python
import jax
import jax.numpy as jnp
from jax import lax
from jax.experimental import pallas as pl
from jax.experimental.pallas import tpu as pltpu

F32 = jnp.float32
BF16 = jnp.bfloat16

EPS = 1e-6
LRU_C = 8.0
LOG2_E = 1.4426950408889634
TINY = 1e-30
GELU_C1 = 0.7978845608028654
GELU_C3 = 0.7978845608028654 * 0.044715
LANES = 128
SUBLANES = 8
HALF_TILE = SUBLANES // 2
MXU_N = 256
S5_CHUNK = 8
S5_TILE_GROUPS = 8
VMEM_LIMIT = 56 * 1024 * 1024
W_CHUNK_BYTES = 8 * 1024 * 1024


def _params(semantics, vmem=VMEM_LIMIT):
    return pltpu.CompilerParams(dimension_semantics=semantics, vmem_limit_bytes=vmem)


def _dot(a, b):
    return jnp.dot(a, b, preferred_element_type=F32)


def _rms_mod(x, gain, scale, shift):
    ms = jnp.mean(x * x, axis=-1, keepdims=True)
    y = x * lax.rsqrt(ms + EPS) * gain
    return y * (1.0 + scale) + shift


def _gelu(x):
    inner = x * (GELU_C1 + GELU_C3 * (x * x))
    hx = 0.5 * x
    return hx + hx * jnp.tanh(inner)


def _cmul(ar, ai, br, bi):
    return ar * br - ai * bi, ar * bi + ai * br


def _mod_kernel(c_ref, w_ref, b_ref, o_ref):
    c = c_ref[...]
    cond = c * jax.nn.sigmoid(c)
    o_ref[...] = _dot(cond.astype(BF16), w_ref[...].astype(BF16)) + b_ref[...]


def _mod_tile(n):
    return 1024 if n % 1024 == 0 else n


def _mod_operands(c, w_ada, b_ada):
    depth, d, n = w_ada.shape
    b = c.shape[0]
    rows = -(-b // SUBLANES) * SUBLANES
    cp = jnp.zeros((rows, d), F32).at[:b].set(c)
    return cp, w_ada, b_ada.reshape(depth, 1, n)


class _Resident:
    def __init__(self, w_all, layer, rows, tm, seq):
        _, self.k, self.n = w_all.shape
        assert rows % tm == 0 and seq % tm == 0
        ck = self.k
        while ck * self.n * 4 > W_CHUNK_BYTES and ck % 32 == 0:
            ck //= 2
        self.ck, self.nc = ck, self.k // ck
        self.layer, self.tm, self.seq = layer, tm, seq
        self.grid = (self.nc + rows // tm,)
        self.scratch = pltpu.VMEM((self.k, self.n), BF16)

    def tile(self, m):
        return jnp.maximum(m - self.nc, 0)

    def weight(self):
        return pl.BlockSpec((None, self.ck, self.n),
                            lambda m: (self.layer, jnp.minimum(m, self.nc - 1), 0))

    def rows(self, width):
        return pl.BlockSpec((self.tm, width), lambda m: (self.tile(m), 0))

    def whole(self, shape):
        return pl.BlockSpec(shape, lambda m: (0,) * len(shape))

    def mod_vec(self, sel):
        table, layer, which = sel
        return pl.BlockSpec(
            (None, 1, None, 1, table.shape[-1]),
            lambda m: (layer, (self.tile(m) * self.tm) // self.seq, which, 0, 0))

    def gain_vec(self, sel):
        table, i, j = sel
        return pl.BlockSpec((None, None, 1, table.shape[-1]), lambda m: (i, j, 0, 0))

    def kernel(self, body, w_pos):
        nc, ck = self.nc, self.ck

        def kern(*refs):
            w_s = refs[-1]
            refs = list(refs[:-1])
            m = pl.program_id(0)

            @pl.when(m < nc)
            def _():
                w_s[pl.ds(pl.multiple_of(m * ck, ck), ck), :] = refs[w_pos][...].astype(BF16)

            @pl.when(m >= nc)
            def _():
                body(*refs[:w_pos], w_s, *refs[w_pos + 1:])

        return kern


def _ln_mm_kernel(x_ref, g_ref, sc_ref, sh_ref, w_ref, o_ref):
    h = _rms_mod(x_ref[...], g_ref[...], sc_ref[0], sh_ref[0]).astype(BF16)
    acc = _dot(h, w_ref[...])
    for q in range(o_ref.shape[0]):
        o_ref[q] = acc[:, q * LANES:(q + 1) * LANES]


def _ln_matmul_tiled(x2d, gain, scale, shift, w_all, layer, *, seq, tm, name):
    rows, d = x2d.shape
    n_out = w_all.shape[-1]
    assert n_out % LANES == 0
    plan = _Resident(w_all, layer, rows, tm, seq)
    return pl.pallas_call(
        plan.kernel(_ln_mm_kernel, 4),
        grid=plan.grid,
        in_specs=[plan.rows(d), plan.gain_vec(gain), plan.mod_vec(scale), plan.mod_vec(shift),
                  plan.weight()],
        out_specs=pl.BlockSpec((n_out // LANES, tm, LANES), lambda m: (0, plan.tile(m), 0)),
        out_shape=jax.ShapeDtypeStruct((n_out // LANES, rows, LANES), F32),
        scratch_shapes=[plan.scratch],
        compiler_params=_params(("arbitrary",)),
        name=name,
    )(x2d, gain[0], scale[0], shift[0], w_all)


def _make_lru_in_kernel(e, tm, tiles_per_seq, first_step):
    halo = SUBLANES

    def kern(a_ref, w_ref, cw_ref, cb_ref, g_out, x_out, xpad):
        @pl.when((pl.program_id(0) - first_step) % tiles_per_seq == 0)
        def _():
            xpad[0:halo, :] = jnp.zeros((halo, e), F32)

        acc = _dot(a_ref[...], w_ref[...])
        g_out[...] = _gelu(acc[:, 0:e]).astype(BF16)
        xpad[halo:halo + tm, :] = acc[:, e:2 * e]
        nk = cw_ref.shape[0]
        xc = cb_ref[...] + cw_ref[nk - 1:nk, :] * xpad[halo:halo + tm, :]
        for k in range(nk - 1):
            off = halo - (nk - 1) + k
            xc = xc + cw_ref[k:k + 1, :] * xpad[off:off + tm, :]
        xpad[0:halo, :] = xpad[tm:tm + halo, :]
        x_out[...] = xc

    return kern


def _lru_in(a, w_all, layer, conv_w, conv_b, *, seq, tm, name):
    rows, k = a.shape
    e = w_all.shape[-1] // 2
    nk = conv_w.shape[0]
    assert nk - 1 <= SUBLANES
    plan = _Resident(w_all, layer, rows, tm, seq)
    return pl.pallas_call(
        plan.kernel(_make_lru_in_kernel(e, tm, seq // tm, plan.nc), 1),
        grid=plan.grid,
        in_specs=[plan.rows(k), plan.weight(), plan.whole((nk, e)), plan.whole((1, e))],
        out_specs=[plan.rows(e), plan.rows(e)],
        out_shape=[jax.ShapeDtypeStruct((rows, e), BF16), jax.ShapeDtypeStruct((rows, e), F32)],
        scratch_shapes=[pltpu.VMEM((tm + SUBLANES, e), F32), plan.scratch],
        compiler_params=_params(("arbitrary",)),
        name=name,
    )(a, w_all, conv_w.reshape(nk, e), conv_b.reshape(1, e))


def _swiglu_kernel(h_ref, wg_ref, wu_ref, o_ref, wg_s, wu_s):
    @pl.when(pl.program_id(1) == 0)
    def _():
        wg_s[...] = wg_ref[...].astype(BF16)
        wu_s[...] = wu_ref[...].astype(BF16)

    h = h_ref[...]
    g = _dot(h, wg_s[...])
    o_ref[...] = (g * jax.nn.sigmoid(g) * _dot(h, wu_s[...])).astype(BF16)


def _swiglu_up(h, w_all, layer, *, tm, tn, name):
    rows, d = h.shape
    n_out = w_all.shape[-1] // 2
    assert rows % tm == 0 and n_out % tn == 0
    nt = n_out // tn
    return pl.pallas_call(
        _swiglu_kernel,
        grid=(nt, rows // tm),
        in_specs=[
            pl.BlockSpec((tm, d), lambda n, m: (m, 0)),
            pl.BlockSpec((None, d, tn), lambda n, m: (layer, 0, n)),
            pl.BlockSpec((None, d, tn), lambda n, m: (layer, 0, n + nt)),
        ],
        out_specs=pl.BlockSpec((tm, tn), lambda n, m: (m, n)),
        out_shape=jax.ShapeDtypeStruct((rows, n_out), BF16),
        scratch_shapes=[pltpu.VMEM((d, tn), BF16), pltpu.VMEM((d, tn), BF16)],
        compiler_params=_params(("parallel", "arbitrary")),
        name=name,
    )(h, w_all, w_all)


def _make_mm_res_kernel(glu, tiled_a, n_out, mode):
    def kern(*refs):
        a_ref, w_ref, res_ref, gate_ref = refs[:4]
        if tiled_a:
            a = jnp.concatenate([a_ref[q] for q in range(a_ref.shape[0])], axis=1)
        else:
            a = a_ref[...]
        acc = _dot(a, w_ref[...])
        if glu:
            acc = acc[:, :n_out] * jax.nn.sigmoid(acc[:, n_out:])
        x = res_ref[...] + gate_ref[0] * acc
        if mode == "final":
            g_ref, o_ref = refs[4:]
            ms = jnp.mean(x * x, axis=-1, keepdims=True)
            o_ref[...] = x * lax.rsqrt(ms + EPS) * g_ref[...]
        elif mode == "x+h":
            g_ref, sc_ref, sh_ref, x_out, h_out = refs[4:]
            x_out[...] = x
            h_out[...] = _rms_mod(x, g_ref[...], sc_ref[0], sh_ref[0]).astype(BF16)
        else:
            (x_out,) = refs[4:]
            x_out[...] = x

    return kern


def _matmul_residual(a, w_all, layer, res2d, gate, *, seq, tm, glu=False, tiled_a=False,
                     next_norm=None, final_gain=None, name):
    rows, n_out = res2d.shape
    k = w_all.shape[1]
    plan = _Resident(w_all, layer, rows, tm, seq)
    if tiled_a:
        a_spec = pl.BlockSpec((k // LANES, tm, LANES), lambda m: (0, plan.tile(m), 0))
    else:
        a_spec = plan.rows(k)
    row_spec = plan.rows(n_out)
    in_specs = [a_spec, plan.weight(), row_spec, plan.mod_vec(gate)]
    args = [a, w_all, res2d, gate[0]]
    x_shape = jax.ShapeDtypeStruct((rows, n_out), F32)
    if final_gain is not None:
        mode = "final"
        in_specs.append(plan.whole((1, n_out)))
        args.append(final_gain.reshape(1, n_out))
        out_specs, out_shape = row_spec, x_shape
    elif next_norm is not None:
        mode = "x+h"
        gain, scale, shift = next_norm
        in_specs += [plan.gain_vec(gain), plan.mod_vec(scale), plan.mod_vec(shift)]
        args += [gain[0], scale[0], shift[0]]
        out_specs = [row_spec, row_spec]
        out_shape = [x_shape, jax.ShapeDtypeStruct((rows, n_out), BF16)]
    else:
        mode = "x"
        out_specs, out_shape = [row_spec], [x_shape]
    out = pl.pallas_call(
        plan.kernel(_make_mm_res_kernel(glu, tiled_a, n_out, mode), 1),
        grid=plan.grid,
        in_specs=in_specs,
        out_specs=out_specs,
        out_shape=out_shape,
        scratch_shapes=[plan.scratch],
        compiler_params=_params(("arbitrary",)),
        name=name,
    )(*args)
    if mode == "final":
        return out
    return (out[0], out[1]) if mode == "x+h" else (out[0], None)


def _make_s5_prep_kernel(tg, p):
    tc = S5_CHUNK
    half = tg * p

    def kern(lr_ref, li_ref, ldt_ref, br_ref, bi_ref, cr_ref, ci_ref,
             wio_ref, wis_ref, wso_ref, tab_ref):
        lr = lr_ref[...]
        li = li_ref[...]
        dt = jnp.exp(ldt_ref[...])
        mag = jnp.exp(lr * dt)
        ab_re = mag * jnp.cos(li * dt)
        ab_im = mag * jnp.sin(li * dt)
        nr, ni = ab_re - 1.0, ab_im
        den = lr * lr + li * li
        f_re = (nr * lr + ni * li) / den
        f_im = (ni * lr - nr * li) / den
        br, bi = br_ref[...], bi_ref[...]
        gc = br.shape[0]
        bb_re = f_re * br - f_im * bi
        bb_im = f_re * bi + f_im * br
        cr, ci = cr_ref[...], ci_ref[...]
        gid = lax.broadcasted_iota(jnp.int32, (1, half), 1) // p

        def expand(v):
            return jnp.concatenate([jnp.where(gid == g, v, 0.0) for g in range(tg)], axis=0)

        pows = [(jnp.ones_like(ab_re), jnp.zeros_like(ab_im))]
        for _ in range(tc):
            pows.append(_cmul(pows[-1][0], pows[-1][1], ab_re, ab_im))

        for t in range(tc):
            pr, pi = pows[tc - 1 - t]
            wr, wi = _cmul(pr, pi, bb_re, bb_im)
            rows = slice(t * gc, (t + 1) * gc)
            wis_ref[rows, 0:half] = wr.astype(BF16)
            wis_ref[rows, half:2 * half] = wi.astype(BF16)

        def split(v):
            hi = v.astype(BF16)
            return hi, (v - hi.astype(F32)).astype(BF16)

        def nt_dot(a, b):
            return lax.dot_general(a, b, (((1,), (1,)), ((), ())), preferred_element_type=F32)

        b_hi, b_lo = split(jnp.concatenate([expand(bb_re), expand(bb_im)], axis=1))
        q_all = []
        for tau in range(tc + 1):
            pr, pi = pows[tau]
            qr, qi = _cmul(cr, ci, pr, pi)
            er, ei = expand(qr), expand(-qi)
            if tau >= 1:
                cols = slice((tau - 1) * LANES, tau * LANES)
                wso_ref[0:half, cols] = er.T.astype(BF16)
                wso_ref[half:2 * half, cols] = ei.T.astype(BF16)
            if tau < tc:
                q_all.append(jnp.concatenate([er, ei], axis=1))
        q_hi, q_lo = split(jnp.concatenate(q_all, axis=0))
        kt = nt_dot(q_hi, b_hi) + (nt_dot(q_lo, b_hi) + nt_dot(q_hi, b_lo))
        for tau in range(tc):
            wio_ref[tau] = kt[tau * LANES:(tau + 1) * LANES, :].T.astype(BF16)

        lam = [(jnp.ones_like(ab_re), jnp.zeros_like(ab_im))]
        for _ in range(SUBLANES):
            lam.append(_cmul(lam[-1][0], lam[-1][1], pows[tc][0], pows[tc][1]))
        zrow = jnp.zeros_like(ab_re)
        groups = [[lam[r] for r in range(SUBLANES)]]
        for d in (1, 2, 4):
            groups.append([lam[d] if r >= d else (zrow, zrow) for r in range(SUBLANES)])
        groups.append([lam[SUBLANES]] * SUBLANES)
        for gi, grp in enumerate(groups):
            rows = slice(gi * SUBLANES, (gi + 1) * SUBLANES)
            tab_ref[rows, 0:half] = jnp.concatenate([v[0] for v in grp], axis=0)
            tab_ref[rows, half:2 * half] = jnp.concatenate([v[1] for v in grp], axis=0)

    return kern


def _s5_operators(lam_re, lam_im, log_dt, b_re, b_im, c_re, c_im, modulation=None):
    g, p = lam_re.shape
    gc = b_re.shape[-1]
    gp = g * p
    tg = S5_TILE_GROUPS
    assert gc * tg == LANES and g % tg == 0
    nj = g // tg
    half = tg * p
    kdim = S5_CHUNK * LANES
    lr = lam_re.reshape(1, gp)
    li = lam_im.reshape(1, gp)
    ldt = jnp.repeat(log_dt, p).reshape(1, gp)
    brt = b_re.transpose(2, 0, 1).reshape(gc, gp)
    bit = b_im.transpose(2, 0, 1).reshape(gc, gp)
    crt = c_re.transpose(1, 0, 2).reshape(gc, gp)
    cit = c_im.transpose(1, 0, 2).reshape(gc, gp)
    prep = _make_s5_prep_kernel(tg, p)
    tile = lambda s: jnp.minimum(s, nj - 1)
    row = pl.BlockSpec((1, half), lambda s: (0, tile(s)))
    mat = pl.BlockSpec((gc, half), lambda s: (0, tile(s)))
    in_specs = [row, row, row, mat, mat, mat, mat]
    args = [lr, li, ldt, brt, bit, crt, cit]
    out_specs = [
        pl.BlockSpec((None, S5_CHUNK, LANES, LANES), lambda s: (tile(s), 0, 0, 0)),
        pl.BlockSpec((None, S5_CHUNK * gc, 2 * half), lambda s: (tile(s), 0, 0)),
        pl.BlockSpec((None, 2 * half, kdim), lambda s: (tile(s), 0, 0)),
        pl.BlockSpec((None, 5 * SUBLANES, 2 * half), lambda s: (tile(s), 0, 0)),
    ]
    out_shape = [
        jax.ShapeDtypeStruct((nj, S5_CHUNK, LANES, LANES), BF16),
        jax.ShapeDtypeStruct((nj, S5_CHUNK * gc, 2 * half), BF16),
        jax.ShapeDtypeStruct((nj, 2 * half, kdim), BF16),
        jax.ShapeDtypeStruct((nj, 5 * SUBLANES, 2 * half), F32),
    ]
    if modulation is None:
        return pl.pallas_call(
            prep, grid=(nj,), in_specs=in_specs, out_specs=out_specs, out_shape=out_shape,
            compiler_params=_params(("parallel",)), name="s5_prep",
        )(*args), None

    cp, w_ada, b_ada3 = modulation
    depth, d, n = w_ada.shape
    rows = cp.shape[0]
    tn = _mod_tile(n)
    nt = n // tn
    nmod = depth * nt
    mtile = lambda s: jnp.minimum(s, nmod - 1)

    def kern(c_ref, w_ref, b_ref, *rest):
        s = pl.program_id(0)

        @pl.when(s < nmod)
        def _():
            _mod_kernel(c_ref, w_ref, b_ref, rest[7])

        @pl.when(s < nj)
        def _():
            prep(*rest[:7], *rest[8:])

    out = pl.pallas_call(
        kern,
        grid=(max(nj, nmod),),
        in_specs=[
            pl.BlockSpec((rows, d), lambda s: (0, 0)),
            pl.BlockSpec((None, d, tn), lambda s: (mtile(s) // nt, 0, mtile(s) % nt)),
            pl.BlockSpec((None, 1, tn), lambda s: (mtile(s) // nt, 0, mtile(s) % nt)),
        ] + in_specs,
        out_specs=[pl.BlockSpec((None, rows, tn), lambda s: (mtile(s) // nt, 0, mtile(s) % nt))]
        + out_specs,
        out_shape=[jax.ShapeDtypeStruct((depth, rows, n), F32)] + out_shape,
        compiler_params=_params(("arbitrary",)),
        name="mod_s5_prep",
    )(cp, w_ada, b_ada3, *args)
    return out[1:], out[0]


def _make_s5_kernel(nb, chunks_per_batch, half, gc):
    tc = S5_CHUNK
    nchunk = nb * chunks_per_batch
    tiles_per_batch = chunks_per_batch // SUBLANES
    ntile = nchunk // SUBLANES

    def kern(u_ref, wio_ref, wis_ref, wso_ref, tab_ref, d_ref, y_ref, s_scr, y_scr, wio_s, wis_s):
        tg = LANES // gc
        gid = (lax.broadcasted_iota(jnp.int32, (1, 2 * half), 1) % half) // (half // tg)
        zero_rows = jnp.zeros((gc, 2 * half), BF16)
        for t in range(tc):
            blk = wis_ref[t * gc:(t + 1) * gc, :]
            for g in range(tg):
                r0 = (t * tg + g) * gc
                wis_s[r0:r0 + gc, :] = jnp.where(gid == g, blk, zero_rows)
        zero_blk = jnp.zeros((LANES, LANES), BF16)
        for t in range(tc):
            for t2 in range(tc):
                wio_s[t * LANES:(t + 1) * LANES, t2 * LANES:(t2 + 1) * LANES] = (
                    wio_ref[t2 - t] if t2 >= t else zero_blk)

        lhs = jnp.concatenate(
            [u_ref[pl.ds(t, nchunk, stride=tc), :].astype(BF16) for t in range(tc)], axis=1)
        s_scr[...] = _dot(lhs, wis_s[...])

        tab = tab_ref[...]
        pw_r, pw_i = tab[0:8, :half], tab[0:8, half:]
        steps = [(d, tab[8 * i:8 * i + 8, :half], tab[8 * i:8 * i + 8, half:])
                 for i, d in ((1, 1), (2, 2), (3, 4))]
        l8_r, l8_i = tab[32:33, :half], tab[32:33, half:]
        first_row = lax.broadcasted_iota(jnp.int32, (SUBLANES, half), 0) == 0

        for m in range(ntile):
            rows = slice(m * SUBLANES, (m + 1) * SUBLANES)
            s = s_scr[rows, :]
            er, ei = s[:, :half], s[:, half:]
            for d, mr, mi in steps:
                rr = pltpu.roll(er, d, axis=0)
                ri = pltpu.roll(ei, d, axis=0)
                er, ei = er + (mr * rr - mi * ri), ei + (mr * ri + mi * rr)
            hr = jnp.where(first_row, 0.0, pltpu.roll(er, 1, axis=0))
            hi = jnp.where(first_row, 0.0, pltpu.roll(ei, 1, axis=0))
            if m % tiles_per_batch == 0:
                cr, ci = er[7:8], ei[7:8]
            else:
                hr = hr + (pw_r * cr - pw_i * ci)
                hi = hi + (pw_r * ci + pw_i * cr)
                cr, ci = er[7:8] + (l8_r * cr - l8_i * ci), ei[7:8] + (l8_r * ci + l8_i * cr)
            s_scr[rows, :] = jnp.concatenate([hr, hi], axis=1)

        y_io = jnp.concatenate(
            [_dot(lhs[:, :(p + 1) * MXU_N], wio_s[0:(p + 1) * MXU_N, p * MXU_N:(p + 1) * MXU_N])
             for p in range(tc * LANES // MXU_N)], axis=1)
        y = y_io + _dot(s_scr[...].astype(BF16), wso_ref[...])
        for t in range(tc):
            y_scr[pl.ds(t, nchunk, stride=tc), :] = y[:, t * LANES:(t + 1) * LANES]
        y_ref[...] = _gelu(y_scr[...] + d_ref[...] * u_ref[...]).astype(BF16)

    return kern


def _s5_core(u3, ops, d_skip, *, nb, seq):
    w_io, w_is, w_so, tab = ops
    nj, rows, _ = u3.shape
    kdim = S5_CHUNK * LANES
    sdim = w_is.shape[-1]
    chunks_per_batch = seq // S5_CHUNK
    gc = w_is.shape[1] // S5_CHUNK
    assert rows == nb * seq and chunks_per_batch % SUBLANES == 0
    tok = pl.BlockSpec((None, rows, LANES), lambda j: (j, 0, 0))
    return pl.pallas_call(
        _make_s5_kernel(nb, chunks_per_batch, sdim // 2, gc),
        grid=(nj,),
        in_specs=[
            tok,
            pl.BlockSpec((None, S5_CHUNK, LANES, LANES), lambda j: (j, 0, 0, 0)),
            pl.BlockSpec((None, S5_CHUNK * gc, sdim), lambda j: (j, 0, 0)),
            pl.BlockSpec((None, sdim, kdim), lambda j: (j, 0, 0)),
            pl.BlockSpec((None, 5 * SUBLANES, sdim), lambda j: (j, 0, 0)),
            pl.BlockSpec((1, LANES), lambda j: (0, j)),
        ],
        out_specs=tok,
        out_shape=jax.ShapeDtypeStruct((nj, rows, LANES), BF16),
        scratch_shapes=[pltpu.VMEM((rows // S5_CHUNK, sdim), F32), pltpu.VMEM((rows, LANES), F32),
                        pltpu.VMEM((kdim, kdim), BF16), pltpu.VMEM((kdim, sdim), BF16)],
        compiler_params=_params(("parallel",)),
        name="s5_core",
    )(u3, w_io, w_is, w_so, tab, d_skip.reshape(1, nj * LANES))


def _band_windows(width, blk, tn):
    starts, ends = [], []
    for n in range(width // tn):
        h_lo = (n * tn) // blk
        h_hi = (n * tn + tn - 1) // blk
        starts.append((h_lo * blk) // LANES)
        ends.append(-(-((h_hi + 1) * blk) // LANES))
    kw = max(e - s for s, e in zip(starts, ends))
    total = width // LANES
    starts = [min(s, total - kw) for s in starts]
    return starts, kw * LANES


def _banded(w, starts, kw, tn):
    blk = w.shape[-1]
    tiles = []
    for n, s in enumerate(starts):
        pieces = []
        for h in range((n * tn) // blk, (n * tn + tn - 1) // blk + 1):
            j0 = max(0, n * tn - h * blk)
            j1 = min(blk, (n + 1) * tn - h * blk)
            r0 = h * blk - s * LANES
            pieces.append(jnp.pad(w[:, h, :, j0:j1], ((0, 0), (r0, kw - r0 - blk), (0, 0))))
        tiles.append(jnp.concatenate(pieces, axis=2))
    return jnp.stack(tiles, axis=1).astype(BF16)


def _make_lru_kernel(e, tm, tn, starts, kw, tiles_per_seq):
    ntile = e // tn
    nrow = tm // SUBLANES

    def kern(g_ref, x_ref, w_ref, vec_ref, y_ref, hcar, xb16):
        @pl.when(pl.program_id(0) % tiles_per_seq == 0)
        def _():
            hcar[...] = jnp.zeros((1, e), F32)

        xb16[...] = x_ref[...].astype(BF16)

        row = lax.broadcasted_iota(jnp.int32, (SUBLANES, tn), 0)
        masks = [row % HALF_TILE >= d for d in (1, 2)]
        in_upper = row < HALF_TILE
        neg = -vec_ref[2:3, :]
        softplus = jnp.maximum(neg, 0.0) + jnp.log1p(jnp.exp(-jnp.abs(neg)))
        rate = (-0.5 * LRU_C * LOG2_E) * softplus
        hb_rg = 0.5 * vec_ref[0:1, :]
        hb_ig = 0.5 * vec_ref[1:2, :]

        for n in range(ntile):
            cols = slice(n * tn, (n + 1) * tn)
            win = xb16[:, starts[n] * LANES:starts[n] * LANES + kw]
            tr = jnp.tanh(_dot(win, w_ref[0, n]) + hb_rg[:, cols])
            ig = 0.5 * jnp.tanh(_dot(win, w_ref[1, n]) + hb_ig[:, cols]) + 0.5
            a = jnp.exp2(rate[:, cols] * tr + rate[:, cols])
            om = 1.0 - a * a
            mult = om * lax.rsqrt(jnp.maximum(om, TINY))
            b = mult * (ig * x_ref[:, cols])

            c = hcar[:, cols]
            hs = []
            for m in range(nrow):
                rows = slice(m * SUBLANES, (m + 1) * SUBLANES)
                av, bv = a[rows], b[rows]
                for d, keep in zip((1, 2), masks):
                    bv = bv + jnp.where(keep, av * pltpu.roll(bv, d, axis=0), 0.0)
                    av = jnp.where(keep, av * pltpu.roll(av, d, axis=0), av)
                upper = bv + av * c
                lower = bv + av * upper[HALF_TILE - 1:HALF_TILE]
                h = jnp.where(in_upper, upper, lower)
                c = h[SUBLANES - 1:SUBLANES]
                hs.append(h)
            hcar[:, cols] = c
            y_ref[:, cols] = (jnp.concatenate(hs, axis=0) * g_ref[:, cols].astype(F32)).astype(BF16)

    return kern


def _lru_core(gg, xc, w_rg, b_rg, w_ig, b_ig, lam, *, seq, tm):
    rows, e = xc.shape
    blk = w_rg.shape[1]
    tn = MXU_N
    assert e % tn == 0 and seq % tm == 0 and tm % SUBLANES == 0
    starts, kw = _band_windows(e, blk, tn)
    gates = _banded(0.5 * jnp.stack([w_rg, w_ig]), starts, kw, tn)
    vecs = jnp.stack([b_rg, b_ig, lam])
    ntile = e // tn
    return pl.pallas_call(
        _make_lru_kernel(e, tm, tn, starts, kw, seq // tm),
        grid=(rows // tm,),
        in_specs=[
            pl.BlockSpec((tm, e), lambda m: (m, 0)),
            pl.BlockSpec((tm, e), lambda m: (m, 0)),
            pl.BlockSpec((2, ntile, kw, tn), lambda m: (0, 0, 0, 0)),
            pl.BlockSpec((3, e), lambda m: (0, 0)),
        ],
        out_specs=pl.BlockSpec((tm, e), lambda m: (m, 0)),
        out_shape=jax.ShapeDtypeStruct((rows, e), BF16),
        scratch_shapes=[pltpu.VMEM((1, e), F32), pltpu.VMEM((tm, e), BF16)],
        compiler_params=_params(("arbitrary",)),
        name="lru_core",
    )(gg, xc, gates, vecs)


def _tile(n, pref):
    t = min(n, pref)
    while n % t:
        t //= 2
    return t


def kernel(x, c, norm_g, w_ada, b_ada, s5_w_in, s5_lam_re, s5_lam_im, s5_log_dt, s5_b_re, s5_b_im, s5_c_re, s5_c_im, s5_d, s5_w_glu, lru_w_in, lru_conv_w, lru_conv_b, lru_w_rg, lru_b_rg, lru_w_ig, lru_b_ig, lru_lam, lru_w_out, ffn_w_gu, ffn_w_down, final_g):
    bsz, seq, d = x.shape
    depth = w_ada.shape[0]
    rows = bsz * seq
    x2 = x.reshape(rows, d)
    s5_ops, mods = _s5_operators(s5_lam_re[0], s5_lam_im[0], s5_log_dt[0], s5_b_re[0], s5_b_im[0],
                                 s5_c_re[0], s5_c_im[0],
                                 modulation=_mod_operands(c, w_ada, b_ada))
    mods = mods.reshape(depth, mods.shape[1], 6, 1, d)
    gains = norm_g.reshape(depth, 2, 1, d)
    tm_in = _tile(seq, 512)
    tm_res = _tile(seq, 256)
    tm_up = _tile(rows, 1024)
    hidden = ffn_w_down.shape[1]

    h = None
    for i in range(depth):
        sh1, sc1, g1, sh2, sc2, g2 = [(mods, i, q) for q in range(6)]
        ffn_norm = ((gains, i, 1), sc2, sh2)
        j = i // 2
        if i % 2 == 0:
            u3 = _ln_matmul_tiled(x2, (gains, i, 0), sc1, sh1, s5_w_in, j, seq=seq, tm=tm_in,
                                  name="s5_in")
            if j > 0:
                s5_ops, _ = _s5_operators(s5_lam_re[j], s5_lam_im[j], s5_log_dt[j], s5_b_re[j],
                                          s5_b_im[j], s5_c_re[j], s5_c_im[j])
            y3 = _s5_core(u3, s5_ops, s5_d[j], nb=bsz, seq=seq)
            x2, h = _matmul_residual(y3, s5_w_glu, j, x2, g1, seq=seq, tm=tm_res, glu=True,
                                     tiled_a=True, next_norm=ffn_norm, name="s5_out")
        else:
            gg, xc = _lru_in(h, lru_w_in, j, lru_conv_w[j], lru_conv_b[j], seq=seq, tm=tm_res,
                             name="lru_in")
            y = _lru_core(gg, xc, lru_w_rg[j], lru_b_rg[j], lru_w_ig[j], lru_b_ig[j], lru_lam[j],
                          seq=seq, tm=_tile(seq, 128))
            x2, h = _matmul_residual(y, lru_w_out, j, x2, g1, seq=seq, tm=tm_res,
                                     next_norm=ffn_norm, name="lru_out")
        act = _swiglu_up(h, ffn_w_gu, i, tm=tm_up, tn=_tile(hidden, 512), name="ffn_up")
        if i + 1 == depth:
            out = _matmul_residual(act, ffn_w_down, i, x2, g2, seq=seq, tm=tm_res, final_gain=final_g,
                                   name="ffn_down")
            return out.reshape(bsz, seq, d)
        nxt = None
        if (i + 1) % 2 == 1:
            nxt = ((gains, i + 1, 0), (mods, i + 1, 1), (mods, i + 1, 0))
        x2, h = _matmul_residual(act, ffn_w_down, i, x2, g2, seq=seq, tm=tm_res, next_norm=nxt,
                                 name="ffn_down")
```
